```python
import math
import jax, jax.numpy as jnp
from jax import lax
import numpy as np

D_MODEL = 1024
BATCH = 2
SEQ = 8192
DEPTH = 1
DEC_BATCH = 16
DEC_SEQ = 32
PAST_LEN = 2048

CHUNK = 64
N_META = 16
D_MIX = D_MODEL
W_SSM = D_MIX // 2
W_ATTN = D_MIX - W_SSM
SSM_P = 16
SSM_G = W_SSM // SSM_P
SSM_N = 64
HEAD_DIM = 64
N_HEADS = W_ATTN // HEAD_DIM
IN_COLS = W_SSM + 3 * W_ATTN + N_HEADS
Q_BLOCK = 128
N_EXPERTS = 32
TOP_K = 4
D_FF = D_MODEL
SWIGLU_LIMIT = 7.0
SWIGLU_ALPHA = 1.702
RMS_EPS = 1e-6
DT_MIN = 1e-3
DT_MAX = 1e-1
LAMBDA_RE_MAX = -1e-4

kernel_name = 'hymba_s5_fox_moe_stream_step'


def rmsnorm(x, g):
    xf = x.astype(jnp.float32)
    inv = lax.rsqrt(jnp.mean(xf * xf, axis=-1, keepdims=True) + RMS_EPS)
    return (xf * inv * g.astype(jnp.float32)).astype(x.dtype)


def split_proj(z):
    o = W_SSM
    u = z[..., :o]
    q = z[..., o:o + W_ATTN]
    k = z[..., o + W_ATTN:o + 2 * W_ATTN]
    v = z[..., o + 2 * W_ATTN:o + 3 * W_ATTN]
    f = z[..., o + 3 * W_ATTN:]
    return u, q, k, v, f


def to_heads(a):
    return a.reshape(a.shape[0], a.shape[1], N_HEADS, HEAD_DIM)


def s5_mixer(u, h0, a_re, a_im, b_re, b_im, c_re, c_im, d_skip, log_dt, w_glu, b_glu):
    bsz, t = u.shape[0], u.shape[1]
    uf = u.astype(jnp.float32).reshape(bsz, t, SSM_G, SSM_P)
    lam = lax.complex(jnp.minimum(a_re.astype(jnp.float32), LAMBDA_RE_MAX), a_im.astype(jnp.float32))
    dt = jnp.exp(log_dt.astype(jnp.float32))[:, None]
    a_bar = jnp.exp(lam * dt)
    b_bar = ((a_bar - 1.0) / lam)[..., None] * lax.complex(b_re.astype(jnp.float32), b_im.astype(jnp.float32))
    bu = jnp.einsum('gnp,btgp->btgn', b_bar, uf.astype(jnp.complex64))
    bu = bu.at[:, 0].add(a_bar[None] * h0)
    a_seq = jnp.broadcast_to(a_bar, bu.shape)

    def combine(e1, e2):
        a1, b1 = e1
        a2, b2 = e2
        return a1 * a2, a2 * b1 + b2

    _, h = lax.associative_scan(combine, (a_seq, bu), axis=1)
    c_mat = lax.complex(c_re.astype(jnp.float32), c_im.astype(jnp.float32))
    y = jnp.einsum('gpn,btgn->btgp', c_mat, h).real + d_skip.astype(jnp.float32) * uf
    y = jax.nn.gelu(y.reshape(bsz, t, W_SSM))
    y = y * jax.nn.sigmoid(y @ w_glu.astype(jnp.float32) + b_glu.astype(jnp.float32))
    return y.astype(u.dtype), h[:, -1]


def fox_attend(q, pos_q, fq, k, v, fk, pos_k):
    s = jnp.einsum('bqhd,bkhd->bhqk', q.astype(jnp.float32), k.astype(jnp.float32)) * (HEAD_DIM ** -0.5)
    bias = jnp.swapaxes(fq, 1, 2)[:, :, :, None] - jnp.swapaxes(fk, 1, 2)[:, :, None, :]
    allowed = (pos_k[None, :] <= pos_q[:, None])[None, None]
    s = jnp.where(allowed, s + bias, -jnp.inf)
    p = jax.nn.softmax(s, axis=-1)
    return jnp.einsum('bhqk,bkhd->bqhd', p, v.astype(jnp.float32))


def fox_prompt(q, k, v, logf):
    bsz, t = q.shape[0], q.shape[1]
    tp = ((t + Q_BLOCK - 1) // Q_BLOCK) * Q_BLOCK
    pad = tp - t

    def pad_t(a):
        return jnp.pad(a, [(0, 0), (0, pad)] + [(0, 0)] * (a.ndim - 2))

    qp, kp, vp = pad_t(q), pad_t(k), pad_t(v)
    fc = pad_t(jnp.cumsum(logf, axis=1))
    pos = jnp.arange(tp, dtype=jnp.int32)
    nb = tp // Q_BLOCK
    qb = qp.reshape(bsz, nb, Q_BLOCK, N_HEADS, HEAD_DIM).transpose(1, 0, 2, 3, 4)
    fb = fc.reshape(bsz, nb, Q_BLOCK, N_HEADS).transpose(1, 0, 2, 3)
    pb = pos.reshape(nb, Q_BLOCK)
    out = lax.map(lambda a: fox_attend(a[0], a[2], a[1], kp, vp, fc, pos), (qb, fb, pb))
    out = out.transpose(1, 0, 2, 3, 4).reshape(bsz, tp, W_ATTN)[:, :t]
    return out.astype(q.dtype)


def fox_sample(q, k, v, logf, cache_k, cache_v, cache_logf):
    past = cache_k.shape[1]
    tq = q.shape[1]
    k_all = jnp.concatenate([cache_k.astype(k.dtype), k], axis=1)
    v_all = jnp.concatenate([cache_v.astype(v.dtype), v], axis=1)
    fc = jnp.cumsum(jnp.concatenate([cache_logf.astype(jnp.float32), logf], axis=1), axis=1)
    pos_k = jnp.arange(past + tq, dtype=jnp.int32)
    out = fox_attend(q, pos_k[past:], fc[:, past:], k_all, v_all, fc, pos_k)
    return out.reshape(q.shape[0], tq, W_ATTN).astype(q.dtype)


def merge_groups(y_ssm, y_attn, g_ssm, g_attn, w_out):
    mix = jnp.concatenate([rmsnorm(y_ssm, g_ssm), rmsnorm(y_attn, g_attn)], axis=-1)
    return mix @ w_out


def moe(x2d, w_router, b_router, w1, b1, w2, b2):
    logits = x2d.astype(jnp.float32) @ w_router.astype(jnp.float32) + b_router.astype(jnp.float32)
    top_v, top_i = lax.top_k(logits, TOP_K)
    top_w = jax.nn.softmax(top_v, axis=-1)
    gates = jnp.einsum('nk,nke->ne', top_w, jax.nn.one_hot(top_i, N_EXPERTS, dtype=jnp.float32))
    out = jnp.zeros(x2d.shape, jnp.float32)
    for e in range(N_EXPERTS):
        h = x2d @ w1[e] + b1[e]
        hg = jnp.minimum(h[:, :D_FF], SWIGLU_LIMIT)
        hl = jnp.clip(h[:, D_FF:], -SWIGLU_LIMIT, SWIGLU_LIMIT)
        act = (hl + 1.0) * (hg * jax.nn.sigmoid(SWIGLU_ALPHA * hg))
        out = out + gates[:, e:e + 1] * (act @ w2[e] + b2[e]).astype(jnp.float32)
    return out.astype(x2d.dtype)


def _normal(k, shape, scale):
    return scale * jax.random.normal(k, shape, jnp.float32)


def setup_inputs(seed: int = 0) -> dict:
    key = jax.random.key(seed)
    ks = jax.random.split(key, 40)
    n_idx = jnp.arange(SSM_N, dtype=jnp.float32)
    return {
        'x_prompt': _normal(ks[0], (BATCH, SEQ, D_MODEL), 1.0),
        'x_sample': _normal(ks[1], (DEC_BATCH, DEC_SEQ, D_MODEL), 1.0),
        'cache_k': _normal(ks[2], (DEPTH, DEC_BATCH, PAST_LEN, N_HEADS, HEAD_DIM), 1.0),
        'cache_v': _normal(ks[3], (DEPTH, DEC_BATCH, PAST_LEN, N_HEADS, HEAD_DIM), 1.0),
        'cache_logf': jax.nn.log_sigmoid(3.0 + _normal(ks[4], (DEPTH, DEC_BATCH, PAST_LEN, N_HEADS), 1.0)),
        'state_ssm_re': _normal(ks[5], (DEPTH, DEC_BATCH, SSM_G, SSM_N), 0.1),
        'state_ssm_im': _normal(ks[6], (DEPTH, DEC_BATCH, SSM_G, SSM_N), 0.1),
        'meta_tokens': _normal(ks[7], (N_META, D_MODEL), 1.0),
        'norm_mix_g': 1.0 + _normal(ks[8], (DEPTH, D_MODEL), 0.01),
        'w_in': _normal(ks[9], (DEPTH, D_MODEL, IN_COLS), D_MODEL ** -0.5),
        'b_forget': 3.0 + _normal(ks[10], (DEPTH, N_HEADS), 0.5),
        'ssm_a_re': -0.5 + _normal(ks[11], (DEPTH, SSM_G, SSM_N), 0.01),
        'ssm_a_im': math.pi * n_idx + _normal(ks[12], (DEPTH, SSM_G, SSM_N), 0.01),
        'ssm_log_dt': jax.random.uniform(ks[13], (DEPTH, SSM_G), jnp.float32, math.log(DT_MIN), math.log(DT_MAX)),
        'ssm_b_re': _normal(ks[14], (DEPTH, SSM_G, SSM_N, SSM_P), (2 * SSM_P) ** -0.5),
        'ssm_b_im': _normal(ks[15], (DEPTH, SSM_G, SSM_N, SSM_P), (2 * SSM_P) ** -0.5),
        'ssm_c_re': _normal(ks[16], (DEPTH, SSM_G, SSM_P, SSM_N), SSM_N ** -0.5),
        'ssm_c_im': _normal(ks[17], (DEPTH, SSM_G, SSM_P, SSM_N), SSM_N ** -0.5),
        'ssm_d': _normal(ks[18], (DEPTH, SSM_G, SSM_P), 1.0),
        'w_glu': _normal(ks[19], (DEPTH, W_SSM, W_SSM), W_SSM ** -0.5),
        'b_glu': _normal(ks[20], (DEPTH, W_SSM), 0.01),
        'g_out_ssm': 1.0 + _normal(ks[21], (DEPTH, W_SSM), 0.01),
        'g_out_attn': 1.0 + _normal(ks[22], (DEPTH, W_ATTN), 0.01),
        'w_out': _normal(ks[23], (DEPTH, D_MIX, D_MODEL), D_MIX ** -0.5),
        'norm_ffn_g': 1.0 + _normal(ks[24], (DEPTH, D_MODEL), 0.01),
        'w_router': _normal(ks[25], (DEPTH, D_MODEL, N_EXPERTS), D_MODEL ** -0.5),
        'b_router': _normal(ks[26], (DEPTH, N_EXPERTS), 0.01),
        'w_mlp1': _normal(ks[27], (DEPTH, N_EXPERTS, D_MODEL, 2 * D_FF), D_MODEL ** -0.5),
        'b_mlp1': _normal(ks[28], (DEPTH, N_EXPERTS, 2 * D_FF), 0.01),
        'w_mlp2': _normal(ks[29], (DEPTH, N_EXPERTS, D_FF, D_MODEL), D_FF ** -0.5),
        'b_mlp2': _normal(ks[30], (DEPTH, N_EXPERTS, D_MODEL), 0.01),
        'norm_final_g': 1.0 + _normal(ks[31], (D_MODEL,), 0.01),
    }


def reference(x_prompt, x_sample, cache_k, cache_v, cache_logf, state_ssm_re, state_ssm_im,
              meta_tokens, norm_mix_g, w_in, b_forget, ssm_a_re, ssm_a_im, ssm_log_dt,
              ssm_b_re, ssm_b_im, ssm_c_re, ssm_c_im, ssm_d, w_glu, b_glu, g_out_ssm, g_out_attn,
              w_out, norm_ffn_g, w_router, b_router, w_mlp1, b_mlp1, w_mlp2, b_mlp2, norm_final_g):
    bp = x_prompt.shape[0]
    meta = jnp.broadcast_to(meta_tokens.astype(x_prompt.dtype)[None], (bp, N_META, D_MODEL))
    xp = jnp.concatenate([meta, x_prompt], axis=1)
    xs = x_sample
    kp_l, vp_l, fp_l, rp_l, ip_l = [], [], [], [], []
    ks_l, vs_l, fs_l, rs_l, is_l = [], [], [], [], []
    for l in range(DEPTH):
        up, qp, kp, vp, flp = split_proj(rmsnorm(xp, norm_mix_g[l]) @ w_in[l])
        us, qs, ks, vs, fls = split_proj(rmsnorm(xs, norm_mix_g[l]) @ w_in[l])
        lfp = jax.nn.log_sigmoid(flp.astype(jnp.float32) + b_forget[l].astype(jnp.float32))
        lfs = jax.nn.log_sigmoid(fls.astype(jnp.float32) + b_forget[l].astype(jnp.float32))
        ssm_params = (ssm_a_re[l], ssm_a_im[l], ssm_b_re[l], ssm_b_im[l], ssm_c_re[l], ssm_c_im[l],
                      ssm_d[l], ssm_log_dt[l], w_glu[l], b_glu[l])
        h0p = jnp.zeros((bp, SSM_G, SSM_N), jnp.complex64)
        h0s = lax.complex(state_ssm_re[l].astype(jnp.float32), state_ssm_im[l].astype(jnp.float32))
        ysp, hTp = s5_mixer(up, h0p, *ssm_params)
        yss, hTs = s5_mixer(us, h0s, *ssm_params)
        kp4, vp4, ks4, vs4 = to_heads(kp), to_heads(vp), to_heads(ks), to_heads(vs)
        yap = fox_prompt(to_heads(qp), kp4, vp4, lfp)
        yas = fox_sample(to_heads(qs), ks4, vs4, lfs, cache_k[l], cache_v[l], cache_logf[l])
        xp = xp + merge_groups(ysp, yap, g_out_ssm[l], g_out_attn[l], w_out[l])
        xs = xs + merge_groups(yss, yas, g_out_ssm[l], g_out_attn[l], w_out[l])
        n_p = xp.shape[0] * xp.shape[1]
        h2 = jnp.concatenate([rmsnorm(xp, norm_ffn_g[l]).reshape(n_p, D_MODEL),
                              rmsnorm(xs, norm_ffn_g[l]).reshape(-1, D_MODEL)], axis=0)
        f = moe(h2, w_router[l], b_router[l], w_mlp1[l], b_mlp1[l], w_mlp2[l], b_mlp2[l])
        xp = xp + f[:n_p].reshape(xp.shape)
        xs = xs + f[n_p:].reshape(xs.shape)
        kp_l.append(kp4); vp_l.append(vp4); fp_l.append(lfp)
        rp_l.append(hTp.real); ip_l.append(hTp.imag)
        ks_l.append(ks4); vs_l.append(vs4); fs_l.append(lfs)
        rs_l.append(hTs.real); is_l.append(hTs.imag)
    y_prompt = rmsnorm(xp, norm_final_g)[:, N_META:]
    y_sample = rmsnorm(xs, norm_final_g)
    k_new_prompt = jnp.stack(kp_l, axis=0)
    v_new_prompt = jnp.stack(vp_l, axis=0)
    logf_new_prompt = jnp.stack(fp_l, axis=0)
    ssm_re_prompt = jnp.stack(rp_l, axis=0)
    ssm_im_prompt = jnp.stack(ip_l, axis=0)
    k_new_sample = jnp.stack(ks_l, axis=0)
    v_new_sample = jnp.stack(vs_l, axis=0)
    logf_new_sample = jnp.stack(fs_l, axis=0)
    ssm_re_sample = jnp.stack(rs_l, axis=0)
    ssm_im_sample = jnp.stack(is_l, axis=0)
    return (y_prompt, y_sample, k_new_prompt, v_new_prompt, logf_new_prompt, ssm_re_prompt, ssm_im_prompt,
            k_new_sample, v_new_sample, logf_new_sample, ssm_re_sample, ssm_im_sample)
```

```python
import functools
import math

import jax
import jax.numpy as jnp
from jax import lax
from jax.experimental import pallas as pl
from jax.experimental.pallas import tpu as pltpu

F32 = jnp.float32
BF16 = jnp.bfloat16

LANES = 128
VMEM_LIMIT_BYTES = 56 * 1024 * 1024

N_META = 16
HEAD_DIM = 64
SSM_P = 16
SSM_N = 64
TOP_K = 4
SWIGLU_LIMIT = 7.0
SWIGLU_ALPHA = 1.702
RMS_EPS = 1e-6
LAMBDA_RE_MAX = -1e-4

ROW_TILE = 512
EXPERT_TILE = 256
GROUP_SLAB = 8
DMA_GROUP = 64


def _cparams(sem):
    return pltpu.CompilerParams(dimension_semantics=sem, vmem_limit_bytes=VMEM_LIMIT_BYTES)


def _round_up(x, m):
    return (x + m - 1) // m * m


def _in_proj_kernel(x_ref, g_ref, w_ref, wf_ref, bf_ref,
                    u_ref, k_ref, v_ref, lf_ref, qh_ref, kh_ref, vh_ref, fc_ref, carry_ref,
                    *, t_valid, tm, n_heads, w_ssm, w_attn):
    i = pl.program_id(1)

    @pl.when(i == 0)
    def _():
        carry_ref[...] = jnp.zeros_like(carry_ref)

    x = x_ref[...]
    inv = lax.rsqrt(jnp.mean(x * x, axis=-1, keepdims=True) + RMS_EPS)
    h = (x * inv * g_ref[...]).astype(BF16)
    rows = i * tm + lax.broadcasted_iota(jnp.int32, (tm, 1), 0)
    valid = rows < t_valid
    z = jnp.where(valid, jnp.dot(h, w_ref[...], preferred_element_type=F32), 0.0)
    u_ref[...] = z[:, :w_ssm]
    q = z[:, w_ssm:w_ssm + w_attn] * (HEAD_DIM ** -0.5)
    k = z[:, w_ssm + w_attn:w_ssm + 2 * w_attn]
    v = z[:, w_ssm + 2 * w_attn:w_ssm + 3 * w_attn]
    k_ref[...] = k
    v_ref[...] = v
    for hh in range(n_heads):
        sl = slice(hh * HEAD_DIM, (hh + 1) * HEAD_DIM)
        qh_ref[hh] = q[:, sl].astype(BF16)
        kh_ref[hh] = k[:, sl].astype(BF16)
        vh_ref[hh] = v[:, sl].astype(BF16)
    zf = jnp.dot(h, wf_ref[...], preferred_element_type=F32) + bf_ref[...]
    lf = jnp.where(valid, jnp.minimum(zf, 0.0) - jnp.log1p(jnp.exp(-jnp.abs(zf))), 0.0)
    lf_ref[...] = lf[:, :n_heads]
    r = lax.broadcasted_iota(jnp.int32, (tm, tm), 0)
    c = lax.broadcasted_iota(jnp.int32, (tm, tm), 1)
    tri = (c <= r).astype(F32)
    fc = jnp.dot(tri, lf, preferred_element_type=F32, precision=lax.Precision.HIGHEST) + carry_ref[0:1, :]
    fc_ref[...] = fc
    carry_ref[0:1, :] = fc[tm - 1:tm, :]


def _in_proj(x, g, w_main, w_f, b_f, *, n_heads, w_ssm, w_attn, tm):
    bsz, t, d = x.shape
    nt = pl.cdiv(t, tm)
    tp = nt * tm
    row = lambda b, i: (b, i, 0)
    const = lambda b, i: (0, 0)
    head = lambda b, i: (b, 0, i, 0)
    kern = functools.partial(_in_proj_kernel, t_valid=t, tm=tm, n_heads=n_heads, w_ssm=w_ssm, w_attn=w_attn)
    return pl.pallas_call(
        kern,
        grid=(bsz, nt),
        in_specs=[
            pl.BlockSpec((None, tm, d), row),
            pl.BlockSpec((1, d), const),
            pl.BlockSpec(w_main.shape, const),
            pl.BlockSpec(w_f.shape, const),
            pl.BlockSpec((1, LANES), const),
        ],
        out_specs=[
            pl.BlockSpec((None, tm, w_ssm), row),
            pl.BlockSpec((None, tm, w_attn), row),
            pl.BlockSpec((None, tm, w_attn), row),
            pl.BlockSpec((None, tm, n_heads), row),
            pl.BlockSpec((None, n_heads, tm, HEAD_DIM), head),
            pl.BlockSpec((None, n_heads, tm, HEAD_DIM), head),
            pl.BlockSpec((None, n_heads, tm, HEAD_DIM), head),
            pl.BlockSpec((None, tm, LANES), row),
        ],
        out_shape=[
            jax.ShapeDtypeStruct((bsz, t, w_ssm), F32),
            jax.ShapeDtypeStruct((bsz, t, w_attn), F32),
            jax.ShapeDtypeStruct((bsz, t, w_attn), F32),
            jax.ShapeDtypeStruct((bsz, t, n_heads), F32),
            jax.ShapeDtypeStruct((bsz, n_heads, tp, HEAD_DIM), BF16),
            jax.ShapeDtypeStruct((bsz, n_heads, tp, HEAD_DIM), BF16),
            jax.ShapeDtypeStruct((bsz, n_heads, tp, HEAD_DIM), BF16),
            jax.ShapeDtypeStruct((bsz, tp, LANES), F32),
        ],
        scratch_shapes=[pltpu.VMEM((8, LANES), F32)],
        compiler_params=_cparams(("arbitrary", "arbitrary")),
        name="in_proj",
    )(x, g, w_main, w_f, b_f)


def _s5_param_kernel(are_ref, aim_ref, ldt_ref, bre_ref, bim_ref, abr_ref, abi_ref, bbr_ref, bbi_ref):
    lam_re = jnp.minimum(are_ref[...], LAMBDA_RE_MAX)
    lam_im = aim_ref[...]
    dt = jnp.exp(ldt_ref[...])
    mag = jnp.exp(lam_re * dt)
    ab_re = mag * jnp.cos(lam_im * dt)
    ab_im = mag * jnp.sin(lam_im * dt)
    abr_ref[...] = ab_re
    abi_ref[...] = ab_im
    nr = ab_re - 1.0
    ni = ab_im
    den = lam_re * lam_re + lam_im * lam_im
    cr = (nr * lam_re + ni * lam_im) / den
    ci = (ni * lam_re - nr * lam_im) / den
    b_re = bre_ref[...]
    b_im = bim_ref[...]
    bbr_ref[...] = cr * b_re - ci * b_im
    bbi_ref[...] = cr * b_im + ci * b_re


def _s5_params(a_re, a_im, log_dt, b_re, b_im):
    g, n = a_re.shape
    p = b_re.shape[-1]
    gn = g * n
    flat = lambda a: a.reshape(1, gn)
    ldt = jnp.broadcast_to(log_dt[:, None], (g, n)).reshape(1, gn)
    bt = lambda b: jnp.transpose(b, (2, 0, 1)).reshape(p, gn)
    return pl.pallas_call(
        _s5_param_kernel,
        out_shape=[jax.ShapeDtypeStruct((1, gn), F32), jax.ShapeDtypeStruct((1, gn), F32),
                   jax.ShapeDtypeStruct((p, gn), F32), jax.ShapeDtypeStruct((p, gn), F32)],
        name="s5_params",
    )(flat(a_re), flat(a_im), ldt, bt(b_re), bt(b_im))


def _s5_block_mats(bb_re, bb_im, c_re, c_im, n_groups):
    p = bb_re.shape[0]
    n = bb_re.shape[1] // n_groups
    s = n_groups // GROUP_SLAB
    eye = jnp.eye(GROUP_SLAB, dtype=F32)

    def in_blk(bb):
        b4 = bb.reshape(p, s, GROUP_SLAB, n)
        return jnp.einsum('qsgn,gh->sgqhn', b4, eye)

    b_mat = jnp.concatenate([in_blk(bb_re).reshape(s, GROUP_SLAB * p, GROUP_SLAB * n),
                             in_blk(bb_im).reshape(s, GROUP_SLAB * p, GROUP_SLAB * n)], axis=-1)

    def out_blk(c):
        c4 = c.reshape(s, GROUP_SLAB, p, n)
        return jnp.einsum('sgpn,gh->sgnhp', c4, eye)

    c_mat = jnp.concatenate([out_blk(c_re).reshape(s, GROUP_SLAB * n, GROUP_SLAB * p),
                             out_blk(-c_im).reshape(s, GROUP_SLAB * n, GROUP_SLAB * p)], axis=1)
    return b_mat.astype(BF16), c_mat.astype(BF16)


def _s5_kernel(u_ref, h0r_ref, h0i_ref, abr_ref, abi_ref, bm_ref, cm_ref, d_ref, wg_ref, bg_ref,
               y_ref, hr_out_ref, hi_out_ref, sre_ref, sim_ref, hre_ref, him_ref,
               *, t_valid, tm, n_slabs, slab_in, slab_state):
    i = pl.program_id(1)
    nt = pl.num_programs(1)

    @pl.when(i == 0)
    def _():
        hre_ref[...] = h0r_ref[...]
        him_ref[...] = h0i_ref[...]

    u = u_ref[...]
    ub = u.astype(BF16)
    for s in range(n_slabs):
        z = jnp.dot(ub[:, s * slab_in:(s + 1) * slab_in], bm_ref[s], preferred_element_type=F32)
        sre_ref[:, s * slab_state:(s + 1) * slab_state] = z[:, :slab_state]
        sim_ref[:, s * slab_state:(s + 1) * slab_state] = z[:, slab_state:]

    a_re = abr_ref[...]
    a_im = abi_ref[...]
    last_row = (t_valid - 1) % tm

    def step(t, carry):
        h_re, h_im = carry
        b_re = sre_ref[pl.ds(t, 1), :]
        b_im = sim_ref[pl.ds(t, 1), :]
        n_re = a_re * h_re - a_im * h_im + b_re
        n_im = a_re * h_im + a_im * h_re + b_im
        sre_ref[pl.ds(t, 1), :] = n_re
        sim_ref[pl.ds(t, 1), :] = n_im
        return n_re, n_im

    h_re, h_im = lax.fori_loop(0, tm, step, (hre_ref[...], him_ref[...]))
    hre_ref[...] = h_re
    him_ref[...] = h_im

    @pl.when(i == nt - 1)
    def _():
        hr_out_ref[...] = sre_ref[last_row:last_row + 1, :]
        hi_out_ref[...] = sim_ref[last_row:last_row + 1, :]

    ys = []
    for s in range(n_slabs):
        sl = slice(s * slab_state, (s + 1) * slab_state)
        cm = cm_ref[s]
        y = jnp.dot(sre_ref[:, sl].astype(BF16), cm[:slab_state], preferred_element_type=F32)
        y = y + jnp.dot(sim_ref[:, sl].astype(BF16), cm[slab_state:], preferred_element_type=F32)
        ys.append(y)
    y = jnp.concatenate(ys, axis=-1) + d_ref[...] * u
    y = jax.nn.gelu(y)
    gate = jnp.dot(y.astype(BF16), wg_ref[...], preferred_element_type=F32) + bg_ref[...]
    y_ref[...] = y * jax.nn.sigmoid(gate)


def _s5_mixer(u, h0_re, h0_im, ab_re, ab_im, b_mat, c_mat, d_skip, w_glu, b_glu, *, tm):
    bsz, t, w = u.shape
    gn = ab_re.shape[-1]
    n_slabs = b_mat.shape[0]
    nt = pl.cdiv(t, tm)
    row = lambda b, i: (b, i, 0)
    st = lambda b, i: (b, 0, 0)
    c2 = lambda b, i: (0, 0)
    c3 = lambda b, i: (0, 0, 0)
    kern = functools.partial(_s5_kernel, t_valid=t, tm=tm, n_slabs=n_slabs,
                             slab_in=w // n_slabs, slab_state=gn // n_slabs)
    return pl.pallas_call(
        kern,
        grid=(bsz, nt),
        in_specs=[
            pl.BlockSpec((None, tm, w), row),
            pl.BlockSpec((None, 1, gn), st),
            pl.BlockSpec((None, 1, gn), st),
            pl.BlockSpec((1, gn), c2),
            pl.BlockSpec((1, gn), c2),
            pl.BlockSpec(b_mat.shape, c3),
            pl.BlockSpec(c_mat.shape, c3),
            pl.BlockSpec((1, w), c2),
            pl.BlockSpec((w, w), c2),
            pl.BlockSpec((1, w), c2),
        ],
        out_specs=[
            pl.BlockSpec((None, tm, w), row),
            pl.BlockSpec((None, 1, gn), st),
            pl.BlockSpec((None, 1, gn), st),
        ],
        out_shape=[
            jax.ShapeDtypeStruct((bsz, t, w), F32),
            jax.ShapeDtypeStruct((bsz, 1, gn), F32),
            jax.ShapeDtypeStruct((bsz, 1, gn), F32),
        ],
        scratch_shapes=[pltpu.VMEM((tm, gn), F32), pltpu.VMEM((tm, gn), F32),
                        pltpu.VMEM((1, gn), F32), pltpu.VMEM((1, gn), F32)],
        compiler_params=_cparams(("arbitrary", "arbitrary")),
        name="s5_mixer",
    )(u, h0_re, h0_im, ab_re, ab_im, b_mat, c_mat, d_skip, w_glu, b_glu)


def _fox_prompt_kernel(qa_ref, qb_ref, ka_ref, kb_ref, va_ref, vb_ref, fca_ref, fcb_ref, fra_ref, frb_ref,
                       o_ref, m_ref, l_ref, acc_ref, *, tq):
    i = pl.program_id(2)
    outs = []
    for q_ref, k_ref, v_ref, fc_ref, fr_ref in ((qa_ref, ka_ref, va_ref, fca_ref, fra_ref),
                                                (qb_ref, kb_ref, vb_ref, fcb_ref, frb_ref)):
        q = q_ref[...]
        fq = fc_ref[...]
        m_ref[...] = jnp.full_like(m_ref, -jnp.inf)
        l_ref[...] = jnp.zeros_like(l_ref)
        acc_ref[...] = jnp.zeros_like(acc_ref)

        def block(j, masked):
            start = pl.multiple_of(j * tq, tq)
            kj = k_ref[pl.ds(start, tq), :]
            vj = v_ref[pl.ds(start, tq), :]
            s = lax.dot_general(q, kj, (((1,), (1,)), ((), ())), preferred_element_type=F32)
            s = s + (fq - fr_ref[pl.ds(j, 1), :])
            if masked:
                r = lax.broadcasted_iota(jnp.int32, (tq, tq), 0)
                c = lax.broadcasted_iota(jnp.int32, (tq, tq), 1)
                s = jnp.where(c <= r, s, -jnp.inf)
            m_old = m_ref[...]
            m_new = jnp.maximum(m_old, jnp.max(s, axis=-1, keepdims=True))
            p = jnp.exp(s - m_new)
            alpha = jnp.exp(m_old - m_new)
            l_ref[...] = alpha * l_ref[...] + jnp.sum(p, axis=-1, keepdims=True)
            acc_ref[...] = alpha * acc_ref[...] + jnp.dot(p.astype(BF16), vj, preferred_element_type=F32)
            m_ref[...] = m_new

        def body(j, c):
            block(j, False)
            return c

        lax.fori_loop(0, i, body, 0)
        block(i, True)
        outs.append(acc_ref[...] / l_ref[...])
    o_ref[...] = jnp.concatenate(outs, axis=-1)


def _fox_prompt(qh, kh, vh, fcol, frow, *, tq):
    bsz, nh, tp, hd = qh.shape
    nq = tp // tq
    qa = lambda b, p, i: (b, 2 * p, i, 0)
    qb = lambda b, p, i: (b, 2 * p + 1, i, 0)
    fa = lambda b, p, i: (b, 2 * p, 0, 0)
    fb = lambda b, p, i: (b, 2 * p + 1, 0, 0)
    q_spec = lambda im: pl.BlockSpec((None, None, tq, hd), im)
    kv_spec = lambda im: pl.BlockSpec((None, None, tp, hd), im)
    return pl.pallas_call(
        functools.partial(_fox_prompt_kernel, tq=tq),
        grid=(bsz, nh // 2, nq),
        in_specs=[q_spec(qa), q_spec(qb), kv_spec(fa), kv_spec(fb), kv_spec(fa), kv_spec(fb),
                  pl.BlockSpec((None, None, tq, 1), qa), pl.BlockSpec((None, None, tq, 1), qb),
                  pl.BlockSpec((None, None, nq, tq), fa), pl.BlockSpec((None, None, nq, tq), fb)],
        out_specs=pl.BlockSpec((None, tq, 2 * hd), lambda b, p, i: (b, i, p)),
        out_shape=jax.ShapeDtypeStruct((bsz, tp, nh * hd), F32),
        scratch_shapes=[pltpu.VMEM((tq, 1), F32), pltpu.VMEM((tq, 1), F32), pltpu.VMEM((tq, hd), F32)],
        compiler_params=_cparams(("arbitrary", "arbitrary", "arbitrary")),
        name="fox_prompt",
    )(qh, qh, kh, kh, vh, vh, fcol, fcol, frow, frow)


def _fox_sample_kernel(q_ref, kn_ref, vn_ref, ck_ref, cv_ref, fq_ref, frc_ref, frn_ref, o_ref, *, n_heads):
    tq = q_ref.shape[1]
    r = lax.broadcasted_iota(jnp.int32, (tq, tq), 0)
    c = lax.broadcasted_iota(jnp.int32, (tq, tq), 1)
    outs = []
    dn = (((1,), (1,)), ((), ()))
    for hh in range(n_heads):
        sl = slice(hh * HEAD_DIM, (hh + 1) * HEAD_DIM)
        q = q_ref[hh]
        kc = ck_ref[:, sl].astype(BF16)
        vc = cv_ref[:, sl].astype(BF16)
        fq = fq_ref[hh]
        s_c = lax.dot_general(q, kc, dn, preferred_element_type=F32) + (fq - frc_ref[hh])
        s_n = lax.dot_general(q, kn_ref[hh], dn, preferred_element_type=F32) + (fq - frn_ref[hh])
        s_n = jnp.where(c <= r, s_n, -jnp.inf)
        m = jnp.maximum(jnp.max(s_c, axis=-1, keepdims=True), jnp.max(s_n, axis=-1, keepdims=True))
        p_c = jnp.exp(s_c - m)
        p_n = jnp.exp(s_n - m)
        den = jnp.sum(p_c, axis=-1, keepdims=True) + jnp.sum(p_n, axis=-1, keepdims=True)
        num = jnp.dot(p_c.astype(BF16), vc, preferred_element_type=F32)
        num = num + jnp.dot(p_n.astype(BF16), vn_ref[hh], preferred_element_type=F32)
        outs.append(num / den)
    o_ref[...] = jnp.concatenate(outs, axis=-1)


def _fox_sample(qh, kh, vh, cache_k, cache_v, fq, fr_cache, fr_new):
    bsz, nh, tq, hd = qh.shape
    past = cache_k.shape[1]
    w = nh * hd
    b4 = lambda b: (b, 0, 0, 0)
    b3 = lambda b: (b, 0, 0)
    return pl.pallas_call(
        functools.partial(_fox_sample_kernel, n_heads=nh),
        grid=(bsz,),
        in_specs=[pl.BlockSpec((None, nh, tq, hd), b4), pl.BlockSpec((None, nh, tq, hd), b4),
                  pl.BlockSpec((None, nh, tq, hd), b4),
                  pl.BlockSpec((None, past, w), b3), pl.BlockSpec((None, past, w), b3),
                  pl.BlockSpec((None, nh, tq, 1), b4), pl.BlockSpec((None, nh, 1, past), b4),
                  pl.BlockSpec((None, nh, 1, tq), b4)],
        out_specs=pl.BlockSpec((None, tq, w), b3),
        out_shape=jax.ShapeDtypeStruct((bsz, tq, w), F32),
        compiler_params=_cparams(("arbitrary",)),
        name="fox_sample",
    )(qh, kh, vh, cache_k, cache_v, fq, fr_cache, fr_new)


def _cumsum_kernel(x_ref, o_ref, carry_ref, *, tm):
    i = pl.program_id(1)

    @pl.when(i == 0)
    def _():
        carry_ref[...] = jnp.zeros_like(carry_ref)

    r = lax.broadcasted_iota(jnp.int32, (tm, tm), 0)
    c = lax.broadcasted_iota(jnp.int32, (tm, tm), 1)
    tri = (c <= r).astype(F32)
    fc = jnp.dot(tri, x_ref[...], preferred_element_type=F32, precision=lax.Precision.HIGHEST) + carry_ref[0:1, :]
    o_ref[...] = fc
    carry_ref[0:1, :] = fc[tm - 1:tm, :]


def _cumsum_time(x, *, tm):
    bsz, t, w = x.shape
    row = lambda b, i: (b, i, 0)
    return pl.pallas_call(
        functools.partial(_cumsum_kernel, tm=tm),
        grid=(bsz, t // tm),
        in_specs=[pl.BlockSpec((None, tm, w), row)],
        out_specs=pl.BlockSpec((None, tm, w), row),
        out_shape=jax.ShapeDtypeStruct((bsz, t, w), F32),
        scratch_shapes=[pltpu.VMEM((8, w), F32)],
        compiler_params=_cparams(("arbitrary", "arbitrary")),
        name="cumsum_time",
    )(x)


def _merge_router_kernel(x_ref, ys_ref, ya_ref, gs_ref, ga_ref, wo_ref, gf_ref, wr_ref, br_ref, cnt_in_ref,
                         x2_ref, h2_ref, ti_ref, tw_ref, tp_ref, cnt_out_ref, cnt_ref, *, t_valid, tm):
    first = (pl.program_id(0) == 0) & (pl.program_id(1) == 0)

    @pl.when(first)
    def _():
        cnt_ref[...] = jnp.broadcast_to(cnt_in_ref[...], cnt_ref.shape)

    def rms(v, g):
        return v * lax.rsqrt(jnp.mean(v * v, axis=-1, keepdims=True) + RMS_EPS) * g

    mix = jnp.concatenate([rms(ys_ref[...], gs_ref[...]), rms(ya_ref[...], ga_ref[...])], axis=-1)
    x2 = x_ref[...] + jnp.dot(mix.astype(BF16), wo_ref[...], preferred_element_type=F32)
    x2_ref[...] = x2
    h2 = rms(x2, gf_ref[...])
    h2_ref[...] = h2
    logits = jnp.dot(h2, wr_ref[...], preferred_element_type=F32, precision=lax.Precision.HIGHEST) + br_ref[...]

    lane = lax.broadcasted_iota(jnp.int32, (tm, LANES), 1)
    rows = pl.program_id(1) * tm + lax.broadcasted_iota(jnp.int32, (tm, 1), 0)
    valid = rows < t_valid
    vals = logits
    top_v, top_i, sels = [], [], []
    for _ in range(TOP_K):
        mx = jnp.max(vals, axis=-1, keepdims=True)
        idx = jnp.min(jnp.where(vals == mx, lane, LANES), axis=-1, keepdims=True)
        sel = lane == idx
        vals = jnp.where(sel, -jnp.inf, vals)
        top_v.append(mx)
        top_i.append(idx)
        sels.append(sel)
    ex = [jnp.exp(v - top_v[0]) for v in top_v]
    den = ex[0] + ex[1] + ex[2] + ex[3]

    member = jnp.zeros((tm, LANES), F32)
    for sel in sels:
        member = member + jnp.where(valid, jnp.where(sel, 1.0, 0.0), 0.0)
    r = lax.broadcasted_iota(jnp.int32, (tm, tm), 0)
    c = lax.broadcasted_iota(jnp.int32, (tm, tm), 1)
    tri = (c < r).astype(BF16)
    before = jnp.dot(tri, member.astype(BF16), preferred_element_type=F32) + cnt_ref[0:1, :]
    cnt_new = cnt_ref[0:1, :] + jnp.sum(member, axis=0, keepdims=True)
    cnt_ref[0:1, :] = cnt_new
    cnt_out_ref[...] = cnt_new

    ti = jnp.zeros((tm, LANES), jnp.int32)
    tw = jnp.zeros((tm, LANES), F32)
    tpos = jnp.zeros((tm, LANES), F32)
    for kk in range(TOP_K):
        pos = jnp.sum(jnp.where(sels[kk], before, 0.0), axis=-1, keepdims=True)
        ti = jnp.where(lane == kk, top_i[kk], ti)
        tw = jnp.where(lane == kk, ex[kk] / den, tw)
        tpos = jnp.where(lane == kk, pos, tpos)
    ti_ref[...] = ti
    tw_ref[...] = tw
    tp_ref[...] = tpos.astype(jnp.int32)


def _merge_router(x, ys, ya, g_ssm, g_attn, w_out, g_ffn, w_router, b_router, cnt_in, *, tm):
    bsz, t, d = x.shape
    w = ys.shape[-1]
    nt = pl.cdiv(t, tm)
    row = lambda b, i: (b, i, 0)
    c2 = lambda b, i: (0, 0)
    return pl.pallas_call(
        functools.partial(_merge_router_kernel, t_valid=t, tm=tm),
        grid=(bsz, nt),
        in_specs=[
            pl.BlockSpec((None, tm, d), row),
            pl.BlockSpec((None, tm, w), row),
            pl.BlockSpec((None, tm, w), row),
            pl.BlockSpec((1, w), c2), pl.BlockSpec((1, w), c2),
            pl.BlockSpec(w_out.shape, c2),
            pl.BlockSpec((1, d), c2),
            pl.BlockSpec(w_router.shape, c2),
            pl.BlockSpec((1, LANES), c2),
            pl.BlockSpec((1, LANES), c2),
        ],
        out_specs=[
            pl.BlockSpec((None, tm, d), row),
            pl.BlockSpec((None, tm, d), row),
            pl.BlockSpec((None, tm, LANES), row),
            pl.BlockSpec((None, tm, LANES), row),
            pl.BlockSpec((None, tm, LANES), row),
            pl.BlockSpec((1, LANES), c2),
        ],
        out_shape=[
            jax.ShapeDtypeStruct((bsz, t, d), F32),
            jax.ShapeDtypeStruct((bsz, t, d), F32),
            jax.ShapeDtypeStruct((bsz, t, LANES), jnp.int32),
            jax.ShapeDtypeStruct((bsz, t, LANES), F32),
            jax.ShapeDtypeStruct((bsz, t, LANES), jnp.int32),
            jax.ShapeDtypeStruct((1, LANES), F32),
        ],
        scratch_shapes=[pltpu.VMEM((8, LANES), F32)],
        compiler_params=_cparams(("arbitrary", "arbitrary")),
        name="merge_router",
    )(x, ys, ya, g_ssm, g_attn, w_out, g_ffn, w_router, b_router, cnt_in)


def _dispatch_kernel(last_tile_ref, nu_ref, slots_ref, h_hbm, xs_hbm, zero_ref, zsem, sem,
                     *, n_tokens, tok_per_step, n_experts, n_tiles, tile):
    step = pl.program_id(0)

    def row_copy(tok, slot):
        return pltpu.make_async_copy(h_hbm.at[pl.ds(tok, 1)], xs_hbm.at[pl.ds(slot, 1)], sem)

    def zero_tile(t):
        return pltpu.make_async_copy(zero_ref, xs_hbm.at[pl.ds(pl.multiple_of(t * tile, tile), tile)], zsem)

    @pl.when(step == 0)
    def _():
        zero_ref[...] = jnp.zeros_like(zero_ref)

        def start_tail(t, c):
            zero_tile(t).start()
            return c

        def wait_tail(t, c):
            zero_tile(t).wait()
            return c

        lax.fori_loop(nu_ref[0], n_tiles, start_tail, 0)
        lax.fori_loop(nu_ref[0], n_tiles, wait_tail, 0)
        for e in range(n_experts):
            lt = last_tile_ref[e]

            @pl.when(lt >= 0)
            def _():
                zero_tile(lt).start()
        for e in range(n_experts):
            lt = last_tile_ref[e]

            @pl.when(lt >= 0)
            def _():
                zero_tile(lt).wait()

    base = step * tok_per_step
    n_here = jnp.minimum(tok_per_step, n_tokens - base)
    n_groups = (n_here + DMA_GROUP - 1) // DMA_GROUP

    def issue(g):
        def body(r, c):
            lt = g * DMA_GROUP + r

            @pl.when(lt < n_here)
            def _():
                for kk in range(TOP_K):
                    row_copy(base + lt, slots_ref[lt * TOP_K + kk]).start()
            return c
        lax.fori_loop(0, DMA_GROUP, body, 0)

    def drain(g):
        def body(r, c):
            lt = g * DMA_GROUP + r

            @pl.when(lt < n_here)
            def _():
                for kk in range(TOP_K):
                    row_copy(base + lt, slots_ref[lt * TOP_K + kk]).wait()
            return c
        lax.fori_loop(0, DMA_GROUP, body, 0)

    def group(g, c):
        issue(g)

        @pl.when(g > 0)
        def _():
            drain(g - 1)
        return c

    lax.fori_loop(0, n_groups, group, 0)

    @pl.when(n_groups > 0)
    def _():
        drain(n_groups - 1)


def _dispatch(h_all, slots_flat, last_tile, n_used, *, n_slots, tok_per_step):
    n, d = h_all.shape
    n_steps = slots_flat.shape[0] // (tok_per_step * TOP_K)
    n_experts = last_tile.shape[0]
    grid_spec = pltpu.PrefetchScalarGridSpec(
        num_scalar_prefetch=2,
        grid=(n_steps,),
        in_specs=[
            pl.BlockSpec((tok_per_step * TOP_K,), lambda i, lt, nu: (i,), memory_space=pltpu.SMEM),
            pl.BlockSpec(memory_space=pl.ANY),
        ],
        out_specs=pl.BlockSpec(memory_space=pl.ANY),
        scratch_shapes=[pltpu.VMEM((EXPERT_TILE, d), F32), pltpu.SemaphoreType.DMA, pltpu.SemaphoreType.DMA],
    )
    return pl.pallas_call(
        functools.partial(_dispatch_kernel, n_tokens=n, tok_per_step=tok_per_step,
                          n_experts=n_experts, n_tiles=n_slots // EXPERT_TILE, tile=EXPERT_TILE),
        grid_spec=grid_spec,
        out_shape=jax.ShapeDtypeStruct((n_slots, d), F32),
        compiler_params=_cparams(("arbitrary",)),
        name="moe_dispatch",
    )(last_tile, n_used, slots_flat, h_all)


def _expert_kernel(te_ref, nu_ref, x_ref, w1_ref, b1_ref, w2_ref, b2_ref, y_ref, w1b_ref, w2b_ref, *, d_ff):
    i = pl.program_id(0)
    used = nu_ref[0]
    prev = te_ref[jnp.maximum(i - 1, 0)]
    changed = (i == 0) | (te_ref[i] != prev)

    @pl.when((i < used) & changed)
    def _():
        w1b_ref[...] = w1_ref[...].astype(BF16)
        w2b_ref[...] = w2_ref[...].astype(BF16)

    @pl.when(i >= used)
    def _():
        y_ref[...] = jnp.zeros_like(y_ref)

    @pl.when(i < used)
    def _():
        x = x_ref[...].astype(BF16)
        h = jnp.dot(x, w1b_ref[...], preferred_element_type=F32) + b1_ref[...]
        hg = jnp.minimum(h[:, :d_ff], SWIGLU_LIMIT)
        hl = jnp.clip(h[:, d_ff:], -SWIGLU_LIMIT, SWIGLU_LIMIT)
        act = (hl + 1.0) * (hg * jax.nn.sigmoid(SWIGLU_ALPHA * hg))
        y_ref[...] = jnp.dot(act.astype(BF16), w2b_ref[...], preferred_element_type=F32) + b2_ref[...]


def _experts(xs, tile_expert, n_used, w1, b1, w2, b2):
    s, d = xs.shape
    n_tiles = s // EXPERT_TILE
    n_e, _, two_ff = w1.shape
    d_ff = two_ff // 2

    def xmap(i, te, nu):
        return (i, 0)

    def wmap(i, te, nu):
        return (te[i], 0, 0)

    grid_spec = pltpu.PrefetchScalarGridSpec(
        num_scalar_prefetch=2,
        grid=(n_tiles,),
        in_specs=[
            pl.BlockSpec((EXPERT_TILE, d), xmap),
            pl.BlockSpec((None, d, two_ff), wmap),
            pl.BlockSpec((None, 1, two_ff), wmap),
            pl.BlockSpec((None, d_ff, d), wmap),
            pl.BlockSpec((None, 1, d), wmap),
        ],
        out_specs=pl.BlockSpec((EXPERT_TILE, d), xmap),
        scratch_shapes=[pltpu.VMEM((d, two_ff), BF16), pltpu.VMEM((d_ff, d), BF16)],
    )
    return pl.pallas_call(
        functools.partial(_expert_kernel, d_ff=d_ff),
        grid_spec=grid_spec,
        out_shape=jax.ShapeDtypeStruct((s, d), F32),
        compiler_params=_cparams(("arbitrary",)),
        name="moe_experts",
    )(tile_expert, n_used, xs, w1, b1.reshape(n_e, 1, two_ff), w2, b2.reshape(n_e, 1, d))


def _combine_kernel(slots_ref, x2_ref, tw_ref, g_ref, ys_hbm, o_ref, buf_ref, sem, *, tm):
    def row_copy(r, kk):
        return pltpu.make_async_copy(ys_hbm.at[pl.ds(slots_ref[r * TOP_K + kk], 1)],
                                     buf_ref.at[kk, pl.ds(r, 1)], sem)

    def issue(r, c):
        for kk in range(TOP_K):
            row_copy(r, kk).start()
        return c

    def drain(r, c):
        for kk in range(TOP_K):
            row_copy(r, kk).wait()
        return c

    lax.fori_loop(0, tm, issue, 0)
    lax.fori_loop(0, tm, drain, 0)
    tw = tw_ref[...]
    acc = x2_ref[...]
    for kk in range(TOP_K):
        acc = acc + tw[:, kk:kk + 1] * buf_ref[kk]
    inv = lax.rsqrt(jnp.mean(acc * acc, axis=-1, keepdims=True) + RMS_EPS)
    o_ref[...] = acc * inv * g_ref[...]


def _combine(x2, tw, slots_flat, ys, g_final, *, tm, row_offset):
    r, d = x2.shape
    grid_spec = pltpu.PrefetchScalarGridSpec(
        num_scalar_prefetch=0,
        grid=(r // tm,),
        in_specs=[
            pl.BlockSpec((tm * TOP_K,), lambda i: (i + row_offset // tm,), memory_space=pltpu.SMEM),
            pl.BlockSpec((tm, d), lambda i: (i, 0)),
            pl.BlockSpec((tm, LANES), lambda i: (i, 0)),
            pl.BlockSpec((1, d), lambda i: (0, 0)),
            pl.BlockSpec(memory_space=pl.ANY),
        ],
        out_specs=pl.BlockSpec((tm, d), lambda i: (i, 0)),
        scratch_shapes=[pltpu.VMEM((TOP_K, tm, d), F32), pltpu.SemaphoreType.DMA],
    )
    return pl.pallas_call(
        functools.partial(_combine_kernel, tm=tm),
        grid_spec=grid_spec,
        out_shape=jax.ShapeDtypeStruct((r, d), F32),
        compiler_params=_cparams(("arbitrary",)),
        name="moe_combine",
    )(slots_flat, x2, tw, g_final, ys)


def _pad_lanes(a, value=0.0):
    return jnp.pad(a, [(0, 0)] * (a.ndim - 1) + [(0, LANES - a.shape[-1])], constant_values=value)


def _layer(xp, xs, cache_k, cache_v, cache_logf, h0s_re, h0s_im, p):
    bp, tp, d = xp.shape
    bs, ts, _ = xs.shape
    n_heads = p['b_forget'].shape[0]
    n_groups, ssm_n = p['ssm_a_re'].shape
    w_ssm = n_groups * SSM_P
    w_attn = n_heads * HEAD_DIM
    gn = n_groups * ssm_n
    past = cache_k.shape[1]

    w_main = p['w_in'][:, :w_ssm + 3 * w_attn].astype(BF16)
    w_f = _pad_lanes(p['w_in'][:, w_ssm + 3 * w_attn:]).astype(BF16)
    b_f = _pad_lanes(p['b_forget'][None, :])
    g_mix = p['norm_mix_g'][None, :]

    ab_re, ab_im, bb_re, bb_im = _s5_params(p['ssm_a_re'], p['ssm_a_im'], p['ssm_log_dt'], p['ssm_b_re'], p['ssm_b_im'])
    b_mat, c_mat = _s5_block_mats(bb_re, bb_im, p['ssm_c_re'], p['ssm_c_im'], n_groups)
    d_skip = p['ssm_d'].reshape(1, w_ssm)
    w_glu = p['w_glu'].astype(BF16)
    b_glu = p['b_glu'][None, :]

    proj = functools.partial(_in_proj, n_heads=n_heads, w_ssm=w_ssm, w_attn=w_attn)
    up, kp, vp, lfp, qhp, khp, vhp, fcp = proj(xp, g_mix, w_main, w_f, b_f, tm=ROW_TILE)
    us, ks, vs, lfs, qhs, khs, vhs, _ = proj(xs, g_mix, w_main, w_f, b_f, tm=ts)

    zeros_state = jnp.zeros((bp, 1, gn), F32)
    s5 = functools.partial(_s5_mixer, ab_re=ab_re, ab_im=ab_im, b_mat=b_mat, c_mat=c_mat,
                           d_skip=d_skip, w_glu=w_glu, b_glu=b_glu)
    ysp, hrp, hip = s5(up, zeros_state, zeros_state, tm=ROW_TILE)
    yss, hrs, his = s5(us, h0s_re.reshape(bs, 1, gn), h0s_im.reshape(bs, 1, gn), tm=ts)

    tpad = qhp.shape[2]
    fc_h = jnp.transpose(fcp[:, :, :n_heads], (0, 2, 1))
    yap = _fox_prompt(qhp, khp, vhp, fc_h[..., None], fc_h.reshape(bp, n_heads, tpad // ROW_TILE, ROW_TILE),
                      tq=ROW_TILE)

    lf_all = _pad_lanes(jnp.concatenate([cache_logf.astype(F32), lfs], axis=1))
    t_all = past + ts
    tm_cum = max(m for m in range(8, ROW_TILE + 1, 8) if t_all % m == 0)
    fcs = _cumsum_time(lf_all, tm=tm_cum)
    fcs_h = jnp.transpose(fcs[:, :, :n_heads], (0, 2, 1))
    yas = _fox_sample(qhs, khs, vhs, cache_k.reshape(bs, past, w_attn), cache_v.reshape(bs, past, w_attn),
                      fcs_h[:, :, past:, None], fcs_h[:, :, None, :past], fcs_h[:, :, None, past:])

    w_out = p['w_out'].astype(BF16)
    n_experts = p['w_router'].shape[1]
    w_router = _pad_lanes(p['w_router'])
    b_router = _pad_lanes(p['b_router'][None, :], value=-1e30)
    merge = functools.partial(_merge_router, g_ssm=p['g_out_ssm'][None, :], g_attn=p['g_out_attn'][None, :],
                              w_out=w_out, g_ffn=p['norm_ffn_g'][None, :], w_router=w_router, b_router=b_router)
    x2p, h2p, tip, twp, tpp, cnt_p = merge(xp, ysp, yap, cnt_in=jnp.zeros((1, LANES), F32), tm=ROW_TILE)
    x2s, h2s, tis, tws, tps, cnt = merge(xs, yss, yas, cnt_in=cnt_p, tm=ts)

    n_p = bp * tp
    n_tok = n_p + bs * ts
    counts = cnt[0, :n_experts].astype(jnp.int32)
    tiles_per = (counts + EXPERT_TILE - 1) // EXPERT_TILE
    tile_end = jnp.cumsum(tiles_per)
    tile_start = tile_end - tiles_per
    n_tiles = (n_tok * TOP_K) // EXPERT_TILE + n_experts
    n_slots = n_tiles * EXPERT_TILE
    n_used = tile_end[-1:].astype(jnp.int32)
    tile_ids = jnp.minimum(jnp.arange(n_tiles, dtype=jnp.int32), n_used - 1)
    tile_expert = jnp.searchsorted(tile_end, tile_ids, side='right').astype(jnp.int32)
    last_tile = jnp.where(tiles_per > 0, tile_end - 1, -1).astype(jnp.int32)
    ids = jnp.concatenate([tip.reshape(n_p, LANES)[:, :TOP_K], tis.reshape(bs * ts, LANES)[:, :TOP_K]], axis=0)
    pos = jnp.concatenate([tpp.reshape(n_p, LANES)[:, :TOP_K], tps.reshape(bs * ts, LANES)[:, :TOP_K]], axis=0)
    slots = tile_start[ids] * EXPERT_TILE + pos
    tok_per_step = 1024
    n_pad = _round_up(n_tok, tok_per_step)
    slots_flat = jnp.pad(slots, ((0, n_pad - n_tok), (0, 0))).reshape(-1).astype(jnp.int32)

    h_all = jnp.concatenate([h2p.reshape(n_p, d), h2s.reshape(bs * ts, d)], axis=0)
    xs_sorted = _dispatch(h_all, slots_flat, last_tile, n_used, n_slots=n_slots, tok_per_step=tok_per_step)
    ys_sorted = _experts(xs_sorted, tile_expert, n_used, p['w_mlp1'], p['b_mlp1'], p['w_mlp2'], p['b_mlp2'])

    return (x2p, twp, x2s, tws, slots_flat, ys_sorted,
            (kp, vp, lfp, hrp, hip, ks, vs, lfs, hrs, his))


def kernel(x_prompt, x_sample, cache_k, cache_v, cache_logf, state_ssm_re, state_ssm_im, meta_tokens, norm_mix_g, w_in, b_forget, ssm_a_re, ssm_a_im, ssm_log_dt, ssm_b_re, ssm_b_im, ssm_c_re, ssm_c_im, ssm_d, w_glu, b_glu, g_out_ssm, g_out_attn, w_out, norm_ffn_g, w_router, b_router, w_mlp1, b_mlp1, w_mlp2, b_mlp2, norm_final_g):
    depth = w_in.shape[0]
    assert depth == 1, "the routing tables below are built for a single trunk layer"
    bp, seq, d = x_prompt.shape
    bs, ts, _ = x_sample.shape
    n_heads = b_forget.shape[1]
    n_groups, ssm_n = ssm_a_re.shape[1:]
    meta = jnp.broadcast_to(meta_tokens.astype(x_prompt.dtype)[None], (bp, N_META, d))
    xp = jnp.concatenate([meta, x_prompt], axis=1)
    tp = xp.shape[1]
    l = 0
    params = dict(norm_mix_g=norm_mix_g[l], w_in=w_in[l], b_forget=b_forget[l], ssm_a_re=ssm_a_re[l],
                  ssm_a_im=ssm_a_im[l], ssm_log_dt=ssm_log_dt[l], ssm_b_re=ssm_b_re[l], ssm_b_im=ssm_b_im[l],
                  ssm_c_re=ssm_c_re[l], ssm_c_im=ssm_c_im[l], ssm_d=ssm_d[l], w_glu=w_glu[l], b_glu=b_glu[l],
                  g_out_ssm=g_out_ssm[l], g_out_attn=g_out_attn[l], w_out=w_out[l], norm_ffn_g=norm_ffn_g[l],
                  w_router=w_router[l], b_router=b_router[l], w_mlp1=w_mlp1[l], b_mlp1=b_mlp1[l],
                  w_mlp2=w_mlp2[l], b_mlp2=b_mlp2[l])
    (x2p, twp, x2s, tws, slots_flat, ys_sorted, outs) = _layer(
        xp, x_sample, cache_k[l], cache_v[l], cache_logf[l], state_ssm_re[l], state_ssm_im[l], params)
    kp, vp, lfp, hrp, hip, ks, vs, lfs, hrs, his = outs

    n_p = bp * tp
    tm_c = 256
    g_fin = norm_final_g[None, :]
    n_p_pad = _round_up(n_p, tm_c)
    x2p_flat = jnp.pad(x2p.reshape(n_p, d), ((0, n_p_pad - n_p), (0, 0)))
    twp_flat = jnp.pad(twp.reshape(n_p, LANES), ((0, n_p_pad - n_p), (0, 0)))
    slots_p = jnp.pad(slots_flat[:n_p * TOP_K], (0, (n_p_pad - n_p) * TOP_K))
    yp = _combine(x2p_flat, twp_flat, slots_p, ys_sorted, g_fin, tm=tm_c, row_offset=0)
    y_prompt = yp[:n_p].reshape(bp, tp, d)[:, N_META:]
    n_s = bs * ts
    slots_s = slots_flat[n_p * TOP_K:(n_p + n_s) * TOP_K]
    y_sample = _combine(x2s.reshape(n_s, d), tws.reshape(n_s, LANES), slots_s, ys_sorted, g_fin,
                        tm=tm_c, row_offset=0).reshape(bs, ts, d)

    hd = HEAD_DIM
    st = lambda a, b: a.reshape(1, b, n_groups, ssm_n)
    return (y_prompt, y_sample,
            kp.reshape(1, bp, tp, n_heads, hd), vp.reshape(1, bp, tp, n_heads, hd), lfp[None],
            st(hrp, bp), st(hip, bp),
            ks.reshape(1, bs, ts, n_heads, hd), vs.reshape(1, bs, ts, n_heads, hd), lfs[None],
            st(hrs, bs), st(his, bs))
```

```python
import functools
import math

import numpy as np
import jax
import jax.numpy as jnp
from jax import lax
from jax.experimental import pallas as pl
from jax.experimental.pallas import tpu as pltpu

F32 = jnp.float32
BF16 = jnp.bfloat16

LANES = 128
VMEM_LIMIT_BYTES = 56 * 1024 * 1024

N_META = 16
HEAD_DIM = 64
SSM_P = 16
SSM_N = 64
TOP_K = 4
SWIGLU_LIMIT = 7.0
SWIGLU_ALPHA = 1.702
RMS_EPS = 1e-6
LAMBDA_RE_MAX = -1e-4
LOG2E = math.log2(math.e)

ROW_TILE = 512
EXPERT_TILE = 256
GROUP_SLAB = 8
BIAS_TERMS = 3
ISSUE_UNROLL = 8


def _cparams(sem):
    return pltpu.CompilerParams(dimension_semantics=sem, vmem_limit_bytes=VMEM_LIMIT_BYTES)


def _rms(v, g):
    return v * lax.rsqrt(jnp.mean(v * v, axis=-1, keepdims=True) + RMS_EPS) * g


def _in_proj_kernel(x_ref, g_ref, w_ref, wf_ref, bf_ref, sel_ref,
                    u_ref, k_ref, v_ref, lf_ref, qh_ref, kh_ref, vh_ref, fc_ref, carry_ref,
                    *, t_valid, tm, n_heads, w_ssm, w_attn):
    i = pl.program_id(1)

    @pl.when(i == 0)
    def _():
        carry_ref[...] = jnp.zeros_like(carry_ref)

    x = x_ref[...]
    h = _rms(x, g_ref[...]).astype(BF16)
    rows = i * tm + lax.broadcasted_iota(jnp.int32, (tm, 1), 0)
    valid = rows < t_valid
    z = jnp.where(valid, jnp.dot(h, w_ref[...], preferred_element_type=F32), 0.0)
    u_ref[...] = z[:, :w_ssm]
    k_ref[...] = z[:, w_ssm + w_attn:w_ssm + 2 * w_attn]
    v_ref[...] = z[:, w_ssm + 2 * w_attn:w_ssm + 3 * w_attn]

    zf = jnp.dot(h, wf_ref[...], preferred_element_type=F32) + bf_ref[...]
    lf = jnp.where(valid, jnp.minimum(zf, 0.0) - jnp.log1p(jnp.exp(-jnp.abs(zf))), 0.0)
    lf_ref[...] = lf[:, :n_heads]
    r = lax.broadcasted_iota(jnp.int32, (tm, tm), 0)
    c = lax.broadcasted_iota(jnp.int32, (tm, tm), 1)
    tri = (c <= r).astype(F32)
    fc = jnp.dot(tri, lf, preferred_element_type=F32, precision=lax.Precision.HIGHEST) + carry_ref[0:1, :]
    fc_ref[...] = fc
    carry_ref[0:1, :] = fc[tm - 1:tm, :]

    bias = -LOG2E * fc
    pieces = []
    rest = bias
    for _ in range(BIAS_TERMS):
        piece = rest.astype(BF16)
        pieces.append(piece)
        rest = rest - piece.astype(F32)
    pieces = jnp.concatenate(pieces, axis=-1)

    lane = lax.broadcasted_iota(jnp.int32, (tm, LANES), 1)
    low = lane < HEAD_DIM
    q_tail = jnp.where((lane >= HEAD_DIM) & (lane < HEAD_DIM + BIAS_TERMS), 1.0, 0.0)
    v_tail = jnp.where(lane == HEAD_DIM, 1.0, 0.0)
    for pair in range(n_heads // 2):
        sl = slice(pair * LANES, (pair + 1) * LANES)
        qt = z[:, w_ssm:w_ssm + w_attn][:, sl] * (LOG2E * HEAD_DIM ** -0.5)
        kt = z[:, w_ssm + w_attn:w_ssm + 2 * w_attn][:, sl]
        vt = z[:, w_ssm + 2 * w_attn:w_ssm + 3 * w_attn][:, sl]
        for half in range(2):
            hh = 2 * pair + half
            if half:
                qt, kt, vt = (pltpu.roll(a, HEAD_DIM, 1) for a in (qt, kt, vt))
            k_tail = jnp.dot(pieces, sel_ref[hh], preferred_element_type=F32)
            qh_ref[hh] = jnp.where(low, qt, q_tail).astype(BF16)
            kh_ref[hh] = jnp.where(low, kt, k_tail).astype(BF16)
            vh_ref[hh] = jnp.where(low, vt, v_tail).astype(BF16)


def _bias_selectors(n_heads):
    sel = np.zeros((n_heads, BIAS_TERMS * LANES, LANES), np.float32)
    for hh in range(n_heads):
        for c in range(BIAS_TERMS):
            sel[hh, c * LANES + hh, HEAD_DIM + c] = 1.0
    return jnp.asarray(sel, BF16)


def _in_proj(x, g, w_main, w_f, b_f, *, n_heads, w_ssm, w_attn, tm):
    bsz, t, d = x.shape
    nt = pl.cdiv(t, tm)
    tp = nt * tm
    row = lambda b, i: (b, i, 0)
    const = lambda b, i: (0, 0)
    head = lambda b, i: (b, 0, i, 0)
    sel = _bias_selectors(n_heads)
    kern = functools.partial(_in_proj_kernel, t_valid=t, tm=tm, n_heads=n_heads, w_ssm=w_ssm, w_attn=w_attn)
    return pl.pallas_call(
        kern,
        grid=(bsz, nt),
        in_specs=[
            pl.BlockSpec((None, tm, d), row),
            pl.BlockSpec((1, d), const),
            pl.BlockSpec(w_main.shape, const),
            pl.BlockSpec(w_f.shape, const),
            pl.BlockSpec((1, LANES), const),
            pl.BlockSpec(sel.shape, lambda b, i: (0, 0, 0)),
        ],
        out_specs=[
            pl.BlockSpec((None, tm, w_ssm), row),
            pl.BlockSpec((None, tm, w_attn), row),
            pl.BlockSpec((None, tm, w_attn), row),
            pl.BlockSpec((None, tm, n_heads), row),
            pl.BlockSpec((None, n_heads, tm, LANES), head),
            pl.BlockSpec((None, n_heads, tm, LANES), head),
            pl.BlockSpec((None, n_heads, tm, LANES), head),
            pl.BlockSpec((None, tm, LANES), row),
        ],
        out_shape=[
            jax.ShapeDtypeStruct((bsz, t, w_ssm), F32),
            jax.ShapeDtypeStruct((bsz, t, w_attn), F32),
            jax.ShapeDtypeStruct((bsz, t, w_attn), F32),
            jax.ShapeDtypeStruct((bsz, t, n_heads), F32),
            jax.ShapeDtypeStruct((bsz, n_heads, tp, LANES), BF16),
            jax.ShapeDtypeStruct((bsz, n_heads, tp, LANES), BF16),
            jax.ShapeDtypeStruct((bsz, n_heads, tp, LANES), BF16),
            jax.ShapeDtypeStruct((bsz, tp, LANES), F32),
        ],
        scratch_shapes=[pltpu.VMEM((8, LANES), F32)],
        compiler_params=_cparams(("arbitrary", "arbitrary")),
        name="in_proj",
    )(x, g, w_main, w_f, b_f, sel)


def _s5_param_kernel(are_ref, aim_ref, ldt_ref, bre_ref, bim_ref, abr_ref, abi_ref, bbr_ref, bbi_ref):
    lam_re = jnp.minimum(are_ref[...], LAMBDA_RE_MAX)
    lam_im = aim_ref[...]
    dt = jnp.exp(ldt_ref[...])
    mag = jnp.exp(lam_re * dt)
    ab_re = mag * jnp.cos(lam_im * dt)
    ab_im = mag * jnp.sin(lam_im * dt)
    abr_ref[...] = ab_re
    abi_ref[...] = ab_im
    nr = ab_re - 1.0
    ni = ab_im
    den = lam_re * lam_re + lam_im * lam_im
    cr = (nr * lam_re + ni * lam_im) / den
    ci = (ni * lam_re - nr * lam_im) / den
    b_re = bre_ref[...]
    b_im = bim_ref[...]
    bbr_ref[...] = cr * b_re - ci * b_im
    bbi_ref[...] = cr * b_im + ci * b_re


def _s5_params(a_re, a_im, log_dt, b_re, b_im):
    g, n = a_re.shape
    p = b_re.shape[-1]
    gn = g * n
    flat = lambda a: a.reshape(1, gn)
    ldt = jnp.broadcast_to(log_dt[:, None], (g, n)).reshape(1, gn)
    bt = lambda b: jnp.transpose(b, (2, 0, 1)).reshape(p, gn)
    return pl.pallas_call(
        _s5_param_kernel,
        out_shape=[jax.ShapeDtypeStruct((1, gn), F32), jax.ShapeDtypeStruct((1, gn), F32),
                   jax.ShapeDtypeStruct((p, gn), F32), jax.ShapeDtypeStruct((p, gn), F32)],
        name="s5_params",
    )(flat(a_re), flat(a_im), ldt, bt(b_re), bt(b_im))


def _s5_block_mats(bb_re, bb_im, c_re, c_im, n_groups):
    p = bb_re.shape[0]
    n = bb_re.shape[1] // n_groups
    s = n_groups // GROUP_SLAB
    eye = jnp.eye(GROUP_SLAB, dtype=F32)

    def in_blk(bb):
        b4 = bb.reshape(p, s, GROUP_SLAB, n)
        return jnp.einsum('qsgn,gh->sgqhn', b4, eye)

    b_mat = jnp.concatenate([in_blk(bb_re).reshape(s, GROUP_SLAB * p, GROUP_SLAB * n),
                             in_blk(bb_im).reshape(s, GROUP_SLAB * p, GROUP_SLAB * n)], axis=-1)

    def out_blk(c):
        c4 = c.reshape(s, GROUP_SLAB, p, n)
        return jnp.einsum('sgpn,gh->sgnhp', c4, eye)

    c_mat = jnp.concatenate([out_blk(c_re).reshape(s, GROUP_SLAB * n, GROUP_SLAB * p),
                             out_blk(-c_im).reshape(s, GROUP_SLAB * n, GROUP_SLAB * p)], axis=1)
    return b_mat.astype(BF16), c_mat.astype(BF16)


def _s5_kernel(u_ref, h0r_ref, h0i_ref, abr_ref, abi_ref, bm_ref, cm_ref, d_ref, wg_ref, bg_ref,
               y_ref, hr_out_ref, hi_out_ref, sre_ref, sim_ref, hre_ref, him_ref,
               *, t_valid, tm, n_slabs, slab_in, slab_state):
    i = pl.program_id(1)
    nt = pl.num_programs(1)

    @pl.when(i == 0)
    def _():
        hre_ref[...] = h0r_ref[...]
        him_ref[...] = h0i_ref[...]

    u = u_ref[...]
    ub = u.astype(BF16)
    for s in range(n_slabs):
        z = jnp.dot(ub[:, s * slab_in:(s + 1) * slab_in], bm_ref[s], preferred_element_type=F32)
        sre_ref[:, s * slab_state:(s + 1) * slab_state] = z[:, :slab_state]
        sim_ref[:, s * slab_state:(s + 1) * slab_state] = z[:, slab_state:]

    a_re = abr_ref[...]
    a_im = abi_ref[...]
    last_row = (t_valid - 1) % tm

    def step(t, carry):
        h_re, h_im = carry
        b_re = sre_ref[pl.ds(t, 1), :]
        b_im = sim_ref[pl.ds(t, 1), :]
        n_re = a_re * h_re - a_im * h_im + b_re
        n_im = a_re * h_im + a_im * h_re + b_im
        sre_ref[pl.ds(t, 1), :] = n_re
        sim_ref[pl.ds(t, 1), :] = n_im
        return n_re, n_im

    h_re, h_im = lax.fori_loop(0, tm, step, (hre_ref[...], him_ref[...]))
    hre_ref[...] = h_re
    him_ref[...] = h_im

    @pl.when(i == nt - 1)
    def _():
        hr_out_ref[...] = sre_ref[last_row:last_row + 1, :]
        hi_out_ref[...] = sim_ref[last_row:last_row + 1, :]

    ys = []
    for s in range(n_slabs):
        sl = slice(s * slab_state, (s + 1) * slab_state)
        cm = cm_ref[s]
        y = jnp.dot(sre_ref[:, sl].astype(BF16), cm[:slab_state], preferred_element_type=F32)
        y = y + jnp.dot(sim_ref[:, sl].astype(BF16), cm[slab_state:], preferred_element_type=F32)
        ys.append(y)
    y = jnp.concatenate(ys, axis=-1) + d_ref[...] * u
    y = jax.nn.gelu(y)
    gate = jnp.dot(y.astype(BF16), wg_ref[...], preferred_element_type=F32) + bg_ref[...]
    y_ref[...] = y * jax.nn.sigmoid(gate)


def _s5_mixer(u, h0_re, h0_im, ab_re, ab_im, b_mat, c_mat, d_skip, w_glu, b_glu, *, tm):
    bsz, t, w = u.shape
    gn = ab_re.shape[-1]
    n_slabs = b_mat.shape[0]
    nt = pl.cdiv(t, tm)
    row = lambda b, i: (b, i, 0)
    st = lambda b, i: (b, 0, 0)
    c2 = lambda b, i: (0, 0)
    c3 = lambda b, i: (0, 0, 0)
    kern = functools.partial(_s5_kernel, t_valid=t, tm=tm, n_slabs=n_slabs,
                             slab_in=w // n_slabs, slab_state=gn // n_slabs)
    return pl.pallas_call(
        kern,
        grid=(bsz, nt),
        in_specs=[
            pl.BlockSpec((None, tm, w), row),
            pl.BlockSpec((None, 1, gn), st),
            pl.BlockSpec((None, 1, gn), st),
            pl.BlockSpec((1, gn), c2),
            pl.BlockSpec((1, gn), c2),
            pl.BlockSpec(b_mat.shape, c3),
            pl.BlockSpec(c_mat.shape, c3),
            pl.BlockSpec((1, w), c2),
            pl.BlockSpec((w, w), c2),
            pl.BlockSpec((1, w), c2),
        ],
        out_specs=[
            pl.BlockSpec((None, tm, w), row),
            pl.BlockSpec((None, 1, gn), st),
            pl.BlockSpec((None, 1, gn), st),
        ],
        out_shape=[
            jax.ShapeDtypeStruct((bsz, t, w), F32),
            jax.ShapeDtypeStruct((bsz, 1, gn), F32),
            jax.ShapeDtypeStruct((bsz, 1, gn), F32),
        ],
        scratch_shapes=[pltpu.VMEM((tm, gn), F32), pltpu.VMEM((tm, gn), F32),
                        pltpu.VMEM((1, gn), F32), pltpu.VMEM((1, gn), F32)],
        compiler_params=_cparams(("arbitrary", "arbitrary")),
        name="s5_mixer",
    )(u, h0_re, h0_im, ab_re, ab_im, b_mat, c_mat, d_skip, w_glu, b_glu)


def _fox_prompt_kernel(qa_ref, qb_ref, ka_ref, kb_ref, va_ref, vb_ref, o_ref, m_ref, acc_ref, *, tq):
    i = pl.program_id(2)
    heads = ((qa_ref, ka_ref, va_ref), (qb_ref, kb_ref, vb_ref))
    m_ref[...] = jnp.full_like(m_ref, -jnp.inf)
    acc_ref[...] = jnp.zeros_like(acc_ref)

    def block(j, masked):
        start = pl.multiple_of(j * tq, tq)
        for hh, (q_ref, k_ref, v_ref) in enumerate(heads):
            kj = k_ref[pl.ds(start, tq), :]
            vj = v_ref[pl.ds(start, tq), :]
            g = lax.dot_general(q_ref[...], kj, (((1,), (1,)), ((), ())), preferred_element_type=F32)
            if masked:
                r = lax.broadcasted_iota(jnp.int32, (tq, tq), 0)
                c = lax.broadcasted_iota(jnp.int32, (tq, tq), 1)
                g = jnp.where(c <= r, g, -jnp.inf)
            m_old = m_ref[hh]
            m_new = jnp.maximum(m_old, jnp.max(g, axis=-1, keepdims=True))
            p = jnp.exp2(g - m_new)
            alpha = jnp.exp2(m_old - m_new)
            acc_ref[hh] = alpha * acc_ref[hh] + jnp.dot(p.astype(BF16), vj, preferred_element_type=F32)
            m_ref[hh] = m_new

    block(i, True)

    def body(t, c):
        block(i - 1 - t, False)
        return c

    lax.fori_loop(0, i, body, 0)
    outs = []
    for hh in range(2):
        acc = acc_ref[hh]
        outs.append(acc[:, :HEAD_DIM] / acc[:, HEAD_DIM:HEAD_DIM + 1])
    o_ref[...] = jnp.concatenate(outs, axis=-1)


def _fox_prompt(qh, kh, vh, *, tq):
    bsz, nh, tp, wl = qh.shape
    nq = tp // tq
    qa = lambda b, p, i: (b, 2 * p, i, 0)
    qb = lambda b, p, i: (b, 2 * p + 1, i, 0)
    fa = lambda b, p, i: (b, 2 * p, 0, 0)
    fb = lambda b, p, i: (b, 2 * p + 1, 0, 0)
    q_spec = lambda im: pl.BlockSpec((None, None, tq, wl), im)
    kv_spec = lambda im: pl.BlockSpec((None, None, tp, wl), im)
    return pl.pallas_call(
        functools.partial(_fox_prompt_kernel, tq=tq),
        grid=(bsz, nh // 2, nq),
        in_specs=[q_spec(qa), q_spec(qb), kv_spec(fa), kv_spec(fb), kv_spec(fa), kv_spec(fb)],
        out_specs=pl.BlockSpec((None, tq, 2 * HEAD_DIM), lambda b, p, i: (b, i, p)),
        out_shape=jax.ShapeDtypeStruct((bsz, tp, nh * HEAD_DIM), F32),
        scratch_shapes=[pltpu.VMEM((2, tq, 1), F32), pltpu.VMEM((2, tq, wl), F32)],
        compiler_params=_cparams(("arbitrary", "arbitrary", "arbitrary")),
        name="fox_prompt",
    )(qh, qh, kh, kh, vh, vh)


def _fox_sample_kernel(q_ref, kn_ref, vn_ref, ck_ref, cv_ref, bc_ref, bn_ref, o_ref, *, n_heads):
    tq = q_ref.shape[1]
    r = lax.broadcasted_iota(jnp.int32, (tq, tq), 0)
    c = lax.broadcasted_iota(jnp.int32, (tq, tq), 1)
    outs = []
    dn = (((1,), (1,)), ((), ()))
    for hh in range(n_heads):
        sl = slice(hh * HEAD_DIM, (hh + 1) * HEAD_DIM)
        q = q_ref[hh][:, :HEAD_DIM]
        kn = kn_ref[hh][:, :HEAD_DIM]
        vn = vn_ref[hh][:, :HEAD_DIM]
        kc = ck_ref[:, sl].astype(BF16)
        vc = cv_ref[:, sl].astype(BF16)
        g_c = lax.dot_general(q, kc, dn, preferred_element_type=F32) + bc_ref[hh]
        g_n = lax.dot_general(q, kn, dn, preferred_element_type=F32) + bn_ref[hh]
        g_n = jnp.where(c <= r, g_n, -jnp.inf)
        m = jnp.maximum(jnp.max(g_c, axis=-1, keepdims=True), jnp.max(g_n, axis=-1, keepdims=True))
        p_c = jnp.exp2(g_c - m)
        p_n = jnp.exp2(g_n - m)
        den = jnp.sum(p_c, axis=-1, keepdims=True) + jnp.sum(p_n, axis=-1, keepdims=True)
        num = jnp.dot(p_c.astype(BF16), vc, preferred_element_type=F32)
        num = num + jnp.dot(p_n.astype(BF16), vn, preferred_element_type=F32)
        outs.append(num / den)
    o_ref[...] = jnp.concatenate(outs, axis=-1)


def _fox_sample(qh, kh, vh, cache_k, cache_v, bias_cache, bias_new):
    bsz, nh, tq, wl = qh.shape
    past = cache_k.shape[1]
    w = nh * HEAD_DIM
    b4 = lambda b: (b, 0, 0, 0)
    b3 = lambda b: (b, 0, 0)
    return pl.pallas_call(
        functools.partial(_fox_sample_kernel, n_heads=nh),
        grid=(bsz,),
        in_specs=[pl.BlockSpec((None, nh, tq, wl), b4), pl.BlockSpec((None, nh, tq, wl), b4),
                  pl.BlockSpec((None, nh, tq, wl), b4),
                  pl.BlockSpec((None, past, w), b3), pl.BlockSpec((None, past, w), b3),
                  pl.BlockSpec((None, nh, 1, past), b4), pl.BlockSpec((None, nh, 1, tq), b4)],
        out_specs=pl.BlockSpec((None, tq, w), b3),
        out_shape=jax.ShapeDtypeStruct((bsz, tq, w), F32),
        compiler_params=_cparams(("arbitrary",)),
        name="fox_sample",
    )(qh, kh, vh, cache_k, cache_v, bias_cache, bias_new)


def _cumsum_kernel(x_ref, o_ref, carry_ref, *, tm):
    i = pl.program_id(1)

    @pl.when(i == 0)
    def _():
        carry_ref[...] = jnp.zeros_like(carry_ref)

    r = lax.broadcasted_iota(jnp.int32, (tm, tm), 0)
    c = lax.broadcasted_iota(jnp.int32, (tm, tm), 1)
    tri = (c <= r).astype(F32)
    fc = jnp.dot(tri, x_ref[...], preferred_element_type=F32, precision=lax.Precision.HIGHEST) + carry_ref[0:1, :]
    o_ref[...] = fc
    carry_ref[0:1, :] = fc[tm - 1:tm, :]


def _cumsum_time(x, *, tm):
    bsz, t, w = x.shape
    row = lambda b, i: (b, i, 0)
    return pl.pallas_call(
        functools.partial(_cumsum_kernel, tm=tm),
        grid=(bsz, t // tm),
        in_specs=[pl.BlockSpec((None, tm, w), row)],
        out_specs=pl.BlockSpec((None, tm, w), row),
        out_shape=jax.ShapeDtypeStruct((bsz, t, w), F32),
        scratch_shapes=[pltpu.VMEM((8, w), F32)],
        compiler_params=_cparams(("arbitrary", "arbitrary")),
        name="cumsum_time",
    )(x)


def _merge_router_kernel(x_ref, ys_ref, ya_ref, gs_ref, ga_ref, wo_ref, gf_ref, wr_ref, br_ref, cnt_in_ref,
                         x2_ref, ti_ref, tw_ref, tp_ref, cnt_out_ref, cnt_ref, *, t_valid, tm):
    first = (pl.program_id(0) == 0) & (pl.program_id(1) == 0)

    @pl.when(first)
    def _():
        cnt_ref[...] = jnp.broadcast_to(cnt_in_ref[...], cnt_ref.shape)

    mix = jnp.concatenate([_rms(ys_ref[...], gs_ref[...]), _rms(ya_ref[...], ga_ref[...])], axis=-1)
    x2 = x_ref[...] + jnp.dot(mix.astype(BF16), wo_ref[...], preferred_element_type=F32)
    x2_ref[...] = x2
    h2 = _rms(x2, gf_ref[...])
    logits = jnp.dot(h2, wr_ref[...], preferred_element_type=F32, precision=lax.Precision.HIGHEST) + br_ref[...]

    lane = lax.broadcasted_iota(jnp.int32, (tm, LANES), 1)
    rows = pl.program_id(1) * tm + lax.broadcasted_iota(jnp.int32, (tm, 1), 0)
    valid = rows < t_valid
    vals = logits
    top_v, top_i, sels = [], [], []
    for _ in range(TOP_K):
        mx = jnp.max(vals, axis=-1, keepdims=True)
        idx = jnp.min(jnp.where(vals == mx, lane, LANES), axis=-1, keepdims=True)
        sel = lane == idx
        vals = jnp.where(sel, -jnp.inf, vals)
        top_v.append(mx)
        top_i.append(idx)
        sels.append(sel)
    ex = [jnp.exp(v - top_v[0]) for v in top_v]
    den = ex[0] + ex[1] + ex[2] + ex[3]

    member = jnp.zeros((tm, LANES), F32)
    for sel in sels:
        member = member + jnp.where(valid, jnp.where(sel, 1.0, 0.0), 0.0)
    r = lax.broadcasted_iota(jnp.int32, (tm, tm), 0)
    c = lax.broadcasted_iota(jnp.int32, (tm, tm), 1)
    tri = (c < r).astype(BF16)
    before = jnp.dot(tri, member.astype(BF16), preferred_element_type=F32) + cnt_ref[0:1, :]
    cnt_new = cnt_ref[0:1, :] + jnp.sum(member, axis=0, keepdims=True)
    cnt_ref[0:1, :] = cnt_new
    cnt_out_ref[...] = cnt_new

    ti = jnp.zeros((tm, LANES), jnp.int32)
    tw = jnp.zeros((tm, LANES), F32)
    tpos = jnp.zeros((tm, LANES), F32)
    for kk in range(TOP_K):
        pos = jnp.sum(jnp.where(sels[kk], before, 0.0), axis=-1, keepdims=True)
        ti = jnp.where(lane == kk, top_i[kk], ti)
        tw = jnp.where(lane == kk, ex[kk] / den, tw)
        tpos = jnp.where(lane == kk, pos, tpos)
    ti_ref[...] = ti
    tw_ref[...] = tw
    tp_ref[...] = tpos.astype(jnp.int32)


def _merge_router(x, ys, ya, g_ssm, g_attn, w_out, g_ffn, w_router, b_router, cnt_in, *, tm):
    bsz, t, d = x.shape
    w = ys.shape[-1]
    nt = pl.cdiv(t, tm)
    row = lambda b, i: (b, i, 0)
    c2 = lambda b, i: (0, 0)
    return pl.pallas_call(
        functools.partial(_merge_router_kernel, t_valid=t, tm=tm),
        grid=(bsz, nt),
        in_specs=[
            pl.BlockSpec((None, tm, d), row),
            pl.BlockSpec((None, tm, w), row),
            pl.BlockSpec((None, tm, w), row),
            pl.BlockSpec((1, w), c2), pl.BlockSpec((1, w), c2),
            pl.BlockSpec(w_out.shape, c2),
            pl.BlockSpec((1, d), c2),
            pl.BlockSpec(w_router.shape, c2),
            pl.BlockSpec((1, LANES), c2),
            pl.BlockSpec((1, LANES), c2),
        ],
        out_specs=[
            pl.BlockSpec((None, tm, d), row),
            pl.BlockSpec((None, tm, LANES), row),
            pl.BlockSpec((None, tm, LANES), row),
            pl.BlockSpec((None, tm, LANES), row),
            pl.BlockSpec((1, LANES), c2),
        ],
        out_shape=[
            jax.ShapeDtypeStruct((bsz, t, d), F32),
            jax.ShapeDtypeStruct((bsz, t, LANES), jnp.int32),
            jax.ShapeDtypeStruct((bsz, t, LANES), F32),
            jax.ShapeDtypeStruct((bsz, t, LANES), jnp.int32),
            jax.ShapeDtypeStruct((1, LANES), F32),
        ],
        scratch_shapes=[pltpu.VMEM((8, LANES), F32)],
        compiler_params=_cparams(("arbitrary", "arbitrary")),
        name="merge_router",
    )(x, ys, ya, g_ssm, g_attn, w_out, g_ffn, w_router, b_router, cnt_in)


def _dispatch_kernel(last_tile_ref, nu_ref, slots_ref, xa_ref, xb_ref, g_ref, xs_hbm, hbuf_ref, zero_ref, zsem, sem,
                     *, t_valid, nt_a, n_steps_a, tm, n_experts, n_tiles, tile):
    step = pl.program_id(0)

    def zero_tile(t):
        return pltpu.make_async_copy(zero_ref, xs_hbm.at[pl.ds(pl.multiple_of(t * tile, tile), tile)], zsem)

    @pl.when(step == 0)
    def _():
        zero_ref[...] = jnp.zeros_like(zero_ref)

        def start_tail(t, c):
            zero_tile(t).start()
            return c

        def wait_tail(t, c):
            zero_tile(t).wait()
            return c

        lax.fori_loop(nu_ref[0], n_tiles, start_tail, 0)
        lax.fori_loop(nu_ref[0], n_tiles, wait_tail, 0)
        for e in range(n_experts):
            lt = last_tile_ref[e]

            @pl.when(lt >= 0)
            def _():
                zero_tile(lt).start()
        for e in range(n_experts):
            lt = last_tile_ref[e]

            @pl.when(lt >= 0)
            def _():
                zero_tile(lt).wait()

    in_a = step < n_steps_a

    @pl.when(in_a)
    def _():
        hbuf_ref[...] = _rms(xa_ref[...], g_ref[...])

    @pl.when(jnp.logical_not(in_a))
    def _():
        hbuf_ref[...] = _rms(xb_ref[...], g_ref[...])

    def row_copy(r, kk):
        return pltpu.make_async_copy(hbuf_ref.at[pl.ds(r, 1)], xs_hbm.at[pl.ds(slots_ref[r * TOP_K + kk], 1)], sem)

    def issue(r, c):
        for kk in range(TOP_K):
            row_copy(r, kk).start()
        return c

    def drain(r, c):
        for kk in range(TOP_K):
            row_copy(r, kk).wait()
        return c

    n_valid = jnp.where(in_a, jnp.minimum(tm, t_valid - (step % nt_a) * tm), tm)

    @pl.when(n_valid == tm)
    def _():
        lax.fori_loop(0, tm, issue, 0, unroll=ISSUE_UNROLL)
        for _ in range(TOP_K):
            pltpu.make_async_copy(hbuf_ref, xs_hbm.at[pl.ds(0, tm)], sem).wait()

    @pl.when(n_valid < tm)
    def _():
        lax.fori_loop(0, n_valid, issue, 0)
        lax.fori_loop(0, n_valid, drain, 0)


def _dispatch(xa, xb, g_ffn, slots_flat, last_tile, n_used, *, n_slots, tm):
    bsz, t, d = xa.shape
    nt_a = pl.cdiv(t, tm)
    n_steps_a = bsz * nt_a
    n_steps_b = xb.shape[0] // tm
    n_experts = last_tile.shape[0]
    grid_spec = pltpu.PrefetchScalarGridSpec(
        num_scalar_prefetch=2,
        grid=(n_steps_a + n_steps_b,),
        in_specs=[
            pl.BlockSpec((tm * TOP_K,), lambda s, lt, nu: (s,), memory_space=pltpu.SMEM),
            pl.BlockSpec((None, tm, d), lambda s, lt, nu: (jnp.minimum(s, n_steps_a - 1) // nt_a,
                                                          jnp.minimum(s, n_steps_a - 1) % nt_a, 0)),
            pl.BlockSpec((tm, d), lambda s, lt, nu: (jnp.maximum(s - n_steps_a, 0), 0)),
            pl.BlockSpec((1, d), lambda s, lt, nu: (0, 0)),
        ],
        out_specs=pl.BlockSpec(memory_space=pl.ANY),
        scratch_shapes=[pltpu.VMEM((tm, d), F32), pltpu.VMEM((EXPERT_TILE, d), F32),
                        pltpu.SemaphoreType.DMA, pltpu.SemaphoreType.DMA],
    )
    return pl.pallas_call(
        functools.partial(_dispatch_kernel, t_valid=t, nt_a=nt_a, n_steps_a=n_steps_a, tm=tm, n_experts=n_experts,
                          n_tiles=n_slots // EXPERT_TILE, tile=EXPERT_TILE),
        grid_spec=grid_spec,
        out_shape=jax.ShapeDtypeStruct((n_slots, d), F32),
        compiler_params=_cparams(("arbitrary",)),
        name="moe_dispatch",
    )(last_tile, n_used, slots_flat, xa, xb, g_ffn)


def _expert_kernel(te_ref, nu_ref, x_ref, w1_ref, b1_ref, w2_ref, b2_ref, y_ref, w1b_ref, w2b_ref, *, d_ff):
    i = pl.program_id(0)
    used = nu_ref[0]
    prev = te_ref[jnp.maximum(i - 1, 0)]
    changed = (i == 0) | (te_ref[i] != prev)

    @pl.when((i < used) & changed)
    def _():
        w1b_ref[...] = w1_ref[...].astype(BF16)
        w2b_ref[...] = w2_ref[...].astype(BF16)

    @pl.when(i >= used)
    def _():
        y_ref[...] = jnp.zeros_like(y_ref)

    @pl.when(i < used)
    def _():
        x = x_ref[...].astype(BF16)
        h = jnp.dot(x, w1b_ref[...], preferred_element_type=F32) + b1_ref[...]
        hg = jnp.minimum(h[:, :d_ff], SWIGLU_LIMIT)
        hl = jnp.clip(h[:, d_ff:], -SWIGLU_LIMIT, SWIGLU_LIMIT)
        act = (hl + 1.0) * (hg * jax.nn.sigmoid(SWIGLU_ALPHA * hg))
        y_ref[...] = jnp.dot(act.astype(BF16), w2b_ref[...], preferred_element_type=F32) + b2_ref[...]


def _experts(xs, tile_expert, n_used, w1, b1, w2, b2):
    s, d = xs.shape
    n_tiles = s // EXPERT_TILE
    n_e, _, two_ff = w1.shape
    d_ff = two_ff // 2

    def xmap(i, te, nu):
        return (i, 0)

    def wmap(i, te, nu):
        return (te[i], 0, 0)

    grid_spec = pltpu.PrefetchScalarGridSpec(
        num_scalar_prefetch=2,
        grid=(n_tiles,),
        in_specs=[
            pl.BlockSpec((EXPERT_TILE, d), xmap),
            pl.BlockSpec((None, d, two_ff), wmap),
            pl.BlockSpec((None, 1, two_ff), wmap),
            pl.BlockSpec((None, d_ff, d), wmap),
            pl.BlockSpec((None, 1, d), wmap),
        ],
        out_specs=pl.BlockSpec((EXPERT_TILE, d), xmap),
        scratch_shapes=[pltpu.VMEM((d, two_ff), BF16), pltpu.VMEM((d_ff, d), BF16)],
    )
    return pl.pallas_call(
        functools.partial(_expert_kernel, d_ff=d_ff),
        grid_spec=grid_spec,
        out_shape=jax.ShapeDtypeStruct((s, d), F32),
        compiler_params=_cparams(("arbitrary",)),
        name="moe_experts",
    )(tile_expert, n_used, xs, w1, b1.reshape(n_e, 1, two_ff), w2, b2.reshape(n_e, 1, d))


def _combine_kernel(slots_ref, g_ref, x2_hbm, tw_hbm, ys_hbm, o_ref, xbuf_ref, wbuf_ref, buf_ref, sem, lsem,
                    *, tm, row_offset):
    b = pl.program_id(0)
    start = pl.multiple_of(row_offset + pl.program_id(1) * tm, 8)
    x_copy = pltpu.make_async_copy(x2_hbm.at[b, pl.ds(start, tm)], xbuf_ref, lsem)
    w_copy = pltpu.make_async_copy(tw_hbm.at[b, pl.ds(start, tm)], wbuf_ref, lsem)
    x_copy.start()
    w_copy.start()

    def issue(r, c):
        for kk in range(TOP_K):
            pltpu.make_async_copy(ys_hbm.at[pl.ds(slots_ref[r * TOP_K + kk], 1)],
                                  buf_ref.at[kk, pl.ds(r, 1)], sem).start()
        return c

    lax.fori_loop(0, tm, issue, 0, unroll=ISSUE_UNROLL)
    x_copy.wait()
    w_copy.wait()
    for kk in range(TOP_K):
        pltpu.make_async_copy(ys_hbm.at[pl.ds(0, tm)], buf_ref.at[kk], sem).wait()
    tw = wbuf_ref[...]
    acc = xbuf_ref[...]
    for kk in range(TOP_K):
        acc = acc + tw[:, kk:kk + 1] * buf_ref[kk]
    o_ref[...] = _rms(acc, g_ref[...])


def _combine(x2, tw, slots_flat, ys, g_final, *, tm, row_offset):
    bsz, t, d = x2.shape
    t_out = t - row_offset
    nt = t_out // tm
    grid_spec = pltpu.PrefetchScalarGridSpec(
        num_scalar_prefetch=0,
        grid=(bsz, nt),
        in_specs=[
            pl.BlockSpec((tm * TOP_K,), lambda b, i: (b * nt + i,), memory_space=pltpu.SMEM),
            pl.BlockSpec((1, d), lambda b, i: (0, 0)),
            pl.BlockSpec(memory_space=pl.ANY),
            pl.BlockSpec(memory_space=pl.ANY),
            pl.BlockSpec(memory_space=pl.ANY),
        ],
        out_specs=pl.BlockSpec((None, tm, d), lambda b, i: (b, i, 0)),
        scratch_shapes=[pltpu.VMEM((tm, d), F32), pltpu.VMEM((tm, LANES), F32),
                        pltpu.VMEM((TOP_K, tm, d), F32), pltpu.SemaphoreType.DMA, pltpu.SemaphoreType.DMA],
    )
    return pl.pallas_call(
        functools.partial(_combine_kernel, tm=tm, row_offset=row_offset),
        grid_spec=grid_spec,
        out_shape=jax.ShapeDtypeStruct((bsz, t_out, d), F32),
        compiler_params=_cparams(("arbitrary", "arbitrary")),
        name="moe_combine",
    )(slots_flat, g_final, x2, tw, ys)


def _pad_lanes(a, value=0.0):
    return jnp.pad(a, [(0, 0)] * (a.ndim - 1) + [(0, LANES - a.shape[-1])], constant_values=value)


def kernel(x_prompt, x_sample, cache_k, cache_v, cache_logf, state_ssm_re, state_ssm_im, meta_tokens, norm_mix_g, w_in, b_forget, ssm_a_re, ssm_a_im, ssm_log_dt, ssm_b_re, ssm_b_im, ssm_c_re, ssm_c_im, ssm_d, w_glu, b_glu, g_out_ssm, g_out_attn, w_out, norm_ffn_g, w_router, b_router, w_mlp1, b_mlp1, w_mlp2, b_mlp2, norm_final_g):
    depth = w_in.shape[0]
    assert depth == 1, "the routing tables below are built for a single trunk layer"
    l = 0
    bp, seq, d = x_prompt.shape
    bs, ts, _ = x_sample.shape
    n_heads = b_forget.shape[1]
    n_groups, ssm_n = ssm_a_re.shape[1:]
    n_experts = w_router.shape[2]
    w_ssm = n_groups * SSM_P
    w_attn = n_heads * HEAD_DIM
    gn = n_groups * ssm_n
    past = cache_k.shape[2]
    assert seq % ROW_TILE == 0 and (bs * ts) % ROW_TILE == 0

    meta = jnp.broadcast_to(meta_tokens.astype(x_prompt.dtype)[None], (bp, N_META, d))
    xp = jnp.concatenate([meta, x_prompt], axis=1)
    xs = x_sample
    tp = xp.shape[1]

    w_main = w_in[l][:, :w_ssm + 3 * w_attn].astype(BF16)
    w_f = _pad_lanes(w_in[l][:, w_ssm + 3 * w_attn:]).astype(BF16)
    b_f = _pad_lanes(b_forget[l][None, :])
    g_mix = norm_mix_g[l][None, :]
    ab_re, ab_im, bb_re, bb_im = _s5_params(ssm_a_re[l], ssm_a_im[l], ssm_log_dt[l], ssm_b_re[l], ssm_b_im[l])
    b_mat, c_mat = _s5_block_mats(bb_re, bb_im, ssm_c_re[l], ssm_c_im[l], n_groups)

    proj = functools.partial(_in_proj, n_heads=n_heads, w_ssm=w_ssm, w_attn=w_attn)
    up, kp, vp, lfp, qhp, khp, vhp, _ = proj(xp, g_mix, w_main, w_f, b_f, tm=ROW_TILE)
    us, ks, vs, lfs, qhs, khs, vhs, _ = proj(xs, g_mix, w_main, w_f, b_f, tm=ts)

    s5 = functools.partial(_s5_mixer, ab_re=ab_re, ab_im=ab_im, b_mat=b_mat, c_mat=c_mat,
                           d_skip=ssm_d[l].reshape(1, w_ssm), w_glu=w_glu[l].astype(BF16), b_glu=b_glu[l][None, :])
    zeros_state = jnp.zeros((bp, 1, gn), F32)
    ysp, hrp, hip = s5(up, zeros_state, zeros_state, tm=ROW_TILE)
    yss, hrs, his = s5(us, state_ssm_re[l].reshape(bs, 1, gn), state_ssm_im[l].reshape(bs, 1, gn), tm=ts)

    yap = _fox_prompt(qhp, khp, vhp, tq=ROW_TILE)
    t_all = past + ts
    tm_cum = max(m for m in range(8, ROW_TILE + 1, 8) if t_all % m == 0)
    fcs = _cumsum_time(_pad_lanes(jnp.concatenate([cache_logf[l].astype(F32), lfs], axis=1)), tm=tm_cum)
    bias_s = -LOG2E * jnp.transpose(fcs[:, :, :n_heads], (0, 2, 1))[:, :, None, :]
    yas = _fox_sample(qhs, khs, vhs, cache_k[l].reshape(bs, past, w_attn), cache_v[l].reshape(bs, past, w_attn),
                      bias_s[..., :past], bias_s[..., past:])

    merge = functools.partial(_merge_router, g_ssm=g_out_ssm[l][None, :], g_attn=g_out_attn[l][None, :],
                              w_out=w_out[l].astype(BF16), g_ffn=norm_ffn_g[l][None, :],
                              w_router=_pad_lanes(w_router[l]), b_router=_pad_lanes(b_router[l][None, :], value=-1e30))
    x2p, tip, twp, tpp, cnt_p = merge(xp, ysp, yap, cnt_in=jnp.zeros((1, LANES), F32), tm=ROW_TILE)
    x2s, tis, tws, tps, cnt = merge(xs, yss, yas, cnt_in=cnt_p, tm=ts)

    n_tok = bp * tp + bs * ts
    counts = cnt[0, :n_experts].astype(jnp.int32)
    tiles_per = (counts + EXPERT_TILE - 1) // EXPERT_TILE
    tile_end = jnp.cumsum(tiles_per)
    tile_start = tile_end - tiles_per
    n_tiles = (n_tok * TOP_K) // EXPERT_TILE + n_experts
    n_slots = n_tiles * EXPERT_TILE
    n_used = tile_end[-1:].astype(jnp.int32)
    tile_ids = jnp.minimum(jnp.arange(n_tiles, dtype=jnp.int32), n_used - 1)
    tile_expert = jnp.sum((tile_end[None, :] <= tile_ids[:, None]).astype(jnp.int32), axis=1)
    last_tile = jnp.where(tiles_per > 0, tile_end - 1, -1).astype(jnp.int32)
    slot_base = (tile_start * EXPERT_TILE).astype(F32)

    def slots_of(ids, pos):
        onehot = jax.nn.one_hot(ids[..., :TOP_K], n_experts, dtype=F32)
        base = jnp.einsum('btke,e->btk', onehot, slot_base, precision=lax.Precision.HIGHEST)
        return base.astype(jnp.int32) + pos[..., :TOP_K]

    slots_p = slots_of(tip, tpp)
    slots_s = slots_of(tis, tps)
    tpad = pl.cdiv(tp, ROW_TILE) * ROW_TILE
    slots_disp = jnp.concatenate([jnp.pad(slots_p, ((0, 0), (0, tpad - tp), (0, 0))).reshape(-1),
                                  slots_s.reshape(-1)])
    xs_sorted = _dispatch(x2p, x2s.reshape(bs * ts, d), norm_ffn_g[l][None, :], slots_disp, last_tile, n_used,
                          n_slots=n_slots, tm=ROW_TILE)
    ys_sorted = _experts(xs_sorted, tile_expert, n_used, w_mlp1[l], b_mlp1[l], w_mlp2[l], b_mlp2[l])

    g_fin = norm_final_g[None, :]
    y_prompt = _combine(x2p, twp, slots_p[:, N_META:].reshape(-1), ys_sorted, g_fin, tm=EXPERT_TILE, row_offset=N_META)
    y_sample = _combine(x2s.reshape(1, bs * ts, d), tws.reshape(1, bs * ts, LANES), slots_s.reshape(-1), ys_sorted,
                        g_fin, tm=EXPERT_TILE, row_offset=0).reshape(bs, ts, d)

    hd = HEAD_DIM
    st = lambda a, b: a.reshape(1, b, n_groups, ssm_n)
    return (y_prompt, y_sample,
            kp.reshape(1, bp, tp, n_heads, hd), vp.reshape(1, bp, tp, n_heads, hd), lfp[None],
            st(hrp, bp), st(hip, bp),
            ks.reshape(1, bs, ts, n_heads, hd), vs.reshape(1, bs, ts, n_heads, hd), lfs[None],
            st(hrs, bs), st(his, bs))
```

```python
import functools
import math

import numpy as np
import jax
import jax.numpy as jnp
from jax import lax
from jax.experimental import pallas as pl
from jax.experimental.pallas import tpu as pltpu

F32 = jnp.float32
BF16 = jnp.bfloat16

LANES = 128
VMEM_LIMIT_BYTES = 56 * 1024 * 1024

N_META = 16
HEAD_DIM = 64
SSM_P = 16
SSM_N = 64
TOP_K = 4
SWIGLU_LIMIT = 7.0
SWIGLU_ALPHA = 1.702
RMS_EPS = 1e-6
LAMBDA_RE_MAX = -1e-4
LOG2E = math.log2(math.e)

ROW_TILE = 512
ATTN_Q_TILE = 512
EXPERT_TILE = 256
GROUP_SLAB = 8
BIAS_TERMS = 3
ISSUE_UNROLL = 8
UNDERFLOW_LOG2 = 160.0
NORM_SLACK = 1.001


def _cparams(sem):
    return pltpu.CompilerParams(dimension_semantics=sem, vmem_limit_bytes=VMEM_LIMIT_BYTES)


def _rms(v, g):
    return v * lax.rsqrt(jnp.mean(v * v, axis=-1, keepdims=True) + RMS_EPS) * g


def _split_bf16(a):
    pieces = []
    rest = a
    for _ in range(BIAS_TERMS):
        piece = rest.astype(BF16)
        pieces.append(piece)
        rest = rest - piece.astype(F32)
    return jnp.concatenate(pieces, axis=-1)


def _prefix_sum_rows(x):
    tm = x.shape[0]
    r = lax.broadcasted_iota(jnp.int32, (tm, tm), 0)
    c = lax.broadcasted_iota(jnp.int32, (tm, tm), 1)
    tri = jnp.where(c <= r, 1.0, 0.0).astype(BF16)
    s = jnp.dot(tri, _split_bf16(x), preferred_element_type=F32)
    return sum(s[:, t * LANES:(t + 1) * LANES] for t in range(BIAS_TERMS))


def _in_proj_kernel(x_ref, g_ref, w_ref, wf_ref, bf_ref, sel_ref,
                    u_ref, k_ref, v_ref, lf_ref, qh_ref, kh_ref, vh_ref, kn_ref, bi_ref, carry_ref,
                    *, t_valid, tm, n_heads, w_ssm, w_attn):
    i = pl.program_id(1)

    @pl.when(i == 0)
    def _():
        carry_ref[...] = jnp.zeros_like(carry_ref)

    x = x_ref[...]
    h = _rms(x, g_ref[...]).astype(BF16)
    rows = i * tm + lax.broadcasted_iota(jnp.int32, (tm, 1), 0)
    valid = rows < t_valid
    z = jnp.where(valid, jnp.dot(h, w_ref[...], preferred_element_type=F32), 0.0)
    u_ref[...] = z[:, :w_ssm]
    k_ref[...] = z[:, w_ssm + w_attn:w_ssm + 2 * w_attn]
    v_ref[...] = z[:, w_ssm + 2 * w_attn:w_ssm + 3 * w_attn]

    zf = jnp.dot(h, wf_ref[...], preferred_element_type=F32) + bf_ref[...]
    lf = jnp.where(valid, jnp.minimum(zf, 0.0) - jnp.log1p(jnp.exp(-jnp.abs(zf))), 0.0)
    lf_ref[...] = lf[:, :n_heads]
    fc = _prefix_sum_rows(lf) + carry_ref[0:1, :]
    carry_ref[0:1, :] = fc[tm - 1:tm, :]

    bias = -LOG2E * fc
    bi_ref[...] = jnp.max(bias, axis=0, keepdims=True)
    tails = jnp.dot(_split_bf16(bias), sel_ref[...], preferred_element_type=F32)

    lane = lax.broadcasted_iota(jnp.int32, (tm, LANES), 1)
    lane1 = lax.broadcasted_iota(jnp.int32, (1, LANES), 1)
    low = lane < HEAD_DIM
    in_tail = (lane >= HEAD_DIM) & (lane < HEAD_DIM + BIAS_TERMS)
    q_tail = jnp.where(in_tail, 1.0, 0.0)
    v_tail = jnp.where(lane == HEAD_DIM, 1.0, 0.0)
    kn = jnp.zeros((1, LANES), F32)
    for pair in range(n_heads // 2):
        sl = slice(pair * LANES, (pair + 1) * LANES)
        qt = z[:, w_ssm:w_ssm + w_attn][:, sl] * (LOG2E * HEAD_DIM ** -0.5)
        kt = z[:, w_ssm + w_attn:w_ssm + 2 * w_attn][:, sl]
        vt = z[:, w_ssm + 2 * w_attn:w_ssm + 3 * w_attn][:, sl]
        for half in range(2):
            hh = 2 * pair + half
            if half:
                qt, kt, vt = (pltpu.roll(a, HEAD_DIM, 1) for a in (qt, kt, vt))
            k_tail = jnp.where(in_tail, pltpu.roll(tails, HEAD_DIM - BIAS_TERMS * hh, 1), 0.0)
            kb = jnp.where(low, kt, k_tail).astype(BF16)
            qh_ref[hh] = jnp.where(low, qt, q_tail).astype(BF16)
            kh_ref[hh] = kb
            vh_ref[hh] = jnp.where(low, vt, v_tail).astype(BF16)
            kf = jnp.where(low, kb.astype(F32), 0.0)
            ksq = jnp.max(jnp.sum(kf * kf, axis=-1, keepdims=True), axis=0, keepdims=True)
            kn = jnp.where(lane1 == hh, ksq, kn)
    kn_ref[...] = kn


def _bias_selector(n_heads):
    sel = np.zeros((BIAS_TERMS * LANES, LANES), np.float32)
    for hh in range(n_heads):
        for c in range(BIAS_TERMS):
            sel[c * LANES + hh, BIAS_TERMS * hh + c] = 1.0
    return jnp.asarray(sel, BF16)


def _in_proj(x, g, w_main, w_f, b_f, *, n_heads, w_ssm, w_attn, tm):
    bsz, t, d = x.shape
    nt = pl.cdiv(t, tm)
    tp = nt * tm
    row = lambda b, i: (b, i, 0)
    const = lambda b, i: (0, 0)
    head = lambda b, i: (b, 0, i, 0)
    tile = lambda b, i: (b, i, 0, 0)
    sel = _bias_selector(n_heads)
    kern = functools.partial(_in_proj_kernel, t_valid=t, tm=tm, n_heads=n_heads, w_ssm=w_ssm, w_attn=w_attn)
    return pl.pallas_call(
        kern,
        grid=(bsz, nt),
        in_specs=[
            pl.BlockSpec((None, tm, d), row),
            pl.BlockSpec((1, d), const),
            pl.BlockSpec(w_main.shape, const),
            pl.BlockSpec(w_f.shape, const),
            pl.BlockSpec((1, LANES), const),
            pl.BlockSpec(sel.shape, const),
        ],
        out_specs=[
            pl.BlockSpec((None, tm, w_ssm), row),
            pl.BlockSpec((None, tm, w_attn), row),
            pl.BlockSpec((None, tm, w_attn), row),
            pl.BlockSpec((None, tm, n_heads), row),
            pl.BlockSpec((None, n_heads, tm, LANES), head),
            pl.BlockSpec((None, n_heads, tm, LANES), head),
            pl.BlockSpec((None, n_heads, tm, LANES), head),
            pl.BlockSpec((None, None, 1, LANES), tile),
            pl.BlockSpec((None, None, 1, LANES), tile),
        ],
        out_shape=[
            jax.ShapeDtypeStruct((bsz, t, w_ssm), F32),
            jax.ShapeDtypeStruct((bsz, t, w_attn), F32),
            jax.ShapeDtypeStruct((bsz, t, w_attn), F32),
            jax.ShapeDtypeStruct((bsz, t, n_heads), F32),
            jax.ShapeDtypeStruct((bsz, n_heads, tp, LANES), BF16),
            jax.ShapeDtypeStruct((bsz, n_heads, tp, LANES), BF16),
            jax.ShapeDtypeStruct((bsz, n_heads, tp, LANES), BF16),
            jax.ShapeDtypeStruct((bsz, nt, 1, LANES), F32),
            jax.ShapeDtypeStruct((bsz, nt, 1, LANES), F32),
        ],
        scratch_shapes=[pltpu.VMEM((8, LANES), F32)],
        compiler_params=_cparams(("arbitrary", "arbitrary")),
        name="in_proj",
    )(x, g, w_main, w_f, b_f, sel)


def _s5_param_kernel(are_ref, aim_ref, ldt_ref, bre_ref, bim_ref, abr_ref, abi_ref, bbr_ref, bbi_ref):
    lam_re = jnp.minimum(are_ref[...], LAMBDA_RE_MAX)
    lam_im = aim_ref[...]
    dt = jnp.exp(ldt_ref[...])
    mag = jnp.exp(lam_re * dt)
    ab_re = mag * jnp.cos(lam_im * dt)
    ab_im = mag * jnp.sin(lam_im * dt)
    abr_ref[...] = ab_re
    abi_ref[...] = ab_im
    nr = ab_re - 1.0
    ni = ab_im
    den = lam_re * lam_re + lam_im * lam_im
    cr = (nr * lam_re + ni * lam_im) / den
    ci = (ni * lam_re - nr * lam_im) / den
    b_re = bre_ref[...]
    b_im = bim_ref[...]
    bbr_ref[...] = cr * b_re - ci * b_im
    bbi_ref[...] = cr * b_im + ci * b_re


def _s5_params(a_re, a_im, log_dt, b_re, b_im):
    g, n = a_re.shape
    p = b_re.shape[-1]
    gn = g * n
    flat = lambda a: a.reshape(1, gn)
    ldt = jnp.broadcast_to(log_dt[:, None], (g, n)).reshape(1, gn)
    bt = lambda b: jnp.transpose(b, (2, 0, 1)).reshape(p, gn)
    return pl.pallas_call(
        _s5_param_kernel,
        out_shape=[jax.ShapeDtypeStruct((1, gn), F32), jax.ShapeDtypeStruct((1, gn), F32),
                   jax.ShapeDtypeStruct((p, gn), F32), jax.ShapeDtypeStruct((p, gn), F32)],
        name="s5_params",
    )(flat(a_re), flat(a_im), ldt, bt(b_re), bt(b_im))


def _s5_block_mats(bb_re, bb_im, c_re, c_im, n_groups):
    p = bb_re.shape[0]
    n = bb_re.shape[1] // n_groups
    s = n_groups // GROUP_SLAB
    eye = jnp.eye(GROUP_SLAB, dtype=F32)

    def in_blk(bb):
        b4 = bb.reshape(p, s, GROUP_SLAB, n)
        return jnp.einsum('qsgn,gh->sgqhn', b4, eye)

    b_mat = jnp.concatenate([in_blk(bb_re).reshape(s, GROUP_SLAB * p, GROUP_SLAB * n),
                             in_blk(bb_im).reshape(s, GROUP_SLAB * p, GROUP_SLAB * n)], axis=-1)

    def out_blk(c):
        c4 = c.reshape(s, GROUP_SLAB, p, n)
        return jnp.einsum('sgpn,gh->sgnhp', c4, eye)

    c_mat = jnp.concatenate([out_blk(c_re).reshape(s, GROUP_SLAB * n, GROUP_SLAB * p),
                             out_blk(-c_im).reshape(s, GROUP_SLAB * n, GROUP_SLAB * p)], axis=1)
    return b_mat.astype(BF16), c_mat.astype(BF16)


def _s5_kernel(u_ref, h0r_ref, h0i_ref, abr_ref, abi_ref, bm_ref, cm_ref, d_ref, wg_ref, bg_ref,
               y_ref, hr_out_ref, hi_out_ref, sre_ref, sim_ref, hre_ref, him_ref,
               *, t_valid, tm, n_slabs, slab_in, slab_state):
    i = pl.program_id(1)
    nt = pl.num_programs(1)

    @pl.when(i == 0)
    def _():
        hre_ref[...] = h0r_ref[...]
        him_ref[...] = h0i_ref[...]

    u = u_ref[...]
    ub = u.astype(BF16)
    for s in range(n_slabs):
        z = jnp.dot(ub[:, s * slab_in:(s + 1) * slab_in], bm_ref[s], preferred_element_type=F32)
        sre_ref[:, s * slab_state:(s + 1) * slab_state] = z[:, :slab_state]
        sim_ref[:, s * slab_state:(s + 1) * slab_state] = z[:, slab_state:]

    a_re = abr_ref[...]
    a_im = abi_ref[...]
    last_row = (t_valid - 1) % tm

    def step(t, carry):
        h_re, h_im = carry
        b_re = sre_ref[pl.ds(t, 1), :]
        b_im = sim_ref[pl.ds(t, 1), :]
        n_re = a_re * h_re - a_im * h_im + b_re
        n_im = a_re * h_im + a_im * h_re + b_im
        sre_ref[pl.ds(t, 1), :] = n_re
        sim_ref[pl.ds(t, 1), :] = n_im
        return n_re, n_im

    h_re, h_im = lax.fori_loop(0, tm, step, (hre_ref[...], him_ref[...]))
    hre_ref[...] = h_re
    him_ref[...] = h_im

    @pl.when(i == nt - 1)
    def _():
        hr_out_ref[...] = sre_ref[last_row:last_row + 1, :]
        hi_out_ref[...] = sim_ref[last_row:last_row + 1, :]

    ys = []
    for s in range(n_slabs):
        sl = slice(s * slab_state, (s + 1) * slab_state)
        cm = cm_ref[s]
        y = jnp.dot(sre_ref[:, sl].astype(BF16), cm[:slab_state], preferred_element_type=F32)
        y = y + jnp.dot(sim_ref[:, sl].astype(BF16), cm[slab_state:], preferred_element_type=F32)
        ys.append(y)
    y = jnp.concatenate(ys, axis=-1) + d_ref[...] * u
    y = jax.nn.gelu(y)
    gate = jnp.dot(y.astype(BF16), wg_ref[...], preferred_element_type=F32) + bg_ref[...]
    y_ref[...] = y * jax.nn.sigmoid(gate)


def _s5_mixer(u, h0_re, h0_im, ab_re, ab_im, b_mat, c_mat, d_skip, w_glu, b_glu, *, tm):
    bsz, t, w = u.shape
    gn = ab_re.shape[-1]
    n_slabs = b_mat.shape[0]
    nt = pl.cdiv(t, tm)
    row = lambda b, i: (b, i, 0)
    st = lambda b, i: (b, 0, 0)
    c2 = lambda b, i: (0, 0)
    c3 = lambda b, i: (0, 0, 0)
    kern = functools.partial(_s5_kernel, t_valid=t, tm=tm, n_slabs=n_slabs,
                             slab_in=w // n_slabs, slab_state=gn // n_slabs)
    return pl.pallas_call(
        kern,
        grid=(bsz, nt),
        in_specs=[
            pl.BlockSpec((None, tm, w), row),
            pl.BlockSpec((None, 1, gn), st),
            pl.BlockSpec((None, 1, gn), st),
            pl.BlockSpec((1, gn), c2),
            pl.BlockSpec((1, gn), c2),
            pl.BlockSpec(b_mat.shape, c3),
            pl.BlockSpec(c_mat.shape, c3),
            pl.BlockSpec((1, w), c2),
            pl.BlockSpec((w, w), c2),
            pl.BlockSpec((1, w), c2),
        ],
        out_specs=[
            pl.BlockSpec((None, tm, w), row),
            pl.BlockSpec((None, 1, gn), st),
            pl.BlockSpec((None, 1, gn), st),
        ],
        out_shape=[
            jax.ShapeDtypeStruct((bsz, t, w), F32),
            jax.ShapeDtypeStruct((bsz, 1, gn), F32),
            jax.ShapeDtypeStruct((bsz, 1, gn), F32),
        ],
        scratch_shapes=[pltpu.VMEM((tm, gn), F32), pltpu.VMEM((tm, gn), F32),
                        pltpu.VMEM((1, gn), F32), pltpu.VMEM((1, gn), F32)],
        compiler_params=_cparams(("arbitrary", "arbitrary")),
        name="s5_mixer",
    )(u, h0_re, h0_im, ab_re, ab_im, b_mat, c_mat, d_skip, w_glu, b_glu)


def _fox_prompt_kernel(kn_ref, bi_ref, qa_ref, qb_ref, ka_ref, kb_ref, va_ref, vb_ref, o_ref,
                       m_ref, acc_ref, qn_ref, *, tq, tk, nk, n_heads):
    i = pl.program_id(2)
    heads = ((qa_ref, ka_ref, va_ref), (qb_ref, kb_ref, vb_ref))
    m_ref[...] = jnp.full_like(m_ref, -jnp.inf)
    acc_ref[...] = jnp.zeros_like(acc_ref)
    lane = lax.broadcasted_iota(jnp.int32, (tq, LANES), 1)
    for hh, (q_ref, _, _) in enumerate(heads):
        qf = jnp.where(lane < HEAD_DIM, q_ref[...].astype(F32), 0.0)
        qn_ref[hh] = jnp.sqrt(jnp.sum(qf * qf, axis=-1, keepdims=True)) * NORM_SLACK
    table = (pl.program_id(0) * n_heads + 2 * pl.program_id(1)) * nk

    def reaches(j):
        hit = False
        for hh in range(2):
            kn = kn_ref[table + hh * nk + j]
            bi = bi_ref[table + hh * nk + j]
            slack = jnp.min(m_ref[hh] - qn_ref[hh] * kn)
            hit = jnp.logical_or(hit, slack <= bi + UNDERFLOW_LOG2)
        return hit

    row0 = i * tq
    diag = row0 // tk

    def block(j, masked):
        start = pl.multiple_of(j * tk, tk)
        for hh, (q_ref, k_ref, v_ref) in enumerate(heads):
            kj = k_ref[pl.ds(start, tk), :]
            vj = v_ref[pl.ds(start, tk), :]
            g = lax.dot_general(q_ref[...], kj, (((1,), (1,)), ((), ())), preferred_element_type=F32)
            if masked:
                r = row0 + lax.broadcasted_iota(jnp.int32, (tq, tk), 0)
                c = start + lax.broadcasted_iota(jnp.int32, (tq, tk), 1)
                g = jnp.where(c <= r, g, -jnp.inf)
            m_old = m_ref[hh]
            m_new = jnp.maximum(m_old, jnp.max(g, axis=-1, keepdims=True))
            p = jnp.exp2(g - m_new)
            alpha = jnp.exp2(m_old - m_new)
            acc_ref[hh] = alpha * acc_ref[hh] + jnp.dot(p.astype(BF16), vj, preferred_element_type=F32)
            m_ref[hh] = m_new

    block(diag, True)

    def cond(c):
        t, go = c
        return jnp.logical_and(t < diag, go)

    def body(c):
        t, _ = c
        j = diag - 1 - t
        block(j, False)
        return t + 1, reaches(jnp.maximum(j - 1, 0))

    lax.while_loop(cond, body, (0, reaches(jnp.maximum(diag - 1, 0))))
    outs = []
    for hh in range(2):
        acc = acc_ref[hh]
        outs.append(acc[:, :HEAD_DIM] / acc[:, HEAD_DIM:HEAD_DIM + 1])
    o_ref[...] = jnp.concatenate(outs, axis=-1)


def _fox_prompt(qh, kh, vh, kn_sq, bias_max, *, tq):
    bsz, nh, tp, wl = qh.shape
    nq = tp // tq
    nk = kn_sq.shape[1]
    tk = tp // nk
    assert tk % tq == 0
    stat = lambda a: lax.cummax(jnp.transpose(a[:, :, 0, :nh], (0, 2, 1)), axis=2).reshape(-1)
    kn_tab = stat(jnp.sqrt(kn_sq) * NORM_SLACK)
    bi_tab = stat(bias_max)
    qa = lambda b, p, i, kn, bi: (b, 2 * p, i, 0)
    qb = lambda b, p, i, kn, bi: (b, 2 * p + 1, i, 0)
    fa = lambda b, p, i, kn, bi: (b, 2 * p, 0, 0)
    fb = lambda b, p, i, kn, bi: (b, 2 * p + 1, 0, 0)
    q_spec = lambda im: pl.BlockSpec((None, None, tq, wl), im)
    kv_spec = lambda im: pl.BlockSpec((None, None, tp, wl), im)
    grid_spec = pltpu.PrefetchScalarGridSpec(
        num_scalar_prefetch=2,
        grid=(bsz, nh // 2, nq),
        in_specs=[q_spec(qa), q_spec(qb), kv_spec(fa), kv_spec(fb), kv_spec(fa), kv_spec(fb)],
        out_specs=pl.BlockSpec((None, tq, 2 * HEAD_DIM), lambda b, p, i, kn, bi: (b, i, p)),
        scratch_shapes=[pltpu.VMEM((2, tq, 1), F32), pltpu.VMEM((2, tq, wl), F32), pltpu.VMEM((2, tq, 1), F32)],
    )
    return pl.pallas_call(
        functools.partial(_fox_prompt_kernel, tq=tq, tk=tk, nk=nk, n_heads=nh),
        grid_spec=grid_spec,
        out_shape=jax.ShapeDtypeStruct((bsz, tp, nh * HEAD_DIM), F32),
        compiler_params=_cparams(("arbitrary", "arbitrary", "arbitrary")),
        name="fox_prompt",
    )(kn_tab, bi_tab, qh, qh, kh, kh, vh, vh)


def _fox_sample_kernel(q_ref, kn_ref, vn_ref, ck_ref, cv_ref, bc_ref, bn_ref, o_ref, *, n_heads):
    tq = q_ref.shape[1]
    r = lax.broadcasted_iota(jnp.int32, (tq, tq), 0)
    c = lax.broadcasted_iota(jnp.int32, (tq, tq), 1)
    outs = []
    dn = (((1,), (1,)), ((), ()))
    for hh in range(n_heads):
        sl = slice(hh * HEAD_DIM, (hh + 1) * HEAD_DIM)
        q = q_ref[hh][:, :HEAD_DIM]
        kn = kn_ref[hh][:, :HEAD_DIM]
        vn = vn_ref[hh][:, :HEAD_DIM]
        kc = ck_ref[:, sl].astype(BF16)
        vc = cv_ref[:, sl].astype(BF16)
        g_c = lax.dot_general(q, kc, dn, preferred_element_type=F32) + bc_ref[hh]
        g_n = lax.dot_general(q, kn, dn, preferred_element_type=F32) + bn_ref[hh]
        g_n = jnp.where(c <= r, g_n, -jnp.inf)
        m = jnp.maximum(jnp.max(g_c, axis=-1, keepdims=True), jnp.max(g_n, axis=-1, keepdims=True))
        p_c = jnp.exp2(g_c - m)
        p_n = jnp.exp2(g_n - m)
        den = jnp.sum(p_c, axis=-1, keepdims=True) + jnp.sum(p_n, axis=-1, keepdims=True)
        num = jnp.dot(p_c.astype(BF16), vc, preferred_element_type=F32)
        num = num + jnp.dot(p_n.astype(BF16), vn, preferred_element_type=F32)
        outs.append(num / den)
    o_ref[...] = jnp.concatenate(outs, axis=-1)


def _fox_sample(qh, kh, vh, cache_k, cache_v, bias_cache, bias_new):
    bsz, nh, tq, wl = qh.shape
    past = cache_k.shape[1]
    w = nh * HEAD_DIM
    b4 = lambda b: (b, 0, 0, 0)
    b3 = lambda b: (b, 0, 0)
    return pl.pallas_call(
        functools.partial(_fox_sample_kernel, n_heads=nh),
        grid=(bsz,),
        in_specs=[pl.BlockSpec((None, nh, tq, wl), b4), pl.BlockSpec((None, nh, tq, wl), b4),
                  pl.BlockSpec((None, nh, tq, wl), b4),
                  pl.BlockSpec((None, past, w), b3), pl.BlockSpec((None, past, w), b3),
                  pl.BlockSpec((None, nh, 1, past), b4), pl.BlockSpec((None, nh, 1, tq), b4)],
        out_specs=pl.BlockSpec((None, tq, w), b3),
        out_shape=jax.ShapeDtypeStruct((bsz, tq, w), F32),
        compiler_params=_cparams(("arbitrary",)),
        name="fox_sample",
    )(qh, kh, vh, cache_k, cache_v, bias_cache, bias_new)


def _cumsum_kernel(x_ref, o_ref, carry_ref, *, tm):
    i = pl.program_id(1)

    @pl.when(i == 0)
    def _():
        carry_ref[...] = jnp.zeros_like(carry_ref)

    fc = _prefix_sum_rows(x_ref[...]) + carry_ref[0:1, :]
    o_ref[...] = fc
    carry_ref[0:1, :] = fc[tm - 1:tm, :]


def _cumsum_time(x, *, tm):
    bsz, t, w = x.shape
    row = lambda b, i: (b, i, 0)
    return pl.pallas_call(
        functools.partial(_cumsum_kernel, tm=tm),
        grid=(bsz, t // tm),
        in_specs=[pl.BlockSpec((None, tm, w), row)],
        out_specs=pl.BlockSpec((None, tm, w), row),
        out_shape=jax.ShapeDtypeStruct((bsz, t, w), F32),
        scratch_shapes=[pltpu.VMEM((8, w), F32)],
        compiler_params=_cparams(("arbitrary", "arbitrary")),
        name="cumsum_time",
    )(x)


def _merge_router_kernel(x_ref, ys_ref, ya_ref, gs_ref, ga_ref, wo_ref, gf_ref, wr_ref, br_ref, cnt_in_ref,
                         x2_ref, ti_ref, tw_ref, tp_ref, cnt_out_ref, cnt_ref, *, t_valid, tm):
    first = (pl.program_id(0) == 0) & (pl.program_id(1) == 0)

    @pl.when(first)
    def _():
        cnt_ref[...] = jnp.broadcast_to(cnt_in_ref[...], cnt_ref.shape)

    mix = jnp.concatenate([_rms(ys_ref[...], gs_ref[...]), _rms(ya_ref[...], ga_ref[...])], axis=-1)
    x2 = x_ref[...] + jnp.dot(mix.astype(BF16), wo_ref[...], preferred_element_type=F32)
    x2_ref[...] = x2
    h2 = _rms(x2, gf_ref[...])
    h_hi = h2.astype(BF16)
    h_lo = (h2 - h_hi.astype(F32)).astype(BF16)
    logits = (jnp.dot(h_hi, wr_ref[0], preferred_element_type=F32)
              + jnp.dot(h_lo, wr_ref[0], preferred_element_type=F32)
              + jnp.dot(h_hi, wr_ref[1], preferred_element_type=F32)) + br_ref[...]

    lane = lax.broadcasted_iota(jnp.int32, (tm, LANES), 1)
    rows = pl.program_id(1) * tm + lax.broadcasted_iota(jnp.int32, (tm, 1), 0)
    valid = rows < t_valid
    vals = logits
    top_v, top_i, sels = [], [], []
    for _ in range(TOP_K):
        mx = jnp.max(vals, axis=-1, keepdims=True)
        idx = jnp.min(jnp.where(vals == mx, lane, LANES), axis=-1, keepdims=True)
        sel = lane == idx
        vals = jnp.where(sel, -jnp.inf, vals)
        top_v.append(mx)
        top_i.append(idx)
        sels.append(sel)
    ex = [jnp.exp(v - top_v[0]) for v in top_v]
    den = ex[0] + ex[1] + ex[2] + ex[3]

    member = jnp.zeros((tm, LANES), F32)
    for sel in sels:
        member = member + jnp.where(valid, jnp.where(sel, 1.0, 0.0), 0.0)
    r = lax.broadcasted_iota(jnp.int32, (tm, tm), 0)
    c = lax.broadcasted_iota(jnp.int32, (tm, tm), 1)
    tri = (c < r).astype(BF16)
    before = jnp.dot(tri, member.astype(BF16), preferred_element_type=F32) + cnt_ref[0:1, :]
    cnt_new = cnt_ref[0:1, :] + jnp.sum(member, axis=0, keepdims=True)
    cnt_ref[0:1, :] = cnt_new
    cnt_out_ref[...] = cnt_new

    ti = jnp.zeros((tm, LANES), jnp.int32)
    tw = jnp.zeros((tm, LANES), F32)
    tpos = jnp.zeros((tm, LANES), F32)
    for kk in range(TOP_K):
        pos = jnp.sum(jnp.where(sels[kk], before, 0.0), axis=-1, keepdims=True)
        ti = jnp.where(lane == kk, top_i[kk], ti)
        tw = jnp.where(lane == kk, ex[kk] / den, tw)
        tpos = jnp.where(lane == kk, pos, tpos)
    ti_ref[...] = ti
    tw_ref[...] = tw
    tp_ref[...] = tpos.astype(jnp.int32)


def _merge_router(x, ys, ya, g_ssm, g_attn, w_out, g_ffn, w_router, b_router, cnt_in, *, tm):
    bsz, t, d = x.shape
    w = ys.shape[-1]
    nt = pl.cdiv(t, tm)
    row = lambda b, i: (b, i, 0)
    c2 = lambda b, i: (0, 0)
    return pl.pallas_call(
        functools.partial(_merge_router_kernel, t_valid=t, tm=tm),
        grid=(bsz, nt),
        in_specs=[
            pl.BlockSpec((None, tm, d), row),
            pl.BlockSpec((None, tm, w), row),
            pl.BlockSpec((None, tm, w), row),
            pl.BlockSpec((1, w), c2), pl.BlockSpec((1, w), c2),
            pl.BlockSpec(w_out.shape, c2),
            pl.BlockSpec((1, d), c2),
            pl.BlockSpec(w_router.shape, lambda b, i: (0, 0, 0)),
            pl.BlockSpec((1, LANES), c2),
            pl.BlockSpec((1, LANES), c2),
        ],
        out_specs=[
            pl.BlockSpec((None, tm, d), row),
            pl.BlockSpec((None, tm, LANES), row),
            pl.BlockSpec((None, tm, LANES), row),
            pl.BlockSpec((None, tm, LANES), row),
            pl.BlockSpec((1, LANES), c2),
        ],
        out_shape=[
            jax.ShapeDtypeStruct((bsz, t, d), F32),
            jax.ShapeDtypeStruct((bsz, t, LANES), jnp.int32),
            jax.ShapeDtypeStruct((bsz, t, LANES), F32),
            jax.ShapeDtypeStruct((bsz, t, LANES), jnp.int32),
            jax.ShapeDtypeStruct((1, LANES), F32),
        ],
        scratch_shapes=[pltpu.VMEM((8, LANES), F32)],
        compiler_params=_cparams(("arbitrary", "arbitrary")),
        name="merge_router",
    )(x, ys, ya, g_ssm, g_attn, w_out, g_ffn, w_router, b_router, cnt_in)


def _dispatch_kernel(last_tile_ref, nu_ref, slots_ref, xa_ref, xb_ref, g_ref, xs_hbm, hbuf_ref, zero_ref, zsem, sem,
                     *, t_valid, nt_a, n_steps_a, tm, n_experts, n_tiles, tile):
    step = pl.program_id(0)

    def zero_tile(t):
        return pltpu.make_async_copy(zero_ref, xs_hbm.at[pl.ds(pl.multiple_of(t * tile, tile), tile)], zsem)

    @pl.when(step == 0)
    def _():
        zero_ref[...] = jnp.zeros_like(zero_ref)

        def start_tail(t, c):
            zero_tile(t).start()
            return c

        def wait_tail(t, c):
            zero_tile(t).wait()
            return c

        lax.fori_loop(nu_ref[0], n_tiles, start_tail, 0)
        lax.fori_loop(nu_ref[0], n_tiles, wait_tail, 0)
        for e in range(n_experts):
            lt = last_tile_ref[e]

            @pl.when(lt >= 0)
            def _():
                zero_tile(lt).start()
        for e in range(n_experts):
            lt = last_tile_ref[e]

            @pl.when(lt >= 0)
            def _():
                zero_tile(lt).wait()

    in_a = step < n_steps_a

    @pl.when(in_a)
    def _():
        hbuf_ref[...] = _rms(xa_ref[...], g_ref[...])

    @pl.when(jnp.logical_not(in_a))
    def _():
        hbuf_ref[...] = _rms(xb_ref[...], g_ref[...])

    def row_copy(r, kk):
        return pltpu.make_async_copy(hbuf_ref.at[pl.ds(r, 1)], xs_hbm.at[pl.ds(slots_ref[r * TOP_K + kk], 1)], sem)

    def issue(r, c):
        for kk in range(TOP_K):
            row_copy(r, kk).start()
        return c

    def drain(r, c):
        for kk in range(TOP_K):
            row_copy(r, kk).wait()
        return c

    n_valid = jnp.where(in_a, jnp.minimum(tm, t_valid - (step % nt_a) * tm), tm)

    @pl.when(n_valid == tm)
    def _():
        lax.fori_loop(0, tm, issue, 0, unroll=ISSUE_UNROLL)
        for _ in range(TOP_K):
            pltpu.make_async_copy(hbuf_ref, xs_hbm.at[pl.ds(0, tm)], sem).wait()

    @pl.when(n_valid < tm)
    def _():
        lax.fori_loop(0, n_valid, issue, 0)
        lax.fori_loop(0, n_valid, drain, 0)


def _dispatch(xa, xb, g_ffn, slots_flat, last_tile, n_used, *, n_slots, tm):
    bsz, t, d = xa.shape
    nt_a = pl.cdiv(t, tm)
    n_steps_a = bsz * nt_a
    n_steps_b = xb.shape[0] // tm
    n_experts = last_tile.shape[0]
    grid_spec = pltpu.PrefetchScalarGridSpec(
        num_scalar_prefetch=2,
        grid=(n_steps_a + n_steps_b,),
        in_specs=[
            pl.BlockSpec((tm * TOP_K,), lambda s, lt, nu: (s,), memory_space=pltpu.SMEM),
            pl.BlockSpec((None, tm, d), lambda s, lt, nu: (jnp.minimum(s, n_steps_a - 1) // nt_a,
                                                          jnp.minimum(s, n_steps_a - 1) % nt_a, 0)),
            pl.BlockSpec((tm, d), lambda s, lt, nu: (jnp.maximum(s - n_steps_a, 0), 0)),
            pl.BlockSpec((1, d), lambda s, lt, nu: (0, 0)),
        ],
        out_specs=pl.BlockSpec(memory_space=pl.ANY),
        scratch_shapes=[pltpu.VMEM((tm, d), F32), pltpu.VMEM((EXPERT_TILE, d), F32),
                        pltpu.SemaphoreType.DMA, pltpu.SemaphoreType.DMA],
    )
    return pl.pallas_call(
        functools.partial(_dispatch_kernel, t_valid=t, nt_a=nt_a, n_steps_a=n_steps_a, tm=tm, n_experts=n_experts,
                          n_tiles=n_slots // EXPERT_TILE, tile=EXPERT_TILE),
        grid_spec=grid_spec,
        out_shape=jax.ShapeDtypeStruct((n_slots, d), F32),
        compiler_params=_cparams(("arbitrary",)),
        name="moe_dispatch",
    )(last_tile, n_used, slots_flat, xa, xb, g_ffn)


def _expert_kernel(te_ref, nu_ref, x_ref, w1_ref, b1_ref, w2_ref, b2_ref, y_ref, w1b_ref, w2b_ref, *, d_ff):
    i = pl.program_id(0)
    used = nu_ref[0]
    prev = te_ref[jnp.maximum(i - 1, 0)]
    changed = (i == 0) | (te_ref[i] != prev)

    @pl.when((i < used) & changed)
    def _():
        w1b_ref[...] = w1_ref[...].astype(BF16)
        w2b_ref[...] = w2_ref[...].astype(BF16)

    @pl.when(i >= used)
    def _():
        y_ref[...] = jnp.zeros_like(y_ref)

    @pl.when(i < used)
    def _():
        x = x_ref[...].astype(BF16)
        h = jnp.dot(x, w1b_ref[...], preferred_element_type=F32) + b1_ref[...]
        hg = jnp.minimum(h[:, :d_ff], SWIGLU_LIMIT)
        hl = jnp.clip(h[:, d_ff:], -SWIGLU_LIMIT, SWIGLU_LIMIT)
        act = (hl + 1.0) * (hg * jax.nn.sigmoid(SWIGLU_ALPHA * hg))
        y_ref[...] = jnp.dot(act.astype(BF16), w2b_ref[...], preferred_element_type=F32) + b2_ref[...]


def _experts(xs, tile_expert, n_used, w1, b1, w2, b2):
    s, d = xs.shape
    n_tiles = s // EXPERT_TILE
    n_e, _, two_ff = w1.shape
    d_ff = two_ff // 2

    def xmap(i, te, nu):
        return (i, 0)

    def wmap(i, te, nu):
        return (te[i], 0, 0)

    grid_spec = pltpu.PrefetchScalarGridSpec(
        num_scalar_prefetch=2,
        grid=(n_tiles,),
        in_specs=[
            pl.BlockSpec((EXPERT_TILE, d), xmap),
            pl.BlockSpec((None, d, two_ff), wmap),
            pl.BlockSpec((None, 1, two_ff), wmap),
            pl.BlockSpec((None, d_ff, d), wmap),
            pl.BlockSpec((None, 1, d), wmap),
        ],
        out_specs=pl.BlockSpec((EXPERT_TILE, d), xmap),
        scratch_shapes=[pltpu.VMEM((d, two_ff), BF16), pltpu.VMEM((d_ff, d), BF16)],
    )
    return pl.pallas_call(
        functools.partial(_expert_kernel, d_ff=d_ff),
        grid_spec=grid_spec,
        out_shape=jax.ShapeDtypeStruct((s, d), F32),
        compiler_params=_cparams(("arbitrary",)),
        name="moe_experts",
    )(tile_expert, n_used, xs, w1, b1.reshape(n_e, 1, two_ff), w2, b2.reshape(n_e, 1, d))


def _combine_kernel(slots_ref, g_ref, x2_hbm, tw_hbm, ys_hbm, o_ref, xbuf_ref, wbuf_ref, buf_ref, sem, lsem,
                    *, tm, row_offset):
    b = pl.program_id(0)
    start = pl.multiple_of(row_offset + pl.program_id(1) * tm, 8)
    x_copy = pltpu.make_async_copy(x2_hbm.at[b, pl.ds(start, tm)], xbuf_ref, lsem)
    w_copy = pltpu.make_async_copy(tw_hbm.at[b, pl.ds(start, tm)], wbuf_ref, lsem)
    x_copy.start()
    w_copy.start()

    def issue(r, c):
        for kk in range(TOP_K):
            pltpu.make_async_copy(ys_hbm.at[pl.ds(slots_ref[r * TOP_K + kk], 1)],
                                  buf_ref.at[kk, pl.ds(r, 1)], sem).start()
        return c

    lax.fori_loop(0, tm, issue, 0, unroll=ISSUE_UNROLL)
    x_copy.wait()
    w_copy.wait()
    for kk in range(TOP_K):
        pltpu.make_async_copy(ys_hbm.at[pl.ds(0, tm)], buf_ref.at[kk], sem).wait()
    tw = wbuf_ref[...]
    acc = xbuf_ref[...]
    for kk in range(TOP_K):
        acc = acc + tw[:, kk:kk + 1] * buf_ref[kk]
    o_ref[...] = _rms(acc, g_ref[...])


def _combine(x2, tw, slots_flat, ys, g_final, *, tm, row_offset):
    bsz, t, d = x2.shape
    t_out = t - row_offset
    nt = t_out // tm
    grid_spec = pltpu.PrefetchScalarGridSpec(
        num_scalar_prefetch=0,
        grid=(bsz, nt),
        in_specs=[
            pl.BlockSpec((tm * TOP_K,), lambda b, i: (b * nt + i,), memory_space=pltpu.SMEM),
            pl.BlockSpec((1, d), lambda b, i: (0, 0)),
            pl.BlockSpec(memory_space=pl.ANY),
            pl.BlockSpec(memory_space=pl.ANY),
            pl.BlockSpec(memory_space=pl.ANY),
        ],
        out_specs=pl.BlockSpec((None, tm, d), lambda b, i: (b, i, 0)),
        scratch_shapes=[pltpu.VMEM((tm, d), F32), pltpu.VMEM((tm, LANES), F32),
                        pltpu.VMEM((TOP_K, tm, d), F32), pltpu.SemaphoreType.DMA, pltpu.SemaphoreType.DMA],
    )
    return pl.pallas_call(
        functools.partial(_combine_kernel, tm=tm, row_offset=row_offset),
        grid_spec=grid_spec,
        out_shape=jax.ShapeDtypeStruct((bsz, t_out, d), F32),
        compiler_params=_cparams(("arbitrary", "arbitrary")),
        name="moe_combine",
    )(slots_flat, g_final, x2, tw, ys)


def _pad_lanes(a, value=0.0):
    return jnp.pad(a, [(0, 0)] * (a.ndim - 1) + [(0, LANES - a.shape[-1])], constant_values=value)


def kernel(x_prompt, x_sample, cache_k, cache_v, cache_logf, state_ssm_re, state_ssm_im, meta_tokens, norm_mix_g, w_in, b_forget, ssm_a_re, ssm_a_im, ssm_log_dt, ssm_b_re, ssm_b_im, ssm_c_re, ssm_c_im, ssm_d, w_glu, b_glu, g_out_ssm, g_out_attn, w_out, norm_ffn_g, w_router, b_router, w_mlp1, b_mlp1, w_mlp2, b_mlp2, norm_final_g):
    depth = w_in.shape[0]
    assert depth == 1, "the routing tables below are built for a single trunk layer"
    l = 0
    bp, seq, d = x_prompt.shape
    bs, ts, _ = x_sample.shape
    n_heads = b_forget.shape[1]
    n_groups, ssm_n = ssm_a_re.shape[1:]
    n_experts = w_router.shape[2]
    w_ssm = n_groups * SSM_P
    w_attn = n_heads * HEAD_DIM
    gn = n_groups * ssm_n
    past = cache_k.shape[2]
    assert seq % ROW_TILE == 0 and (bs * ts) % ROW_TILE == 0

    meta = jnp.broadcast_to(meta_tokens.astype(x_prompt.dtype)[None], (bp, N_META, d))
    xp = jnp.concatenate([meta, x_prompt], axis=1)
    xs = x_sample
    tp = xp.shape[1]

    w_main = w_in[l][:, :w_ssm + 3 * w_attn].astype(BF16)
    w_f = _pad_lanes(w_in[l][:, w_ssm + 3 * w_attn:]).astype(BF16)
    b_f = _pad_lanes(b_forget[l][None, :])
    g_mix = norm_mix_g[l][None, :]
    ab_re, ab_im, bb_re, bb_im = _s5_params(ssm_a_re[l], ssm_a_im[l], ssm_log_dt[l], ssm_b_re[l], ssm_b_im[l])
    b_mat, c_mat = _s5_block_mats(bb_re, bb_im, ssm_c_re[l], ssm_c_im[l], n_groups)

    proj = functools.partial(_in_proj, n_heads=n_heads, w_ssm=w_ssm, w_attn=w_attn)
    up, kp, vp, lfp, qhp, khp, vhp, knp, bip = proj(xp, g_mix, w_main, w_f, b_f, tm=ROW_TILE)
    us, ks, vs, lfs, qhs, khs, vhs, _, _ = proj(xs, g_mix, w_main, w_f, b_f, tm=ts)

    s5 = functools.partial(_s5_mixer, ab_re=ab_re, ab_im=ab_im, b_mat=b_mat, c_mat=c_mat,
                           d_skip=ssm_d[l].reshape(1, w_ssm), w_glu=w_glu[l].astype(BF16), b_glu=b_glu[l][None, :])
    zeros_state = jnp.zeros((bp, 1, gn), F32)
    ysp, hrp, hip = s5(up, zeros_state, zeros_state, tm=ROW_TILE)
    yss, hrs, his = s5(us, state_ssm_re[l].reshape(bs, 1, gn), state_ssm_im[l].reshape(bs, 1, gn), tm=ts)

    yap = _fox_prompt(qhp, khp, vhp, knp, bip, tq=ATTN_Q_TILE)
    t_all = past + ts
    assert bs * n_heads == LANES
    tm_cum = max(m for m in range(8, ROW_TILE + 1, 8) if t_all % m == 0)
    lf_all = jnp.concatenate([cache_logf[l].astype(F32), lfs], axis=1)
    fcs = _cumsum_time(jnp.transpose(lf_all, (1, 0, 2)).reshape(1, t_all, LANES), tm=tm_cum)
    bias_s = -LOG2E * jnp.transpose(fcs.reshape(t_all, bs, n_heads), (1, 2, 0))[:, :, None, :]
    yas = _fox_sample(qhs, khs, vhs, cache_k[l].reshape(bs, past, w_attn), cache_v[l].reshape(bs, past, w_attn),
                      bias_s[..., :past], bias_s[..., past:])

    wr = _pad_lanes(w_router[l])
    wr_hi = wr.astype(BF16)
    wr_parts = jnp.stack([wr_hi, (wr - wr_hi.astype(F32)).astype(BF16)])
    merge = functools.partial(_merge_router, g_ssm=g_out_ssm[l][None, :], g_attn=g_out_attn[l][None, :],
                              w_out=w_out[l].astype(BF16), g_ffn=norm_ffn_g[l][None, :],
                              w_router=wr_parts, b_router=_pad_lanes(b_router[l][None, :], value=-1e30))
    x2p, tip, twp, tpp, cnt_p = merge(xp, ysp, yap, cnt_in=jnp.zeros((1, LANES), F32), tm=ROW_TILE)
    x2s, tis, tws, tps, cnt = merge(xs, yss, yas, cnt_in=cnt_p, tm=ts)

    n_tok = bp * tp + bs * ts
    counts = cnt[0, :n_experts].astype(jnp.int32)
    tiles_per = (counts + EXPERT_TILE - 1) // EXPERT_TILE
    tile_end = jnp.cumsum(tiles_per)
    tile_start = tile_end - tiles_per
    n_tiles = (n_tok * TOP_K) // EXPERT_TILE + n_experts
    n_slots = n_tiles * EXPERT_TILE
    n_used = tile_end[-1:].astype(jnp.int32)
    tile_ids = jnp.minimum(jnp.arange(n_tiles, dtype=jnp.int32), n_used - 1)
    tile_expert = jnp.sum((tile_end[None, :] <= tile_ids[:, None]).astype(jnp.int32), axis=1)
    last_tile = jnp.where(tiles_per > 0, tile_end - 1, -1).astype(jnp.int32)
    slot_base = (tile_start * EXPERT_TILE).astype(F32)

    def slots_of(ids, pos):
        onehot = jax.nn.one_hot(ids[..., :TOP_K], n_experts, dtype=F32)
        base = jnp.einsum('btke,e->btk', onehot, slot_base, precision=lax.Precision.HIGHEST)
        return base.astype(jnp.int32) + pos[..., :TOP_K]

    slots_p = slots_of(tip, tpp)
    slots_s = slots_of(tis, tps)
    tpad = pl.cdiv(tp, ROW_TILE) * ROW_TILE
    slots_disp = jnp.concatenate([jnp.pad(slots_p, ((0, 0), (0, tpad - tp), (0, 0))).reshape(-1),
                                  slots_s.reshape(-1)])
    xs_sorted = _dispatch(x2p, x2s.reshape(bs * ts, d), norm_ffn_g[l][None, :], slots_disp, last_tile, n_used,
                          n_slots=n_slots, tm=ROW_TILE)
    ys_sorted = _experts(xs_sorted, tile_expert, n_used, w_mlp1[l], b_mlp1[l], w_mlp2[l], b_mlp2[l])

    g_fin = norm_final_g[None, :]
    y_prompt = _combine(x2p, twp, slots_p[:, N_META:].reshape(-1), ys_sorted, g_fin, tm=EXPERT_TILE, row_offset=N_META)
    y_sample = _combine(x2s.reshape(1, bs * ts, d), tws.reshape(1, bs * ts, LANES), slots_s.reshape(-1), ys_sorted,
                        g_fin, tm=EXPERT_TILE, row_offset=0).reshape(bs, ts, d)

    hd = HEAD_DIM
    st = lambda a, b: a.reshape(1, b, n_groups, ssm_n)
    return (y_prompt, y_sample,
            kp.reshape(1, bp, tp, n_heads, hd), vp.reshape(1, bp, tp, n_heads, hd), lfp[None],
            st(hrp, bp), st(hip, bp),
            ks.reshape(1, bs, ts, n_heads, hd), vs.reshape(1, bs, ts, n_heads, hd), lfs[None],
            st(hrs, bs), st(his, bs))
```

```python
import functools
import math

import numpy as np
import jax
import jax.numpy as jnp
from jax import lax
from jax.experimental import pallas as pl
from jax.experimental.pallas import tpu as pltpu

F32 = jnp.float32
BF16 = jnp.bfloat16

LANES = 128
VMEM_LIMIT_BYTES = 56 * 1024 * 1024

N_META = 16
HEAD_DIM = 64
SSM_P = 16
SSM_N = 64
TOP_K = 4
SWIGLU_LIMIT = 7.0
SWIGLU_ALPHA = 1.702
RMS_EPS = 1e-6
LAMBDA_RE_MAX = -1e-4
LOG2E = math.log2(math.e)

ROW_TILE = 512
ATTN_Q_TILE = 512
EXPERT_TILE = 256
GROUP_SLAB = 8
BIAS_TERMS = 3
ISSUE_UNROLL = 8
UNDERFLOW_LOG2 = 160.0
NORM_SLACK = 1.001


def _cparams(sem):
    return pltpu.CompilerParams(dimension_semantics=sem, vmem_limit_bytes=VMEM_LIMIT_BYTES)


def _rms(v, g):
    return v * lax.rsqrt(jnp.mean(v * v, axis=-1, keepdims=True) + RMS_EPS) * g


def _split_bf16(a):
    pieces = []
    rest = a
    for _ in range(BIAS_TERMS):
        piece = rest.astype(BF16)
        pieces.append(piece)
        rest = rest - piece.astype(F32)
    return jnp.concatenate(pieces, axis=-1)


def _prefix_sum_rows(x):
    tm = x.shape[0]
    r = lax.broadcasted_iota(jnp.int32, (tm, tm), 0)
    c = lax.broadcasted_iota(jnp.int32, (tm, tm), 1)
    tri = jnp.where(c <= r, 1.0, 0.0).astype(BF16)
    s = jnp.dot(tri, _split_bf16(x), preferred_element_type=F32)
    return sum(s[:, t * LANES:(t + 1) * LANES] for t in range(BIAS_TERMS))


def _in_proj_kernel(x_ref, g_ref, w_ref, wf_ref, bf_ref, sel_ref,
                    u_ref, k_ref, v_ref, lf_ref, qh_ref, kh_ref, vh_ref, kn_ref, bi_ref, carry_ref,
                    *, t_valid, tm, n_heads, w_ssm, w_attn):
    i = pl.program_id(1)

    @pl.when(i == 0)
    def _():
        carry_ref[...] = jnp.zeros_like(carry_ref)

    x = x_ref[...]
    h = _rms(x, g_ref[...]).astype(BF16)
    rows = i * tm + lax.broadcasted_iota(jnp.int32, (tm, 1), 0)
    valid = rows < t_valid
    z = jnp.where(valid, jnp.dot(h, w_ref[...], preferred_element_type=F32), 0.0)
    u_ref[...] = z[:, :w_ssm]
    k_ref[...] = z[:, w_ssm + w_attn:w_ssm + 2 * w_attn]
    v_ref[...] = z[:, w_ssm + 2 * w_attn:w_ssm + 3 * w_attn]

    zf = jnp.dot(h, wf_ref[...], preferred_element_type=F32) + bf_ref[...]
    lf = jnp.where(valid, jnp.minimum(zf, 0.0) - jnp.log1p(jnp.exp(-jnp.abs(zf))), 0.0)
    lf_ref[...] = lf[:, :n_heads]
    fc = _prefix_sum_rows(lf) + carry_ref[0:1, :]
    carry_ref[0:1, :] = fc[tm - 1:tm, :]

    bias = -LOG2E * fc
    bi_ref[...] = jnp.max(bias, axis=0, keepdims=True)
    tails = jnp.dot(_split_bf16(bias), sel_ref[...], preferred_element_type=F32)

    lane = lax.broadcasted_iota(jnp.int32, (tm, LANES), 1)
    lane1 = lax.broadcasted_iota(jnp.int32, (1, LANES), 1)
    low = lane < HEAD_DIM
    in_tail = (lane >= HEAD_DIM) & (lane < HEAD_DIM + BIAS_TERMS)
    q_tail = jnp.where(in_tail, 1.0, 0.0)
    v_tail = jnp.where(lane == HEAD_DIM, 1.0, 0.0)
    kn = jnp.zeros((1, LANES), F32)
    for pair in range(n_heads // 2):
        sl = slice(pair * LANES, (pair + 1) * LANES)
        qt = z[:, w_ssm:w_ssm + w_attn][:, sl] * (LOG2E * HEAD_DIM ** -0.5)
        kt = z[:, w_ssm + w_attn:w_ssm + 2 * w_attn][:, sl]
        vt = z[:, w_ssm + 2 * w_attn:w_ssm + 3 * w_attn][:, sl]
        for half in range(2):
            hh = 2 * pair + half
            if half:
                qt, kt, vt = (pltpu.roll(a, HEAD_DIM, 1) for a in (qt, kt, vt))
            k_tail = jnp.where(in_tail, pltpu.roll(tails, HEAD_DIM - BIAS_TERMS * hh, 1), 0.0)
            kb = jnp.where(low, kt, k_tail).astype(BF16)
            qh_ref[hh] = jnp.where(low, qt, q_tail).astype(BF16)
            kh_ref[hh] = kb
            vh_ref[hh] = jnp.where(low, vt, v_tail).astype(BF16)
            kf = jnp.where(low, kb.astype(F32), 0.0)
            ksq = jnp.max(jnp.sum(kf * kf, axis=-1, keepdims=True), axis=0, keepdims=True)
            kn = jnp.where(lane1 == hh, ksq, kn)
    kn_ref[...] = kn


def _bias_selector(n_heads):
    sel = np.zeros((BIAS_TERMS * LANES, LANES), np.float32)
    for hh in range(n_heads):
        for c in range(BIAS_TERMS):
            sel[c * LANES + hh, BIAS_TERMS * hh + c] = 1.0
    return jnp.asarray(sel, BF16)


def _in_proj(x, g, w_main, w_f, b_f, *, n_heads, w_ssm, w_attn, tm):
    bsz, t, d = x.shape
    nt = pl.cdiv(t, tm)
    tp = nt * tm
    row = lambda b, i: (b, i, 0)
    const = lambda b, i: (0, 0)
    head = lambda b, i: (b, 0, i, 0)
    tile = lambda b, i: (b, i, 0, 0)
    sel = _bias_selector(n_heads)
    kern = functools.partial(_in_proj_kernel, t_valid=t, tm=tm, n_heads=n_heads, w_ssm=w_ssm, w_attn=w_attn)
    return pl.pallas_call(
        kern,
        grid=(bsz, nt),
        in_specs=[
            pl.BlockSpec((None, tm, d), row),
            pl.BlockSpec((1, d), const),
            pl.BlockSpec(w_main.shape, const),
            pl.BlockSpec(w_f.shape, const),
            pl.BlockSpec((1, LANES), const),
            pl.BlockSpec(sel.shape, const),
        ],
        out_specs=[
            pl.BlockSpec((None, tm, w_ssm), row),
            pl.BlockSpec((None, tm, w_attn), row),
            pl.BlockSpec((None, tm, w_attn), row),
            pl.BlockSpec((None, tm, n_heads), row),
            pl.BlockSpec((None, n_heads, tm, LANES), head),
            pl.BlockSpec((None, n_heads, tm, LANES), head),
            pl.BlockSpec((None, n_heads, tm, LANES), head),
            pl.BlockSpec((None, None, 1, LANES), tile),
            pl.BlockSpec((None, None, 1, LANES), tile),
        ],
        out_shape=[
            jax.ShapeDtypeStruct((bsz, t, w_ssm), F32),
            jax.ShapeDtypeStruct((bsz, t, w_attn), F32),
            jax.ShapeDtypeStruct((bsz, t, w_attn), F32),
            jax.ShapeDtypeStruct((bsz, t, n_heads), F32),
            jax.ShapeDtypeStruct((bsz, n_heads, tp, LANES), BF16),
            jax.ShapeDtypeStruct((bsz, n_heads, tp, LANES), BF16),
            jax.ShapeDtypeStruct((bsz, n_heads, tp, LANES), BF16),
            jax.ShapeDtypeStruct((bsz, nt, 1, LANES), F32),
            jax.ShapeDtypeStruct((bsz, nt, 1, LANES), F32),
        ],
        scratch_shapes=[pltpu.VMEM((8, LANES), F32)],
        compiler_params=_cparams(("arbitrary", "arbitrary")),
        name="in_proj",
    )(x, g, w_main, w_f, b_f, sel)


def _s5_param_kernel(are_ref, aim_ref, ldt_ref, bre_ref, bim_ref, abr_ref, abi_ref, bbr_ref, bbi_ref):
    lam_re = jnp.minimum(are_ref[...], LAMBDA_RE_MAX)
    lam_im = aim_ref[...]
    dt = jnp.exp(ldt_ref[...])
    mag = jnp.exp(lam_re * dt)
    ab_re = mag * jnp.cos(lam_im * dt)
    ab_im = mag * jnp.sin(lam_im * dt)
    abr_ref[...] = ab_re
    abi_ref[...] = ab_im
    nr = ab_re - 1.0
    ni = ab_im
    den = lam_re * lam_re + lam_im * lam_im
    cr = (nr * lam_re + ni * lam_im) / den
    ci = (ni * lam_re - nr * lam_im) / den
    b_re = bre_ref[...]
    b_im = bim_ref[...]
    bbr_ref[...] = cr * b_re - ci * b_im
    bbi_ref[...] = cr * b_im + ci * b_re


def _s5_params(a_re, a_im, log_dt, b_re, b_im):
    g, n = a_re.shape
    p = b_re.shape[-1]
    gn = g * n
    flat = lambda a: a.reshape(1, gn)
    ldt = jnp.broadcast_to(log_dt[:, None], (g, n)).reshape(1, gn)
    bt = lambda b: jnp.transpose(b, (2, 0, 1)).reshape(p, gn)
    return pl.pallas_call(
        _s5_param_kernel,
        out_shape=[jax.ShapeDtypeStruct((1, gn), F32), jax.ShapeDtypeStruct((1, gn), F32),
                   jax.ShapeDtypeStruct((p, gn), F32), jax.ShapeDtypeStruct((p, gn), F32)],
        name="s5_params",
    )(flat(a_re), flat(a_im), ldt, bt(b_re), bt(b_im))


def _s5_block_mats(bb_re, bb_im, c_re, c_im, n_groups):
    p = bb_re.shape[0]
    n = bb_re.shape[1] // n_groups
    s = n_groups // GROUP_SLAB
    eye = jnp.eye(GROUP_SLAB, dtype=F32)

    def in_blk(bb):
        b4 = bb.reshape(p, s, GROUP_SLAB, n)
        return jnp.einsum('qsgn,gh->sgqhn', b4, eye)

    b_mat = jnp.concatenate([in_blk(bb_re).reshape(s, GROUP_SLAB * p, GROUP_SLAB * n),
                             in_blk(bb_im).reshape(s, GROUP_SLAB * p, GROUP_SLAB * n)], axis=-1)

    def out_blk(c):
        c4 = c.reshape(s, GROUP_SLAB, p, n)
        return jnp.einsum('sgpn,gh->sgnhp', c4, eye)

    c_mat = jnp.concatenate([out_blk(c_re).reshape(s, GROUP_SLAB * n, GROUP_SLAB * p),
                             out_blk(-c_im).reshape(s, GROUP_SLAB * n, GROUP_SLAB * p)], axis=1)
    return b_mat.astype(BF16), c_mat.astype(BF16)


def _s5_kernel(u_ref, h0r_ref, h0i_ref, abr_ref, abi_ref, bm_ref, cm_ref, d_ref, wg_ref, bg_ref,
               y_ref, hr_out_ref, hi_out_ref, sre_ref, sim_ref, hre_ref, him_ref,
               *, t_valid, tm, n_slabs, slab_in, slab_state):
    i = pl.program_id(1)
    nt = pl.num_programs(1)

    @pl.when(i == 0)
    def _():
        hre_ref[...] = h0r_ref[...]
        him_ref[...] = h0i_ref[...]

    u = u_ref[...]
    ub = u.astype(BF16)
    for s in range(n_slabs):
        z = jnp.dot(ub[:, s * slab_in:(s + 1) * slab_in], bm_ref[s], preferred_element_type=F32)
        sre_ref[:, s * slab_state:(s + 1) * slab_state] = z[:, :slab_state]
        sim_ref[:, s * slab_state:(s + 1) * slab_state] = z[:, slab_state:]

    a_re = abr_ref[...]
    a_im = abi_ref[...]
    last_row = (t_valid - 1) % tm

    def step(t, carry):
        h_re, h_im = carry
        b_re = sre_ref[pl.ds(t, 1), :]
        b_im = sim_ref[pl.ds(t, 1), :]
        n_re = a_re * h_re - a_im * h_im + b_re
        n_im = a_re * h_im + a_im * h_re + b_im
        sre_ref[pl.ds(t, 1), :] = n_re
        sim_ref[pl.ds(t, 1), :] = n_im
        return n_re, n_im

    h_re, h_im = lax.fori_loop(0, tm, step, (hre_ref[...], him_ref[...]))
    hre_ref[...] = h_re
    him_ref[...] = h_im

    @pl.when(i == nt - 1)
    def _():
        hr_out_ref[...] = sre_ref[last_row:last_row + 1, :]
        hi_out_ref[...] = sim_ref[last_row:last_row + 1, :]

    ys = []
    for s in range(n_slabs):
        sl = slice(s * slab_state, (s + 1) * slab_state)
        cm = cm_ref[s]
        y = jnp.dot(sre_ref[:, sl].astype(BF16), cm[:slab_state], preferred_element_type=F32)
        y = y + jnp.dot(sim_ref[:, sl].astype(BF16), cm[slab_state:], preferred_element_type=F32)
        ys.append(y)
    y = jnp.concatenate(ys, axis=-1) + d_ref[...] * u
    y = jax.nn.gelu(y)
    gate = jnp.dot(y.astype(BF16), wg_ref[...], preferred_element_type=F32) + bg_ref[...]
    y_ref[...] = y * jax.nn.sigmoid(gate)


def _s5_mixer(u, h0_re, h0_im, ab_re, ab_im, b_mat, c_mat, d_skip, w_glu, b_glu, *, tm):
    bsz, t, w = u.shape
    gn = ab_re.shape[-1]
    n_slabs = b_mat.shape[0]
    nt = pl.cdiv(t, tm)
    row = lambda b, i: (b, i, 0)
    st = lambda b, i: (b, 0, 0)
    c2 = lambda b, i: (0, 0)
    c3 = lambda b, i: (0, 0, 0)
    kern = functools.partial(_s5_kernel, t_valid=t, tm=tm, n_slabs=n_slabs,
                             slab_in=w // n_slabs, slab_state=gn // n_slabs)
    return pl.pallas_call(
        kern,
        grid=(bsz, nt),
        in_specs=[
            pl.BlockSpec((None, tm, w), row),
            pl.BlockSpec((None, 1, gn), st),
            pl.BlockSpec((None, 1, gn), st),
            pl.BlockSpec((1, gn), c2),
            pl.BlockSpec((1, gn), c2),
            pl.BlockSpec(b_mat.shape, c3),
            pl.BlockSpec(c_mat.shape, c3),
            pl.BlockSpec((1, w), c2),
            pl.BlockSpec((w, w), c2),
            pl.BlockSpec((1, w), c2),
        ],
        out_specs=[
            pl.BlockSpec((None, tm, w), row),
            pl.BlockSpec((None, 1, gn), st),
            pl.BlockSpec((None, 1, gn), st),
        ],
        out_shape=[
            jax.ShapeDtypeStruct((bsz, t, w), F32),
            jax.ShapeDtypeStruct((bsz, 1, gn), F32),
            jax.ShapeDtypeStruct((bsz, 1, gn), F32),
        ],
        scratch_shapes=[pltpu.VMEM((tm, gn), F32), pltpu.VMEM((tm, gn), F32),
                        pltpu.VMEM((1, gn), F32), pltpu.VMEM((1, gn), F32)],
        compiler_params=_cparams(("arbitrary", "arbitrary")),
        name="s5_mixer",
    )(u, h0_re, h0_im, ab_re, ab_im, b_mat, c_mat, d_skip, w_glu, b_glu)


def _fox_prompt_kernel(kn_ref, bi_ref, qa_ref, qb_ref, ka_ref, kb_ref, va_ref, vb_ref, o_ref,
                       m_ref, acc_ref, qn_ref, g_ref, *, tq, tk, nk, n_heads):
    i = pl.program_id(2)
    heads = ((qa_ref, ka_ref, va_ref), (qb_ref, kb_ref, vb_ref))
    m_ref[...] = jnp.full_like(m_ref, -jnp.inf)
    acc_ref[...] = jnp.zeros_like(acc_ref)
    lane = lax.broadcasted_iota(jnp.int32, (tq, LANES), 1)
    for hh, (q_ref, _, _) in enumerate(heads):
        qf = jnp.where(lane < HEAD_DIM, q_ref[...].astype(F32), 0.0)
        qn_ref[hh] = jnp.sqrt(jnp.sum(qf * qf, axis=-1, keepdims=True)) * NORM_SLACK
    table = (pl.program_id(0) * n_heads + 2 * pl.program_id(1)) * nk

    def reaches(j):
        hit = False
        for hh in range(2):
            kn = kn_ref[table + hh * nk + j]
            bi = bi_ref[table + hh * nk + j]
            slack = jnp.min(m_ref[hh] - qn_ref[hh] * kn)
            hit = jnp.logical_or(hit, slack <= bi + UNDERFLOW_LOG2)
        return hit

    row0 = i * tq
    diag = row0 // tk

    def scores(j, masked):
        start = pl.multiple_of(j * tk, tk)
        out = []
        for q_ref, k_ref, _ in heads:
            g = lax.dot_general(q_ref[...], k_ref[pl.ds(start, tk), :], (((1,), (1,)), ((), ())),
                                preferred_element_type=F32)
            if masked:
                r = row0 + lax.broadcasted_iota(jnp.int32, (tq, tk), 0)
                c = start + lax.broadcasted_iota(jnp.int32, (tq, tk), 1)
                g = jnp.where(c <= r, g, -jnp.inf)
            out.append(g)
        return out

    def consume(j):
        start = pl.multiple_of(j * tk, tk)
        for hh, (_, _, v_ref) in enumerate(heads):
            g = g_ref[hh]
            m_old = m_ref[hh]
            m_new = jnp.maximum(m_old, jnp.max(g, axis=-1, keepdims=True))
            p = jnp.exp2(g - m_new)
            alpha = jnp.exp2(m_old - m_new)
            acc_ref[hh] = alpha * acc_ref[hh] + jnp.dot(p.astype(BF16), v_ref[pl.ds(start, tk), :],
                                                        preferred_element_type=F32)
            m_ref[hh] = m_new

    for hh, g in enumerate(scores(diag, True)):
        g_ref[hh] = g

    def body(c):
        t, _ = c
        cur = diag - t
        nxt = jnp.maximum(cur - 1, 0)
        g_next = scores(nxt, False)
        consume(cur)
        for hh, g in enumerate(g_next):
            g_ref[hh] = g
        return t + 1, jnp.logical_and(cur >= 1, reaches(nxt))

    lax.while_loop(lambda c: c[1], body, (jnp.int32(0), jnp.bool_(True)))
    outs = []
    for hh in range(2):
        acc = acc_ref[hh]
        outs.append(acc[:, :HEAD_DIM] / acc[:, HEAD_DIM:HEAD_DIM + 1])
    o_ref[...] = jnp.concatenate(outs, axis=-1)


def _fox_prompt(qh, kh, vh, kn_sq, bias_max, *, tq):
    bsz, nh, tp, wl = qh.shape
    nq = tp // tq
    nk = kn_sq.shape[1]
    tk = tp // nk
    assert tk % tq == 0
    stat = lambda a: lax.cummax(jnp.transpose(a[:, :, 0, :nh], (0, 2, 1)), axis=2).reshape(-1)
    kn_tab = stat(jnp.sqrt(kn_sq) * NORM_SLACK)
    bi_tab = stat(bias_max)
    qa = lambda b, p, i, kn, bi: (b, 2 * p, i, 0)
    qb = lambda b, p, i, kn, bi: (b, 2 * p + 1, i, 0)
    fa = lambda b, p, i, kn, bi: (b, 2 * p, 0, 0)
    fb = lambda b, p, i, kn, bi: (b, 2 * p + 1, 0, 0)
    q_spec = lambda im: pl.BlockSpec((None, None, tq, wl), im)
    kv_spec = lambda im: pl.BlockSpec((None, None, tp, wl), im)
    grid_spec = pltpu.PrefetchScalarGridSpec(
        num_scalar_prefetch=2,
        grid=(bsz, nh // 2, nq),
        in_specs=[q_spec(qa), q_spec(qb), kv_spec(fa), kv_spec(fb), kv_spec(fa), kv_spec(fb)],
        out_specs=pl.BlockSpec((None, tq, 2 * HEAD_DIM), lambda b, p, i, kn, bi: (b, i, p)),
        scratch_shapes=[pltpu.VMEM((2, tq, 1), F32), pltpu.VMEM((2, tq, wl), F32), pltpu.VMEM((2, tq, 1), F32),
                        pltpu.VMEM((2, tq, tk), F32)],
    )
    return pl.pallas_call(
        functools.partial(_fox_prompt_kernel, tq=tq, tk=tk, nk=nk, n_heads=nh),
        grid_spec=grid_spec,
        out_shape=jax.ShapeDtypeStruct((bsz, tp, nh * HEAD_DIM), F32),
        compiler_params=_cparams(("arbitrary", "arbitrary", "arbitrary")),
        name="fox_prompt",
    )(kn_tab, bi_tab, qh, qh, kh, kh, vh, vh)


def _fox_sample_kernel(q_ref, kn_ref, vn_ref, ck_ref, cv_ref, bc_ref, bn_ref, o_ref, *, n_heads):
    tq = q_ref.shape[1]
    r = lax.broadcasted_iota(jnp.int32, (tq, tq), 0)
    c = lax.broadcasted_iota(jnp.int32, (tq, tq), 1)
    outs = []
    dn = (((1,), (1,)), ((), ()))
    for hh in range(n_heads):
        sl = slice(hh * HEAD_DIM, (hh + 1) * HEAD_DIM)
        q = q_ref[hh][:, :HEAD_DIM]
        kn = kn_ref[hh][:, :HEAD_DIM]
        vn = vn_ref[hh][:, :HEAD_DIM]
        kc = ck_ref[:, sl].astype(BF16)
        vc = cv_ref[:, sl].astype(BF16)
        g_c = lax.dot_general(q, kc, dn, preferred_element_type=F32) + bc_ref[hh]
        g_n = lax.dot_general(q, kn, dn, preferred_element_type=F32) + bn_ref[hh]
        g_n = jnp.where(c <= r, g_n, -jnp.inf)
        m = jnp.maximum(jnp.max(g_c, axis=-1, keepdims=True), jnp.max(g_n, axis=-1, keepdims=True))
        p_c = jnp.exp2(g_c - m)
        p_n = jnp.exp2(g_n - m)
        den = jnp.sum(p_c, axis=-1, keepdims=True) + jnp.sum(p_n, axis=-1, keepdims=True)
        num = jnp.dot(p_c.astype(BF16), vc, preferred_element_type=F32)
        num = num + jnp.dot(p_n.astype(BF16), vn, preferred_element_type=F32)
        outs.append(num / den)
    o_ref[...] = jnp.concatenate(outs, axis=-1)


def _fox_sample(qh, kh, vh, cache_k, cache_v, bias_cache, bias_new):
    bsz, nh, tq, wl = qh.shape
    past = cache_k.shape[1]
    w = nh * HEAD_DIM
    b4 = lambda b: (b, 0, 0, 0)
    b3 = lambda b: (b, 0, 0)
    return pl.pallas_call(
        functools.partial(_fox_sample_kernel, n_heads=nh),
        grid=(bsz,),
        in_specs=[pl.BlockSpec((None, nh, tq, wl), b4), pl.BlockSpec((None, nh, tq, wl), b4),
                  pl.BlockSpec((None, nh, tq, wl), b4),
                  pl.BlockSpec((None, past, w), b3), pl.BlockSpec((None, past, w), b3),
                  pl.BlockSpec((None, nh, 1, past), b4), pl.BlockSpec((None, nh, 1, tq), b4)],
        out_specs=pl.BlockSpec((None, tq, w), b3),
        out_shape=jax.ShapeDtypeStruct((bsz, tq, w), F32),
        compiler_params=_cparams(("arbitrary",)),
        name="fox_sample",
    )(qh, kh, vh, cache_k, cache_v, bias_cache, bias_new)


def _cumsum_kernel(x_ref, o_ref, carry_ref, *, tm):
    i = pl.program_id(1)

    @pl.when(i == 0)
    def _():
        carry_ref[...] = jnp.zeros_like(carry_ref)

    fc = _prefix_sum_rows(x_ref[...]) + carry_ref[0:1, :]
    o_ref[...] = fc
    carry_ref[0:1, :] = fc[tm - 1:tm, :]


def _cumsum_time(x, *, tm):
    bsz, t, w = x.shape
    row = lambda b, i: (b, i, 0)
    return pl.pallas_call(
        functools.partial(_cumsum_kernel, tm=tm),
        grid=(bsz, t // tm),
        in_specs=[pl.BlockSpec((None, tm, w), row)],
        out_specs=pl.BlockSpec((None, tm, w), row),
        out_shape=jax.ShapeDtypeStruct((bsz, t, w), F32),
        scratch_shapes=[pltpu.VMEM((8, w), F32)],
        compiler_params=_cparams(("arbitrary", "arbitrary")),
        name="cumsum_time",
    )(x)


def _merge_router_kernel(x_ref, ys_ref, ya_ref, gs_ref, ga_ref, wo_ref, gf_ref, wr_ref, br_ref, cnt_in_ref,
                         x2_ref, ti_ref, tw_ref, tp_ref, cnt_out_ref, cnt_ref, *, t_valid, tm):
    first = (pl.program_id(0) == 0) & (pl.program_id(1) == 0)

    @pl.when(first)
    def _():
        cnt_ref[...] = jnp.broadcast_to(cnt_in_ref[...], cnt_ref.shape)

    mix = jnp.concatenate([_rms(ys_ref[...], gs_ref[...]), _rms(ya_ref[...], ga_ref[...])], axis=-1)
    x2 = x_ref[...] + jnp.dot(mix.astype(BF16), wo_ref[...], preferred_element_type=F32)
    x2_ref[...] = x2
    h2 = _rms(x2, gf_ref[...])
    h_hi = h2.astype(BF16)
    h_lo = (h2 - h_hi.astype(F32)).astype(BF16)
    logits = (jnp.dot(h_hi, wr_ref[0], preferred_element_type=F32)
              + jnp.dot(h_lo, wr_ref[0], preferred_element_type=F32)
              + jnp.dot(h_hi, wr_ref[1], preferred_element_type=F32)) + br_ref[...]

    lane = lax.broadcasted_iota(jnp.int32, (tm, LANES), 1)
    rows = pl.program_id(1) * tm + lax.broadcasted_iota(jnp.int32, (tm, 1), 0)
    valid = rows < t_valid
    vals = logits
    top_v, top_i, sels = [], [], []
    for _ in range(TOP_K):
        mx = jnp.max(vals, axis=-1, keepdims=True)
        idx = jnp.min(jnp.where(vals == mx, lane, LANES), axis=-1, keepdims=True)
        sel = lane == idx
        vals = jnp.where(sel, -jnp.inf, vals)
        top_v.append(mx)
        top_i.append(idx)
        sels.append(sel)
    ex = [jnp.exp(v - top_v[0]) for v in top_v]
    den = ex[0] + ex[1] + ex[2] + ex[3]

    member = jnp.zeros((tm, LANES), F32)
    for sel in sels:
        member = member + jnp.where(valid, jnp.where(sel, 1.0, 0.0), 0.0)
    r = lax.broadcasted_iota(jnp.int32, (tm, tm), 0)
    c = lax.broadcasted_iota(jnp.int32, (tm, tm), 1)
    tri = (c < r).astype(BF16)
    before = jnp.dot(tri, member.astype(BF16), preferred_element_type=F32) + cnt_ref[0:1, :]
    cnt_new = cnt_ref[0:1, :] + jnp.sum(member, axis=0, keepdims=True)
    cnt_ref[0:1, :] = cnt_new
    cnt_out_ref[...] = cnt_new

    ti = jnp.zeros((tm, LANES), jnp.int32)
    tw = jnp.zeros((tm, LANES), F32)
    tpos = jnp.zeros((tm, LANES), F32)
    for kk in range(TOP_K):
        pos = jnp.sum(jnp.where(sels[kk], before, 0.0), axis=-1, keepdims=True)
        ti = jnp.where(lane == kk, top_i[kk], ti)
        tw = jnp.where(lane == kk, ex[kk] / den, tw)
        tpos = jnp.where(lane == kk, pos, tpos)
    ti_ref[...] = ti
    tw_ref[...] = tw
    tp_ref[...] = tpos.astype(jnp.int32)


def _merge_router(x, ys, ya, g_ssm, g_attn, w_out, g_ffn, w_router, b_router, cnt_in, *, tm):
    bsz, t, d = x.shape
    w = ys.shape[-1]
    nt = pl.cdiv(t, tm)
    row = lambda b, i: (b, i, 0)
    c2 = lambda b, i: (0, 0)
    return pl.pallas_call(
        functools.partial(_merge_router_kernel, t_valid=t, tm=tm),
        grid=(bsz, nt),
        in_specs=[
            pl.BlockSpec((None, tm, d), row),
            pl.BlockSpec((None, tm, w), row),
            pl.BlockSpec((None, tm, w), row),
            pl.BlockSpec((1, w), c2), pl.BlockSpec((1, w), c2),
            pl.BlockSpec(w_out.shape, c2),
            pl.BlockSpec((1, d), c2),
            pl.BlockSpec(w_router.shape, lambda b, i: (0, 0, 0)),
            pl.BlockSpec((1, LANES), c2),
            pl.BlockSpec((1, LANES), c2),
        ],
        out_specs=[
            pl.BlockSpec((None, tm, d), row),
            pl.BlockSpec((None, tm, LANES), row),
            pl.BlockSpec((None, tm, LANES), row),
            pl.BlockSpec((None, tm, LANES), row),
            pl.BlockSpec((1, LANES), c2),
        ],
        out_shape=[
            jax.ShapeDtypeStruct((bsz, t, d), F32),
            jax.ShapeDtypeStruct((bsz, t, LANES), jnp.int32),
            jax.ShapeDtypeStruct((bsz, t, LANES), F32),
            jax.ShapeDtypeStruct((bsz, t, LANES), jnp.int32),
            jax.ShapeDtypeStruct((1, LANES), F32),
        ],
        scratch_shapes=[pltpu.VMEM((8, LANES), F32)],
        compiler_params=_cparams(("arbitrary", "arbitrary")),
        name="merge_router",
    )(x, ys, ya, g_ssm, g_attn, w_out, g_ffn, w_router, b_router, cnt_in)


def _dispatch_kernel(last_tile_ref, nu_ref, slots_ref, xa_ref, xb_ref, g_ref, xs_hbm, hbuf_ref, zero_ref, zsem, sem,
                     *, t_valid, nt_a, n_steps_a, tm, n_experts, n_tiles, tile):
    step = pl.program_id(0)

    def zero_tile(t):
        return pltpu.make_async_copy(zero_ref, xs_hbm.at[pl.ds(pl.multiple_of(t * tile, tile), tile)], zsem)

    @pl.when(step == 0)
    def _():
        zero_ref[...] = jnp.zeros_like(zero_ref)

        def start_tail(t, c):
            zero_tile(t).start()
            return c

        def wait_tail(t, c):
            zero_tile(t).wait()
            return c

        lax.fori_loop(nu_ref[0], n_tiles, start_tail, 0)
        lax.fori_loop(nu_ref[0], n_tiles, wait_tail, 0)
        for e in range(n_experts):
            lt = last_tile_ref[e]

            @pl.when(lt >= 0)
            def _():
                zero_tile(lt).start()
        for e in range(n_experts):
            lt = last_tile_ref[e]

            @pl.when(lt >= 0)
            def _():
                zero_tile(lt).wait()

    in_a = step < n_steps_a

    @pl.when(in_a)
    def _():
        hbuf_ref[...] = _rms(xa_ref[...], g_ref[...])

    @pl.when(jnp.logical_not(in_a))
    def _():
        hbuf_ref[...] = _rms(xb_ref[...], g_ref[...])

    def row_copy(r, kk):
        return pltpu.make_async_copy(hbuf_ref.at[pl.ds(r, 1)], xs_hbm.at[pl.ds(slots_ref[r * TOP_K + kk], 1)], sem)

    def issue(r, c):
        for kk in range(TOP_K):
            row_copy(r, kk).start()
        return c

    def drain(r, c):
        for kk in range(TOP_K):
            row_copy(r, kk).wait()
        return c

    n_valid = jnp.where(in_a, jnp.minimum(tm, t_valid - (step % nt_a) * tm), tm)

    @pl.when(n_valid == tm)
    def _():
        lax.fori_loop(0, tm, issue, 0, unroll=ISSUE_UNROLL)
        for _ in range(TOP_K):
            pltpu.make_async_copy(hbuf_ref, xs_hbm.at[pl.ds(0, tm)], sem).wait()

    @pl.when(n_valid < tm)
    def _():
        lax.fori_loop(0, n_valid, issue, 0)
        lax.fori_loop(0, n_valid, drain, 0)


def _dispatch(xa, xb, g_ffn, slots_flat, last_tile, n_used, *, n_slots, tm):
    bsz, t, d = xa.shape
    nt_a = pl.cdiv(t, tm)
    n_steps_a = bsz * nt_a
    n_steps_b = xb.shape[0] // tm
    n_experts = last_tile.shape[0]
    grid_spec = pltpu.PrefetchScalarGridSpec(
        num_scalar_prefetch=2,
        grid=(n_steps_a + n_steps_b,),
        in_specs=[
            pl.BlockSpec((tm * TOP_K,), lambda s, lt, nu: (s,), memory_space=pltpu.SMEM),
            pl.BlockSpec((None, tm, d), lambda s, lt, nu: (jnp.minimum(s, n_steps_a - 1) // nt_a,
                                                          jnp.minimum(s, n_steps_a - 1) % nt_a, 0)),
            pl.BlockSpec((tm, d), lambda s, lt, nu: (jnp.maximum(s - n_steps_a, 0), 0)),
            pl.BlockSpec((1, d), lambda s, lt, nu: (0, 0)),
        ],
        out_specs=pl.BlockSpec(memory_space=pl.ANY),
        scratch_shapes=[pltpu.VMEM((tm, d), F32), pltpu.VMEM((EXPERT_TILE, d), F32),
                        pltpu.SemaphoreType.DMA, pltpu.SemaphoreType.DMA],
    )
    return pl.pallas_call(
        functools.partial(_dispatch_kernel, t_valid=t, nt_a=nt_a, n_steps_a=n_steps_a, tm=tm, n_experts=n_experts,
                          n_tiles=n_slots // EXPERT_TILE, tile=EXPERT_TILE),
        grid_spec=grid_spec,
        out_shape=jax.ShapeDtypeStruct((n_slots, d), F32),
        compiler_params=_cparams(("arbitrary",)),
        name="moe_dispatch",
    )(last_tile, n_used, slots_flat, xa, xb, g_ffn)


def _expert_kernel(te_ref, nu_ref, tpe_ref, x_ref, b1_ref, b2_ref, w1_hbm, w2_hbm, y_ref,
                   w1f_ref, w2f_ref, w1b_ref, w2b_ref, grp_ref, sem1, sem2, *, d_ff):
    i = pl.program_id(0)
    used = nu_ref[0]
    e = te_ref[i]
    prev = te_ref[jnp.maximum(i - 1, 0)]
    changed = (i == 0) | (e != prev)

    def fetch(expert, slot):
        return (pltpu.make_async_copy(w1_hbm.at[expert], w1f_ref.at[slot], sem1.at[slot]),
                pltpu.make_async_copy(w2_hbm.at[expert], w2f_ref.at[slot], sem2.at[slot]))

    @pl.when(i == 0)
    def _():
        grp_ref[0] = 0
        for cp in fetch(e, 0):
            cp.start()

    @pl.when((i < used) & changed)
    def _():
        slot = grp_ref[0] % 2
        for cp in fetch(e, slot):
            cp.wait()
        w1b_ref[...] = w1f_ref[slot].astype(BF16)
        w2b_ref[...] = w2f_ref[slot].astype(BF16)
        nxt = i + tpe_ref[e]

        @pl.when(nxt < used)
        def _():
            for cp in fetch(te_ref[nxt], 1 - slot):
                cp.start()
        grp_ref[0] = grp_ref[0] + 1

    @pl.when(i >= used)
    def _():
        y_ref[...] = jnp.zeros_like(y_ref)

    @pl.when(i < used)
    def _():
        x = x_ref[...].astype(BF16)
        h = jnp.dot(x, w1b_ref[...], preferred_element_type=F32) + b1_ref[...]
        hg = jnp.minimum(h[:, :d_ff], SWIGLU_LIMIT)
        hl = jnp.clip(h[:, d_ff:], -SWIGLU_LIMIT, SWIGLU_LIMIT)
        act = (hl + 1.0) * (hg * jax.nn.sigmoid(SWIGLU_ALPHA * hg))
        y_ref[...] = jnp.dot(act.astype(BF16), w2b_ref[...], preferred_element_type=F32) + b2_ref[...]


def _experts(xs, tile_expert, n_used, tiles_per, w1, b1, w2, b2):
    s, d = xs.shape
    n_tiles = s // EXPERT_TILE
    n_e, _, two_ff = w1.shape
    d_ff = two_ff // 2

    def xmap(i, te, nu, tpe):
        return (i, 0)

    def bmap(i, te, nu, tpe):
        return (te[i], 0, 0)

    grid_spec = pltpu.PrefetchScalarGridSpec(
        num_scalar_prefetch=3,
        grid=(n_tiles,),
        in_specs=[
            pl.BlockSpec((EXPERT_TILE, d), xmap),
            pl.BlockSpec((None, 1, two_ff), bmap),
            pl.BlockSpec((None, 1, d), bmap),
            pl.BlockSpec(memory_space=pl.ANY),
            pl.BlockSpec(memory_space=pl.ANY),
        ],
        out_specs=pl.BlockSpec((EXPERT_TILE, d), xmap),
        scratch_shapes=[pltpu.VMEM((2, d, two_ff), F32), pltpu.VMEM((2, d_ff, d), F32),
                        pltpu.VMEM((d, two_ff), BF16), pltpu.VMEM((d_ff, d), BF16),
                        pltpu.SMEM((1,), jnp.int32),
                        pltpu.SemaphoreType.DMA((2,)), pltpu.SemaphoreType.DMA((2,))],
    )
    return pl.pallas_call(
        functools.partial(_expert_kernel, d_ff=d_ff),
        grid_spec=grid_spec,
        out_shape=jax.ShapeDtypeStruct((s, d), F32),
        compiler_params=_cparams(("arbitrary",)),
        name="moe_experts",
    )(tile_expert, n_used, tiles_per, xs, b1.reshape(n_e, 1, two_ff), b2.reshape(n_e, 1, d), w1, w2)


def _combine_kernel(slots_ref, g_ref, x2_hbm, tw_hbm, ys_hbm, o_ref, xbuf_ref, wbuf_ref, buf_ref, sem, lsem,
                    *, tm, row_offset):
    b = pl.program_id(0)
    start = pl.multiple_of(row_offset + pl.program_id(1) * tm, 8)
    x_copy = pltpu.make_async_copy(x2_hbm.at[b, pl.ds(start, tm)], xbuf_ref, lsem)
    w_copy = pltpu.make_async_copy(tw_hbm.at[b, pl.ds(start, tm)], wbuf_ref, lsem)
    x_copy.start()
    w_copy.start()

    def issue(r, c):
        for kk in range(TOP_K):
            pltpu.make_async_copy(ys_hbm.at[pl.ds(slots_ref[r * TOP_K + kk], 1)],
                                  buf_ref.at[kk, pl.ds(r, 1)], sem).start()
        return c

    lax.fori_loop(0, tm, issue, 0, unroll=ISSUE_UNROLL)
    x_copy.wait()
    w_copy.wait()
    for kk in range(TOP_K):
        pltpu.make_async_copy(ys_hbm.at[pl.ds(0, tm)], buf_ref.at[kk], sem).wait()
    tw = wbuf_ref[...]
    acc = xbuf_ref[...]
    for kk in range(TOP_K):
        acc = acc + tw[:, kk:kk + 1] * buf_ref[kk]
    o_ref[...] = _rms(acc, g_ref[...])


def _combine(x2, tw, slots_flat, ys, g_final, *, tm, row_offset):
    bsz, t, d = x2.shape
    t_out = t - row_offset
    nt = t_out // tm
    grid_spec = pltpu.PrefetchScalarGridSpec(
        num_scalar_prefetch=0,
        grid=(bsz, nt),
        in_specs=[
            pl.BlockSpec((tm * TOP_K,), lambda b, i: (b * nt + i,), memory_space=pltpu.SMEM),
            pl.BlockSpec((1, d), lambda b, i: (0, 0)),
            pl.BlockSpec(memory_space=pl.ANY),
            pl.BlockSpec(memory_space=pl.ANY),
            pl.BlockSpec(memory_space=pl.ANY),
        ],
        out_specs=pl.BlockSpec((None, tm, d), lambda b, i: (b, i, 0)),
        scratch_shapes=[pltpu.VMEM((tm, d), F32), pltpu.VMEM((tm, LANES), F32),
                        pltpu.VMEM((TOP_K, tm, d), F32), pltpu.SemaphoreType.DMA, pltpu.SemaphoreType.DMA],
    )
    return pl.pallas_call(
        functools.partial(_combine_kernel, tm=tm, row_offset=row_offset),
        grid_spec=grid_spec,
        out_shape=jax.ShapeDtypeStruct((bsz, t_out, d), F32),
        compiler_params=_cparams(("arbitrary", "arbitrary")),
        name="moe_combine",
    )(slots_flat, g_final, x2, tw, ys)


def _pad_lanes(a, value=0.0):
    return jnp.pad(a, [(0, 0)] * (a.ndim - 1) + [(0, LANES - a.shape[-1])], constant_values=value)


def kernel(x_prompt, x_sample, cache_k, cache_v, cache_logf, state_ssm_re, state_ssm_im, meta_tokens, norm_mix_g, w_in, b_forget, ssm_a_re, ssm_a_im, ssm_log_dt, ssm_b_re, ssm_b_im, ssm_c_re, ssm_c_im, ssm_d, w_glu, b_glu, g_out_ssm, g_out_attn, w_out, norm_ffn_g, w_router, b_router, w_mlp1, b_mlp1, w_mlp2, b_mlp2, norm_final_g):
    depth = w_in.shape[0]
    assert depth == 1, "the routing tables below are built for a single trunk layer"
    l = 0
    bp, seq, d = x_prompt.shape
    bs, ts, _ = x_sample.shape
    n_heads = b_forget.shape[1]
    n_groups, ssm_n = ssm_a_re.shape[1:]
    n_experts = w_router.shape[2]
    w_ssm = n_groups * SSM_P
    w_attn = n_heads * HEAD_DIM
    gn = n_groups * ssm_n
    past = cache_k.shape[2]
    assert seq % ROW_TILE == 0 and (bs * ts) % ROW_TILE == 0

    meta = jnp.broadcast_to(meta_tokens.astype(x_prompt.dtype)[None], (bp, N_META, d))
    xp = jnp.concatenate([meta, x_prompt], axis=1)
    xs = x_sample
    tp = xp.shape[1]

    w_main = w_in[l][:, :w_ssm + 3 * w_attn].astype(BF16)
    w_f = _pad_lanes(w_in[l][:, w_ssm + 3 * w_attn:]).astype(BF16)
    b_f = _pad_lanes(b_forget[l][None, :])
    g_mix = norm_mix_g[l][None, :]
    ab_re, ab_im, bb_re, bb_im = _s5_params(ssm_a_re[l], ssm_a_im[l], ssm_log_dt[l], ssm_b_re[l], ssm_b_im[l])
    b_mat, c_mat = _s5_block_mats(bb_re, bb_im, ssm_c_re[l], ssm_c_im[l], n_groups)

    proj = functools.partial(_in_proj, n_heads=n_heads, w_ssm=w_ssm, w_attn=w_attn)
    up, kp, vp, lfp, qhp, khp, vhp, knp, bip = proj(xp, g_mix, w_main, w_f, b_f, tm=ROW_TILE)
    us, ks, vs, lfs, qhs, khs, vhs, _, _ = proj(xs, g_mix, w_main, w_f, b_f, tm=ts)

    s5 = functools.partial(_s5_mixer, ab_re=ab_re, ab_im=ab_im, b_mat=b_mat, c_mat=c_mat,
                           d_skip=ssm_d[l].reshape(1, w_ssm), w_glu=w_glu[l].astype(BF16), b_glu=b_glu[l][None, :])
    zeros_state = jnp.zeros((bp, 1, gn), F32)
    ysp, hrp, hip = s5(up, zeros_state, zeros_state, tm=ROW_TILE)
    yss, hrs, his = s5(us, state_ssm_re[l].reshape(bs, 1, gn), state_ssm_im[l].reshape(bs, 1, gn), tm=ts)

    yap = _fox_prompt(qhp, khp, vhp, knp, bip, tq=ATTN_Q_TILE)
    t_all = past + ts
    assert bs * n_heads == LANES
    tm_cum = max(m for m in range(8, ROW_TILE + 1, 8) if t_all % m == 0)
    lf_all = jnp.concatenate([cache_logf[l].astype(F32), lfs], axis=1)
    fcs = _cumsum_time(jnp.transpose(lf_all, (1, 0, 2)).reshape(1, t_all, LANES), tm=tm_cum)
    bias_s = -LOG2E * jnp.transpose(fcs.reshape(t_all, bs, n_heads), (1, 2, 0))[:, :, None, :]
    yas = _fox_sample(qhs, khs, vhs, cache_k[l].reshape(bs, past, w_attn), cache_v[l].reshape(bs, past, w_attn),
                      bias_s[..., :past], bias_s[..., past:])

    wr = _pad_lanes(w_router[l])
    wr_hi = wr.astype(BF16)
    wr_parts = jnp.stack([wr_hi, (wr - wr_hi.astype(F32)).astype(BF16)])
    merge = functools.partial(_merge_router, g_ssm=g_out_ssm[l][None, :], g_attn=g_out_attn[l][None, :],
                              w_out=w_out[l].astype(BF16), g_ffn=norm_ffn_g[l][None, :],
                              w_router=wr_parts, b_router=_pad_lanes(b_router[l][None, :], value=-1e30))
    x2p, tip, twp, tpp, cnt_p = merge(xp, ysp, yap, cnt_in=jnp.zeros((1, LANES), F32), tm=ROW_TILE)
    x2s, tis, tws, tps, cnt = merge(xs, yss, yas, cnt_in=cnt_p, tm=ts)

    n_tok = bp * tp + bs * ts
    counts = cnt[0, :n_experts].astype(jnp.int32)
    tiles_per = (counts + EXPERT_TILE - 1) // EXPERT_TILE
    tile_end = jnp.cumsum(tiles_per)
    tile_start = tile_end - tiles_per
    n_tiles = (n_tok * TOP_K) // EXPERT_TILE + n_experts
    n_slots = n_tiles * EXPERT_TILE
    n_used = tile_end[-1:].astype(jnp.int32)
    tile_ids = jnp.minimum(jnp.arange(n_tiles, dtype=jnp.int32), n_used - 1)
    tile_expert = jnp.sum((tile_end[None, :] <= tile_ids[:, None]).astype(jnp.int32), axis=1)
    last_tile = jnp.where(tiles_per > 0, tile_end - 1, -1).astype(jnp.int32)
    slot_base = (tile_start * EXPERT_TILE).astype(F32)

    def slots_of(ids, pos):
        onehot = jax.nn.one_hot(ids[..., :TOP_K], n_experts, dtype=F32)
        base = jnp.einsum('btke,e->btk', onehot, slot_base, precision=lax.Precision.HIGHEST)
        return base.astype(jnp.int32) + pos[..., :TOP_K]

    slots_p = slots_of(tip, tpp)
    slots_s = slots_of(tis, tps)
    tpad = pl.cdiv(tp, ROW_TILE) * ROW_TILE
    slots_disp = jnp.concatenate([jnp.pad(slots_p, ((0, 0), (0, tpad - tp), (0, 0))).reshape(-1),
                                  slots_s.reshape(-1)])
    xs_sorted = _dispatch(x2p, x2s.reshape(bs * ts, d), norm_ffn_g[l][None, :], slots_disp, last_tile, n_used,
                          n_slots=n_slots, tm=ROW_TILE)
    ys_sorted = _experts(xs_sorted, tile_expert, n_used, tiles_per.astype(jnp.int32),
                         w_mlp1[l], b_mlp1[l], w_mlp2[l], b_mlp2[l])

    g_fin = norm_final_g[None, :]
    y_prompt = _combine(x2p, twp, slots_p[:, N_META:].reshape(-1), ys_sorted, g_fin, tm=EXPERT_TILE, row_offset=N_META)
    y_sample = _combine(x2s.reshape(1, bs * ts, d), tws.reshape(1, bs * ts, LANES), slots_s.reshape(-1), ys_sorted,
                        g_fin, tm=EXPERT_TILE, row_offset=0).reshape(bs, ts, d)

    hd = HEAD_DIM
    st = lambda a, b: a.reshape(1, b, n_groups, ssm_n)
    return (y_prompt, y_sample,
            kp.reshape(1, bp, tp, n_heads, hd), vp.reshape(1, bp, tp, n_heads, hd), lfp[None],
            st(hrp, bp), st(hip, bp),
            ks.reshape(1, bs, ts, n_heads, hd), vs.reshape(1, bs, ts, n_heads, hd), lfs[None],
            st(hrs, bs), st(his, bs))
```

```python
import functools
import math

import numpy as np
import jax
import jax.numpy as jnp
from jax import lax
from jax.experimental import pallas as pl
from jax.experimental.pallas import tpu as pltpu

F32 = jnp.float32
BF16 = jnp.bfloat16

LANES = 128
VMEM_LIMIT_BYTES = 56 * 1024 * 1024

N_META = 16
HEAD_DIM = 64
SSM_P = 16
SSM_N = 64
TOP_K = 4
SWIGLU_LIMIT = 7.0
SWIGLU_ALPHA = 1.702
RMS_EPS = 1e-6
LAMBDA_RE_MAX = -1e-4
LOG2E = math.log2(math.e)

ROW_TILE = 512
ATTN_Q_TILE = 512
EXPERT_TILE = 256
GROUP_SLAB = 8
BIAS_TERMS = 3
ISSUE_UNROLL = 8
UNDERFLOW_LOG2 = 160.0
NORM_SLACK = 1.001


def _cparams(sem):
    return pltpu.CompilerParams(dimension_semantics=sem, vmem_limit_bytes=VMEM_LIMIT_BYTES)


def _rms(v, g):
    return v * lax.rsqrt(jnp.mean(v * v, axis=-1, keepdims=True) + RMS_EPS) * g


def _timeline_rows(x_refs, i, n_meta):
    if not n_meta:
        return x_refs[0][...]
    meta_ref, prev_ref, cur_ref = x_refs
    head = jnp.where(i == 0, meta_ref[...], prev_ref[...])
    return jnp.concatenate([head, cur_ref[...][:cur_ref.shape[0] - n_meta]], axis=0)


def _timeline_specs(x, meta, tm):
    bsz, t_in, d = x.shape
    if meta is None:
        return [x], [pl.BlockSpec((None, tm, d), lambda b, i: (b, i, 0))], t_in
    n_meta = meta.shape[0]
    assert t_in % tm == 0 and tm % n_meta == 0
    per = tm // n_meta
    last = t_in // tm - 1
    specs = [pl.BlockSpec((n_meta, d), lambda b, i: (0, 0)),
             pl.BlockSpec((None, n_meta, d), lambda b, i: (b, jnp.maximum(i * per - 1, 0), 0)),
             pl.BlockSpec((None, tm, d), lambda b, i: (b, jnp.minimum(i, last), 0))]
    return [meta, x, x], specs, t_in + n_meta


def _split_bf16(a):
    pieces = []
    rest = a
    for _ in range(BIAS_TERMS):
        piece = rest.astype(BF16)
        pieces.append(piece)
        rest = rest - piece.astype(F32)
    return jnp.concatenate(pieces, axis=-1)


def _prefix_sum_rows(x):
    tm = x.shape[0]
    r = lax.broadcasted_iota(jnp.int32, (tm, tm), 0)
    c = lax.broadcasted_iota(jnp.int32, (tm, tm), 1)
    tri = jnp.where(c <= r, 1.0, 0.0).astype(BF16)
    s = jnp.dot(tri, _split_bf16(x), preferred_element_type=F32)
    return sum(s[:, t * LANES:(t + 1) * LANES] for t in range(BIAS_TERMS))


def _in_proj_kernel(*refs, t_valid, tm, n_heads, w_ssm, w_attn, n_meta):
    n_x = 3 if n_meta else 1
    (g_ref, w_ref, wf_ref, bf_ref, sel_ref,
     u_ref, k_ref, v_ref, lf_ref, qh_ref, kh_ref, vh_ref, kn_ref, bi_ref, carry_ref) = refs[n_x:]
    i = pl.program_id(1)

    @pl.when(i == 0)
    def _():
        carry_ref[...] = jnp.zeros_like(carry_ref)

    x = _timeline_rows(refs[:n_x], i, n_meta)
    h = _rms(x, g_ref[...]).astype(BF16)
    rows = i * tm + lax.broadcasted_iota(jnp.int32, (tm, 1), 0)
    valid = rows < t_valid
    z = jnp.where(valid, jnp.dot(h, w_ref[...], preferred_element_type=F32), 0.0)
    u_ref[...] = z[:, :w_ssm]
    k_ref[...] = z[:, w_ssm + w_attn:w_ssm + 2 * w_attn]
    v_ref[...] = z[:, w_ssm + 2 * w_attn:w_ssm + 3 * w_attn]

    zf = jnp.dot(h, wf_ref[...], preferred_element_type=F32) + bf_ref[...]
    lf = jnp.where(valid, jnp.minimum(zf, 0.0) - jnp.log1p(jnp.exp(-jnp.abs(zf))), 0.0)
    lf_ref[...] = lf[:, :n_heads]
    fc = _prefix_sum_rows(lf) + carry_ref[0:1, :]
    carry_ref[0:1, :] = fc[tm - 1:tm, :]

    bias = -LOG2E * fc
    bi_ref[...] = jnp.max(bias, axis=0, keepdims=True)
    tails = jnp.dot(_split_bf16(bias), sel_ref[...], preferred_element_type=F32)

    lane = lax.broadcasted_iota(jnp.int32, (tm, LANES), 1)
    lane1 = lax.broadcasted_iota(jnp.int32, (1, LANES), 1)
    low = lane < HEAD_DIM
    in_tail = (lane >= HEAD_DIM) & (lane < HEAD_DIM + BIAS_TERMS)
    q_tail = jnp.where(in_tail, 1.0, 0.0)
    v_tail = jnp.where(lane == HEAD_DIM, 1.0, 0.0)
    kn = jnp.zeros((1, LANES), F32)
    for pair in range(n_heads // 2):
        sl = slice(pair * LANES, (pair + 1) * LANES)
        qt = z[:, w_ssm:w_ssm + w_attn][:, sl] * (LOG2E * HEAD_DIM ** -0.5)
        kt = z[:, w_ssm + w_attn:w_ssm + 2 * w_attn][:, sl]
        vt = z[:, w_ssm + 2 * w_attn:w_ssm + 3 * w_attn][:, sl]
        for half in range(2):
            hh = 2 * pair + half
            if half:
                qt, kt, vt = (pltpu.roll(a, HEAD_DIM, 1) for a in (qt, kt, vt))
            k_tail = jnp.where(in_tail, pltpu.roll(tails, HEAD_DIM - BIAS_TERMS * hh, 1), 0.0)
            kb = jnp.where(low, kt, k_tail).astype(BF16)
            qh_ref[hh] = jnp.where(low, qt, q_tail).astype(BF16)
            kh_ref[hh] = kb
            vh_ref[hh] = jnp.where(low, vt, v_tail).astype(BF16)
            kf = jnp.where(low, kb.astype(F32), 0.0)
            ksq = jnp.max(jnp.sum(kf * kf, axis=-1, keepdims=True), axis=0, keepdims=True)
            kn = jnp.where(lane1 == hh, ksq, kn)
    kn_ref[...] = kn


def _bias_selector(n_heads):
    sel = np.zeros((BIAS_TERMS * LANES, LANES), np.float32)
    for hh in range(n_heads):
        for c in range(BIAS_TERMS):
            sel[c * LANES + hh, BIAS_TERMS * hh + c] = 1.0
    return jnp.asarray(sel, BF16)


def _in_proj(x, g, w_main, w_f, b_f, *, n_heads, w_ssm, w_attn, tm, meta=None):
    bsz, _, d = x.shape
    x_ops, x_specs, t = _timeline_specs(x, meta, tm)
    nt = pl.cdiv(t, tm)
    tp = nt * tm
    row = lambda b, i: (b, i, 0)
    const = lambda b, i: (0, 0)
    head = lambda b, i: (b, 0, i, 0)
    tile = lambda b, i: (b, i, 0, 0)
    sel = _bias_selector(n_heads)
    kern = functools.partial(_in_proj_kernel, t_valid=t, tm=tm, n_heads=n_heads, w_ssm=w_ssm, w_attn=w_attn,
                             n_meta=0 if meta is None else meta.shape[0])
    return pl.pallas_call(
        kern,
        grid=(bsz, nt),
        in_specs=x_specs + [
            pl.BlockSpec((1, d), const),
            pl.BlockSpec(w_main.shape, const),
            pl.BlockSpec(w_f.shape, const),
            pl.BlockSpec((1, LANES), const),
            pl.BlockSpec(sel.shape, const),
        ],
        out_specs=[
            pl.BlockSpec((None, tm, w_ssm), row),
            pl.BlockSpec((None, tm, w_attn), row),
            pl.BlockSpec((None, tm, w_attn), row),
            pl.BlockSpec((None, tm, n_heads), row),
            pl.BlockSpec((None, n_heads, tm, LANES), head),
            pl.BlockSpec((None, n_heads, tm, LANES), head),
            pl.BlockSpec((None, n_heads, tm, LANES), head),
            pl.BlockSpec((None, None, 1, LANES), tile),
            pl.BlockSpec((None, None, 1, LANES), tile),
        ],
        out_shape=[
            jax.ShapeDtypeStruct((bsz, t, w_ssm), F32),
            jax.ShapeDtypeStruct((bsz, t, w_attn), F32),
            jax.ShapeDtypeStruct((bsz, t, w_attn), F32),
            jax.ShapeDtypeStruct((bsz, t, n_heads), F32),
            jax.ShapeDtypeStruct((bsz, n_heads, tp, LANES), BF16),
            jax.ShapeDtypeStruct((bsz, n_heads, tp, LANES), BF16),
            jax.ShapeDtypeStruct((bsz, n_heads, tp, LANES), BF16),
            jax.ShapeDtypeStruct((bsz, nt, 1, LANES), F32),
            jax.ShapeDtypeStruct((bsz, nt, 1, LANES), F32),
        ],
        scratch_shapes=[pltpu.VMEM((8, LANES), F32)],
        compiler_params=_cparams(("arbitrary", "arbitrary")),
        name="in_proj",
    )(*x_ops, g, w_main, w_f, b_f, sel)


def _s5_param_kernel(are_ref, aim_ref, ldt_ref, bre_ref, bim_ref, abr_ref, abi_ref, bbr_ref, bbi_ref):
    lam_re = jnp.minimum(are_ref[...], LAMBDA_RE_MAX)
    lam_im = aim_ref[...]
    dt = jnp.exp(ldt_ref[...])
    mag = jnp.exp(lam_re * dt)
    ab_re = mag * jnp.cos(lam_im * dt)
    ab_im = mag * jnp.sin(lam_im * dt)
    abr_ref[...] = ab_re
    abi_ref[...] = ab_im
    nr = ab_re - 1.0
    ni = ab_im
    den = lam_re * lam_re + lam_im * lam_im
    cr = (nr * lam_re + ni * lam_im) / den
    ci = (ni * lam_re - nr * lam_im) / den
    b_re = bre_ref[...]
    b_im = bim_ref[...]
    bbr_ref[...] = cr * b_re - ci * b_im
    bbi_ref[...] = cr * b_im + ci * b_re


def _s5_params(a_re, a_im, log_dt, b_re, b_im):
    g, n = a_re.shape
    p = b_re.shape[-1]
    gn = g * n
    flat = lambda a: a.reshape(1, gn)
    ldt = jnp.broadcast_to(log_dt[:, None], (g, n)).reshape(1, gn)
    bt = lambda b: jnp.transpose(b, (2, 0, 1)).reshape(p, gn)
    return pl.pallas_call(
        _s5_param_kernel,
        out_shape=[jax.ShapeDtypeStruct((1, gn), F32), jax.ShapeDtypeStruct((1, gn), F32),
                   jax.ShapeDtypeStruct((p, gn), F32), jax.ShapeDtypeStruct((p, gn), F32)],
        name="s5_params",
    )(flat(a_re), flat(a_im), ldt, bt(b_re), bt(b_im))


def _s5_block_mats(bb_re, bb_im, c_re, c_im, n_groups):
    p = bb_re.shape[0]
    n = bb_re.shape[1] // n_groups
    s = n_groups // GROUP_SLAB
    eye = jnp.eye(GROUP_SLAB, dtype=F32)

    def in_blk(bb):
        b4 = bb.reshape(p, s, GROUP_SLAB, n)
        return jnp.einsum('qsgn,gh->sgqhn', b4, eye)

    b_mat = jnp.concatenate([in_blk(bb_re).reshape(s, GROUP_SLAB * p, GROUP_SLAB * n),
                             in_blk(bb_im).reshape(s, GROUP_SLAB * p, GROUP_SLAB * n)], axis=-1)

    def out_blk(c):
        c4 = c.reshape(s, GROUP_SLAB, p, n)
        return jnp.einsum('sgpn,gh->sgnhp', c4, eye)

    c_mat = jnp.concatenate([out_blk(c_re).reshape(s, GROUP_SLAB * n, GROUP_SLAB * p),
                             out_blk(-c_im).reshape(s, GROUP_SLAB * n, GROUP_SLAB * p)], axis=1)
    return b_mat.astype(BF16), c_mat.astype(BF16)


def _s5_kernel(u_ref, h0r_ref, h0i_ref, abr_ref, abi_ref, bm_ref, cm_ref, d_ref, wg_ref, bg_ref,
               y_ref, hr_out_ref, hi_out_ref, sre_ref, sim_ref, hre_ref, him_ref,
               *, t_valid, tm, nb, n_slabs, slab_in, slab_state):
    i = pl.program_id(1)
    nt = pl.num_programs(1)

    @pl.when(i == 0)
    def _():
        hre_ref[...] = h0r_ref[...]
        him_ref[...] = h0i_ref[...]

    for b in range(nb):
        ub = u_ref[b].astype(BF16)
        for s in range(n_slabs):
            z = jnp.dot(ub[:, s * slab_in:(s + 1) * slab_in], bm_ref[s], preferred_element_type=F32)
            sre_ref[b, :, s * slab_state:(s + 1) * slab_state] = z[:, :slab_state]
            sim_ref[b, :, s * slab_state:(s + 1) * slab_state] = z[:, slab_state:]

    a_re = abr_ref[...]
    a_im = abi_ref[...]
    last_row = (t_valid - 1) % tm

    def step(t, carry):
        out = []
        for b in range(nb):
            h_re, h_im = carry[2 * b], carry[2 * b + 1]
            b_re = sre_ref[b, pl.ds(t, 1), :]
            b_im = sim_ref[b, pl.ds(t, 1), :]
            n_re = a_re * h_re - a_im * h_im + b_re
            n_im = a_re * h_im + a_im * h_re + b_im
            sre_ref[b, pl.ds(t, 1), :] = n_re
            sim_ref[b, pl.ds(t, 1), :] = n_im
            out += [n_re, n_im]
        return tuple(out)

    init = tuple(ref[b] for b in range(nb) for ref in (hre_ref, him_ref))
    fin = lax.fori_loop(0, tm, step, init)
    for b in range(nb):
        hre_ref[b] = fin[2 * b]
        him_ref[b] = fin[2 * b + 1]

    @pl.when(i == nt - 1)
    def _():
        for b in range(nb):
            hr_out_ref[b] = sre_ref[b, last_row:last_row + 1, :]
            hi_out_ref[b] = sim_ref[b, last_row:last_row + 1, :]

    for b in range(nb):
        ys = []
        for s in range(n_slabs):
            sl = slice(s * slab_state, (s + 1) * slab_state)
            cm = cm_ref[s]
            y = jnp.dot(sre_ref[b, :, sl].astype(BF16), cm[:slab_state], preferred_element_type=F32)
            y = y + jnp.dot(sim_ref[b, :, sl].astype(BF16), cm[slab_state:], preferred_element_type=F32)
            ys.append(y)
        y = jnp.concatenate(ys, axis=-1) + d_ref[...] * u_ref[b]
        y = jax.nn.gelu(y)
        gate = jnp.dot(y.astype(BF16), wg_ref[...], preferred_element_type=F32) + bg_ref[...]
        y_ref[b] = y * jax.nn.sigmoid(gate)


def _s5_mixer(u, h0_re, h0_im, ab_re, ab_im, b_mat, c_mat, d_skip, w_glu, b_glu, *, tm, nb):
    bsz, t, w = u.shape
    assert bsz % nb == 0
    gn = ab_re.shape[-1]
    n_slabs = b_mat.shape[0]
    nt = pl.cdiv(t, tm)
    row = lambda b, i: (b, i, 0)
    st = lambda b, i: (b, 0, 0)
    c2 = lambda b, i: (0, 0)
    c3 = lambda b, i: (0, 0, 0)
    kern = functools.partial(_s5_kernel, t_valid=t, tm=tm, nb=nb, n_slabs=n_slabs,
                             slab_in=w // n_slabs, slab_state=gn // n_slabs)
    return pl.pallas_call(
        kern,
        grid=(bsz // nb, nt),
        in_specs=[
            pl.BlockSpec((nb, tm, w), row),
            pl.BlockSpec((nb, 1, gn), st),
            pl.BlockSpec((nb, 1, gn), st),
            pl.BlockSpec((1, gn), c2),
            pl.BlockSpec((1, gn), c2),
            pl.BlockSpec(b_mat.shape, c3),
            pl.BlockSpec(c_mat.shape, c3),
            pl.BlockSpec((1, w), c2),
            pl.BlockSpec((w, w), c2),
            pl.BlockSpec((1, w), c2),
        ],
        out_specs=[
            pl.BlockSpec((nb, tm, w), row),
            pl.BlockSpec((nb, 1, gn), st),
            pl.BlockSpec((nb, 1, gn), st),
        ],
        out_shape=[
            jax.ShapeDtypeStruct((bsz, t, w), F32),
            jax.ShapeDtypeStruct((bsz, 1, gn), F32),
            jax.ShapeDtypeStruct((bsz, 1, gn), F32),
        ],
        scratch_shapes=[pltpu.VMEM((nb, tm, gn), F32), pltpu.VMEM((nb, tm, gn), F32),
                        pltpu.VMEM((nb, 1, gn), F32), pltpu.VMEM((nb, 1, gn), F32)],
        compiler_params=_cparams(("arbitrary", "arbitrary")),
        name="s5_mixer",
    )(u, h0_re, h0_im, ab_re, ab_im, b_mat, c_mat, d_skip, w_glu, b_glu)


def _fox_prompt_kernel(kn_ref, bi_ref, qa_ref, qb_ref, ka_ref, kb_ref, va_ref, vb_ref, o_ref,
                       m_ref, acc_ref, qn_ref, g_ref, *, tq, tk, nk, n_heads):
    i = pl.program_id(2)
    heads = ((qa_ref, ka_ref, va_ref), (qb_ref, kb_ref, vb_ref))
    m_ref[...] = jnp.full_like(m_ref, -jnp.inf)
    acc_ref[...] = jnp.zeros_like(acc_ref)
    lane = lax.broadcasted_iota(jnp.int32, (tq, LANES), 1)
    for hh, (q_ref, _, _) in enumerate(heads):
        qf = jnp.where(lane < HEAD_DIM, q_ref[...].astype(F32), 0.0)
        qn_ref[hh] = jnp.sqrt(jnp.sum(qf * qf, axis=-1, keepdims=True)) * NORM_SLACK
    table = (pl.program_id(0) * n_heads + 2 * pl.program_id(1)) * nk

    def reaches(j):
        hit = False
        for hh in range(2):
            kn = kn_ref[table + hh * nk + j]
            bi = bi_ref[table + hh * nk + j]
            slack = jnp.min(m_ref[hh] - qn_ref[hh] * kn)
            hit = jnp.logical_or(hit, slack <= bi + UNDERFLOW_LOG2)
        return hit

    row0 = i * tq
    diag = row0 // tk

    def scores(j, masked):
        start = pl.multiple_of(j * tk, tk)
        out = []
        for q_ref, k_ref, _ in heads:
            g = lax.dot_general(q_ref[...], k_ref[pl.ds(start, tk), :], (((1,), (1,)), ((), ())),
                                preferred_element_type=F32)
            if masked:
                r = row0 + lax.broadcasted_iota(jnp.int32, (tq, tk), 0)
                c = start + lax.broadcasted_iota(jnp.int32, (tq, tk), 1)
                g = jnp.where(c <= r, g, -jnp.inf)
            out.append(g)
        return out

    def consume(j):
        start = pl.multiple_of(j * tk, tk)
        for hh, (_, _, v_ref) in enumerate(heads):
            g = g_ref[hh]
            m_old = m_ref[hh]
            m_new = jnp.maximum(m_old, jnp.max(g, axis=-1, keepdims=True))
            p = jnp.exp2(g - m_new)
            alpha = jnp.exp2(m_old - m_new)
            acc_ref[hh] = alpha * acc_ref[hh] + jnp.dot(p.astype(BF16), v_ref[pl.ds(start, tk), :],
                                                        preferred_element_type=F32)
            m_ref[hh] = m_new

    for hh, g in enumerate(scores(diag, True)):
        g_ref[hh] = g

    def body(c):
        t, _ = c
        cur = diag - t
        nxt = jnp.maximum(cur - 1, 0)
        g_next = scores(nxt, False)
        consume(cur)
        for hh, g in enumerate(g_next):
            g_ref[hh] = g
        return t + 1, jnp.logical_and(cur >= 1, reaches(nxt))

    lax.while_loop(lambda c: c[1], body, (jnp.int32(0), jnp.bool_(True)))
    outs = []
    for hh in range(2):
        acc = acc_ref[hh]
        outs.append(acc[:, :HEAD_DIM] / acc[:, HEAD_DIM:HEAD_DIM + 1])
    o_ref[...] = jnp.concatenate(outs, axis=-1)


def _fox_prompt(qh, kh, vh, kn_sq, bias_max, *, tq):
    bsz, nh, tp, wl = qh.shape
    nq = tp // tq
    nk = kn_sq.shape[1]
    tk = tp // nk
    assert tk % tq == 0
    stat = lambda a: lax.cummax(jnp.transpose(a[:, :, 0, :nh], (0, 2, 1)), axis=2).reshape(-1)
    kn_tab = stat(jnp.sqrt(kn_sq) * NORM_SLACK)
    bi_tab = stat(bias_max)
    qa = lambda b, p, i, kn, bi: (b, 2 * p, i, 0)
    qb = lambda b, p, i, kn, bi: (b, 2 * p + 1, i, 0)
    fa = lambda b, p, i, kn, bi: (b, 2 * p, 0, 0)
    fb = lambda b, p, i, kn, bi: (b, 2 * p + 1, 0, 0)
    q_spec = lambda im: pl.BlockSpec((None, None, tq, wl), im)
    kv_spec = lambda im: pl.BlockSpec((None, None, tp, wl), im)
    grid_spec = pltpu.PrefetchScalarGridSpec(
        num_scalar_prefetch=2,
        grid=(bsz, nh // 2, nq),
        in_specs=[q_spec(qa), q_spec(qb), kv_spec(fa), kv_spec(fb), kv_spec(fa), kv_spec(fb)],
        out_specs=pl.BlockSpec((None, tq, 2 * HEAD_DIM), lambda b, p, i, kn, bi: (b, i, p)),
        scratch_shapes=[pltpu.VMEM((2, tq, 1), F32), pltpu.VMEM((2, tq, wl), F32), pltpu.VMEM((2, tq, 1), F32),
                        pltpu.VMEM((2, tq, tk), F32)],
    )
    return pl.pallas_call(
        functools.partial(_fox_prompt_kernel, tq=tq, tk=tk, nk=nk, n_heads=nh),
        grid_spec=grid_spec,
        out_shape=jax.ShapeDtypeStruct((bsz, tp, nh * HEAD_DIM), F32),
        compiler_params=_cparams(("arbitrary", "arbitrary", "arbitrary")),
        name="fox_prompt",
    )(kn_tab, bi_tab, qh, qh, kh, kh, vh, vh)


def _fox_sample_kernel(q_ref, kn_ref, vn_ref, ck_ref, cv_ref, bc_ref, bn_ref, o_ref, *, n_heads):
    tq = q_ref.shape[1]
    r = lax.broadcasted_iota(jnp.int32, (tq, tq), 0)
    c = lax.broadcasted_iota(jnp.int32, (tq, tq), 1)
    outs = []
    dn = (((1,), (1,)), ((), ()))
    for hh in range(n_heads):
        sl = slice(hh * HEAD_DIM, (hh + 1) * HEAD_DIM)
        q = q_ref[hh][:, :HEAD_DIM]
        kn = kn_ref[hh][:, :HEAD_DIM]
        vn = vn_ref[hh][:, :HEAD_DIM]
        kc = ck_ref[:, sl].astype(BF16)
        vc = cv_ref[:, sl].astype(BF16)
        g_c = lax.dot_general(q, kc, dn, preferred_element_type=F32) + bc_ref[hh]
        g_n = lax.dot_general(q, kn, dn, preferred_element_type=F32) + bn_ref[hh]
        g_n = jnp.where(c <= r, g_n, -jnp.inf)
        m = jnp.maximum(jnp.max(g_c, axis=-1, keepdims=True), jnp.max(g_n, axis=-1, keepdims=True))
        p_c = jnp.exp2(g_c - m)
        p_n = jnp.exp2(g_n - m)
        den = jnp.sum(p_c, axis=-1, keepdims=True) + jnp.sum(p_n, axis=-1, keepdims=True)
        num = jnp.dot(p_c.astype(BF16), vc, preferred_element_type=F32)
        num = num + jnp.dot(p_n.astype(BF16), vn, preferred_element_type=F32)
        outs.append(num / den)
    o_ref[...] = jnp.concatenate(outs, axis=-1)


def _fox_sample(qh, kh, vh, cache_k, cache_v, bias_cache, bias_new):
    bsz, nh, tq, wl = qh.shape
    past = cache_k.shape[1]
    w = nh * HEAD_DIM
    b4 = lambda b: (b, 0, 0, 0)
    b3 = lambda b: (b, 0, 0)
    return pl.pallas_call(
        functools.partial(_fox_sample_kernel, n_heads=nh),
        grid=(bsz,),
        in_specs=[pl.BlockSpec((None, nh, tq, wl), b4), pl.BlockSpec((None, nh, tq, wl), b4),
                  pl.BlockSpec((None, nh, tq, wl), b4),
                  pl.BlockSpec((None, past, w), b3), pl.BlockSpec((None, past, w), b3),
                  pl.BlockSpec((None, nh, 1, past), b4), pl.BlockSpec((None, nh, 1, tq), b4)],
        out_specs=pl.BlockSpec((None, tq, w), b3),
        out_shape=jax.ShapeDtypeStruct((bsz, tq, w), F32),
        compiler_params=_cparams(("arbitrary",)),
        name="fox_sample",
    )(qh, kh, vh, cache_k, cache_v, bias_cache, bias_new)


def _cumsum_kernel(x_ref, o_ref, carry_ref, *, tm):
    i = pl.program_id(1)

    @pl.when(i == 0)
    def _():
        carry_ref[...] = jnp.zeros_like(carry_ref)

    fc = _prefix_sum_rows(x_ref[...]) + carry_ref[0:1, :]
    o_ref[...] = fc
    carry_ref[0:1, :] = fc[tm - 1:tm, :]


def _cumsum_time(x, *, tm):
    bsz, t, w = x.shape
    row = lambda b, i: (b, i, 0)
    return pl.pallas_call(
        functools.partial(_cumsum_kernel, tm=tm),
        grid=(bsz, t // tm),
        in_specs=[pl.BlockSpec((None, tm, w), row)],
        out_specs=pl.BlockSpec((None, tm, w), row),
        out_shape=jax.ShapeDtypeStruct((bsz, t, w), F32),
        scratch_shapes=[pltpu.VMEM((8, w), F32)],
        compiler_params=_cparams(("arbitrary", "arbitrary")),
        name="cumsum_time",
    )(x)


def _merge_router_kernel(*refs, t_valid, tm, n_meta):
    n_x = 3 if n_meta else 1
    (ys_ref, ya_ref, gs_ref, ga_ref, wo_ref, gf_ref, wr_ref, br_ref, cnt_in_ref,
     x2_ref, ti_ref, tw_ref, tp_ref, cnt_out_ref, cnt_ref) = refs[n_x:]
    first = (pl.program_id(0) == 0) & (pl.program_id(1) == 0)

    @pl.when(first)
    def _():
        cnt_ref[...] = jnp.broadcast_to(cnt_in_ref[...], cnt_ref.shape)

    mix = jnp.concatenate([_rms(ys_ref[...], gs_ref[...]), _rms(ya_ref[...], ga_ref[...])], axis=-1)
    x = _timeline_rows(refs[:n_x], pl.program_id(1), n_meta)
    x2 = x + jnp.dot(mix.astype(BF16), wo_ref[...], preferred_element_type=F32)
    x2_ref[...] = x2
    h2 = _rms(x2, gf_ref[...])
    h_hi = h2.astype(BF16)
    h_lo = (h2 - h_hi.astype(F32)).astype(BF16)
    logits = (jnp.dot(h_hi, wr_ref[0], preferred_element_type=F32)
              + jnp.dot(h_lo, wr_ref[0], preferred_element_type=F32)
              + jnp.dot(h_hi, wr_ref[1], preferred_element_type=F32)) + br_ref[...]

    lane = lax.broadcasted_iota(jnp.int32, (tm, LANES), 1)
    rows = pl.program_id(1) * tm + lax.broadcasted_iota(jnp.int32, (tm, 1), 0)
    valid = rows < t_valid
    vals = logits
    top_v, top_i, sels = [], [], []
    for _ in range(TOP_K):
        mx = jnp.max(vals, axis=-1, keepdims=True)
        idx = jnp.min(jnp.where(vals == mx, lane, LANES), axis=-1, keepdims=True)
        sel = lane == idx
        vals = jnp.where(sel, -jnp.inf, vals)
        top_v.append(mx)
        top_i.append(idx)
        sels.append(sel)
    ex = [jnp.exp(v - top_v[0]) for v in top_v]
    den = ex[0] + ex[1] + ex[2] + ex[3]

    member = jnp.zeros((tm, LANES), F32)
    for sel in sels:
        member = member + jnp.where(valid, jnp.where(sel, 1.0, 0.0), 0.0)
    r = lax.broadcasted_iota(jnp.int32, (tm, tm), 0)
    c = lax.broadcasted_iota(jnp.int32, (tm, tm), 1)
    tri = (c < r).astype(BF16)
    before = jnp.dot(tri, member.astype(BF16), preferred_element_type=F32) + cnt_ref[0:1, :]
    cnt_new = cnt_ref[0:1, :] + jnp.sum(member, axis=0, keepdims=True)
    cnt_ref[0:1, :] = cnt_new
    cnt_out_ref[...] = cnt_new

    ti = jnp.zeros((tm, LANES), jnp.int32)
    tw = jnp.zeros((tm, LANES), F32)
    tpos = jnp.zeros((tm, LANES), F32)
    for kk in range(TOP_K):
        pos = jnp.sum(jnp.where(sels[kk], before, 0.0), axis=-1, keepdims=True)
        ti = jnp.where(lane == kk, top_i[kk], ti)
        tw = jnp.where(lane == kk, ex[kk] / den, tw)
        tpos = jnp.where(lane == kk, pos, tpos)
    ti_ref[...] = ti
    tw_ref[...] = tw
    tp_ref[...] = tpos.astype(jnp.int32)


def _merge_router(x, ys, ya, g_ssm, g_attn, w_out, g_ffn, w_router, b_router, cnt_in, *, tm, meta=None):
    bsz, _, d = x.shape
    x_ops, x_specs, t = _timeline_specs(x, meta, tm)
    w = ys.shape[-1]
    nt = pl.cdiv(t, tm)
    row = lambda b, i: (b, i, 0)
    c2 = lambda b, i: (0, 0)
    return pl.pallas_call(
        functools.partial(_merge_router_kernel, t_valid=t, tm=tm, n_meta=0 if meta is None else meta.shape[0]),
        grid=(bsz, nt),
        in_specs=x_specs + [
            pl.BlockSpec((None, tm, w), row),
            pl.BlockSpec((None, tm, w), row),
            pl.BlockSpec((1, w), c2), pl.BlockSpec((1, w), c2),
            pl.BlockSpec(w_out.shape, c2),
            pl.BlockSpec((1, d), c2),
            pl.BlockSpec(w_router.shape, lambda b, i: (0, 0, 0)),
            pl.BlockSpec((1, LANES), c2),
            pl.BlockSpec((1, LANES), c2),
        ],
        out_specs=[
            pl.BlockSpec((None, tm, d), row),
            pl.BlockSpec((None, tm, LANES), row),
            pl.BlockSpec((None, tm, LANES), row),
            pl.BlockSpec((None, tm, LANES), row),
            pl.BlockSpec((1, LANES), c2),
        ],
        out_shape=[
            jax.ShapeDtypeStruct((bsz, t, d), F32),
            jax.ShapeDtypeStruct((bsz, t, LANES), jnp.int32),
            jax.ShapeDtypeStruct((bsz, t, LANES), F32),
            jax.ShapeDtypeStruct((bsz, t, LANES), jnp.int32),
            jax.ShapeDtypeStruct((1, LANES), F32),
        ],
        scratch_shapes=[pltpu.VMEM((8, LANES), F32)],
        compiler_params=_cparams(("arbitrary", "arbitrary")),
        name="merge_router",
    )(*x_ops, ys, ya, g_ssm, g_attn, w_out, g_ffn, w_router, b_router, cnt_in)


def _dispatch_kernel(last_tile_ref, nu_ref, slots_ref, xa_ref, xb_ref, g_ref, xs_hbm, hbuf_ref, zero_ref, zsem, sem,
                     *, t_valid, nt_a, n_steps_a, tm, n_experts, n_tiles, tile):
    step = pl.program_id(0)

    def zero_tile(t):
        return pltpu.make_async_copy(zero_ref, xs_hbm.at[pl.ds(pl.multiple_of(t * tile, tile), tile)], zsem)

    @pl.when(step == 0)
    def _():
        zero_ref[...] = jnp.zeros_like(zero_ref)

        def start_tail(t, c):
            zero_tile(t).start()
            return c

        def wait_tail(t, c):
            zero_tile(t).wait()
            return c

        lax.fori_loop(nu_ref[0], n_tiles, start_tail, 0)
        lax.fori_loop(nu_ref[0], n_tiles, wait_tail, 0)
        for e in range(n_experts):
            lt = last_tile_ref[e]

            @pl.when(lt >= 0)
            def _():
                zero_tile(lt).start()
        for e in range(n_experts):
            lt = last_tile_ref[e]

            @pl.when(lt >= 0)
            def _():
                zero_tile(lt).wait()

    in_a = step < n_steps_a

    @pl.when(in_a)
    def _():
        hbuf_ref[...] = _rms(xa_ref[...], g_ref[...])

    @pl.when(jnp.logical_not(in_a))
    def _():
        hbuf_ref[...] = _rms(xb_ref[...], g_ref[...])

    def row_copy(r, kk):
        return pltpu.make_async_copy(hbuf_ref.at[pl.ds(r, 1)], xs_hbm.at[pl.ds(slots_ref[r * TOP_K + kk], 1)], sem)

    def issue(r, c):
        for kk in range(TOP_K):
            row_copy(r, kk).start()
        return c

    def drain(r, c):
        for kk in range(TOP_K):
            row_copy(r, kk).wait()
        return c

    n_valid = jnp.where(in_a, jnp.minimum(tm, t_valid - (step % nt_a) * tm), tm)

    @pl.when(n_valid == tm)
    def _():
        lax.fori_loop(0, tm, issue, 0, unroll=ISSUE_UNROLL)
        for _ in range(TOP_K):
            pltpu.make_async_copy(hbuf_ref, xs_hbm.at[pl.ds(0, tm)], sem).wait()

    @pl.when(n_valid < tm)
    def _():
        lax.fori_loop(0, n_valid, issue, 0)
        lax.fori_loop(0, n_valid, drain, 0)


def _dispatch(xa, xb, g_ffn, slots_flat, last_tile, n_used, *, n_slots, tm):
    bsz, t, d = xa.shape
    nt_a = pl.cdiv(t, tm)
    n_steps_a = bsz * nt_a
    n_steps_b = xb.shape[0] // tm
    n_experts = last_tile.shape[0]
    grid_spec = pltpu.PrefetchScalarGridSpec(
        num_scalar_prefetch=2,
        grid=(n_steps_a + n_steps_b,),
        in_specs=[
            pl.BlockSpec((tm * TOP_K,), lambda s, lt, nu: (s,), memory_space=pltpu.SMEM),
            pl.BlockSpec((None, tm, d), lambda s, lt, nu: (jnp.minimum(s, n_steps_a - 1) // nt_a,
                                                          jnp.minimum(s, n_steps_a - 1) % nt_a, 0)),
            pl.BlockSpec((tm, d), lambda s, lt, nu: (jnp.maximum(s - n_steps_a, 0), 0)),
            pl.BlockSpec((1, d), lambda s, lt, nu: (0, 0)),
        ],
        out_specs=pl.BlockSpec(memory_space=pl.ANY),
        scratch_shapes=[pltpu.VMEM((tm, d), F32), pltpu.VMEM((EXPERT_TILE, d), F32),
                        pltpu.SemaphoreType.DMA, pltpu.SemaphoreType.DMA],
    )
    return pl.pallas_call(
        functools.partial(_dispatch_kernel, t_valid=t, nt_a=nt_a, n_steps_a=n_steps_a, tm=tm, n_experts=n_experts,
                          n_tiles=n_slots // EXPERT_TILE, tile=EXPERT_TILE),
        grid_spec=grid_spec,
        out_shape=jax.ShapeDtypeStruct((n_slots, d), F32),
        compiler_params=_cparams(("arbitrary",)),
        name="moe_dispatch",
    )(last_tile, n_used, slots_flat, xa, xb, g_ffn)


def _expert_kernel(te_ref, nu_ref, tpe_ref, x_ref, b1_ref, b2_ref, w1_hbm, w2_hbm, y_ref,
                   w1f_ref, w2f_ref, w1b_ref, w2b_ref, grp_ref, sem1, sem2, *, d_ff):
    i = pl.program_id(0)
    used = nu_ref[0]
    e = te_ref[i]
    prev = te_ref[jnp.maximum(i - 1, 0)]
    changed = (i == 0) | (e != prev)

    def fetch(expert, slot):
        return (pltpu.make_async_copy(w1_hbm.at[expert], w1f_ref.at[slot], sem1.at[slot]),
                pltpu.make_async_copy(w2_hbm.at[expert], w2f_ref.at[slot], sem2.at[slot]))

    @pl.when(i == 0)
    def _():
        grp_ref[0] = 0
        for cp in fetch(e, 0):
            cp.start()

    @pl.when((i < used) & changed)
    def _():
        slot = grp_ref[0] % 2
        for cp in fetch(e, slot):
            cp.wait()
        w1b_ref[...] = w1f_ref[slot].astype(BF16)
        w2b_ref[...] = w2f_ref[slot].astype(BF16)
        nxt = i + tpe_ref[e]

        @pl.when(nxt < used)
        def _():
            for cp in fetch(te_ref[nxt], 1 - slot):
                cp.start()
        grp_ref[0] = grp_ref[0] + 1

    @pl.when(i >= used)
    def _():
        y_ref[...] = jnp.zeros_like(y_ref)

    @pl.when(i < used)
    def _():
        x = x_ref[...].astype(BF16)
        h = jnp.dot(x, w1b_ref[...], preferred_element_type=F32) + b1_ref[...]
        hg = jnp.minimum(h[:, :d_ff], SWIGLU_LIMIT)
        hl = jnp.clip(h[:, d_ff:], -SWIGLU_LIMIT, SWIGLU_LIMIT)
        act = (hl + 1.0) * (hg * jax.nn.sigmoid(SWIGLU_ALPHA * hg))
        y_ref[...] = jnp.dot(act.astype(BF16), w2b_ref[...], preferred_element_type=F32) + b2_ref[...]


def _experts(xs, tile_expert, n_used, tiles_per, w1, b1, w2, b2):
    s, d = xs.shape
    n_tiles = s // EXPERT_TILE
    n_e, _, two_ff = w1.shape
    d_ff = two_ff // 2

    def xmap(i, te, nu, tpe):
        return (i, 0)

    def bmap(i, te, nu, tpe):
        return (te[i], 0, 0)

    grid_spec = pltpu.PrefetchScalarGridSpec(
        num_scalar_prefetch=3,
        grid=(n_tiles,),
        in_specs=[
            pl.BlockSpec((EXPERT_TILE, d), xmap),
            pl.BlockSpec((None, 1, two_ff), bmap),
            pl.BlockSpec((None, 1, d), bmap),
            pl.BlockSpec(memory_space=pl.ANY),
            pl.BlockSpec(memory_space=pl.ANY),
        ],
        out_specs=pl.BlockSpec((EXPERT_TILE, d), xmap),
        scratch_shapes=[pltpu.VMEM((2, d, two_ff), F32), pltpu.VMEM((2, d_ff, d), F32),
                        pltpu.VMEM((d, two_ff), BF16), pltpu.VMEM((d_ff, d), BF16),
                        pltpu.SMEM((1,), jnp.int32),
                        pltpu.SemaphoreType.DMA((2,)), pltpu.SemaphoreType.DMA((2,))],
    )
    return pl.pallas_call(
        functools.partial(_expert_kernel, d_ff=d_ff),
        grid_spec=grid_spec,
        out_shape=jax.ShapeDtypeStruct((s, d), F32),
        compiler_params=_cparams(("arbitrary",)),
        name="moe_experts",
    )(tile_expert, n_used, tiles_per, xs, b1.reshape(n_e, 1, two_ff), b2.reshape(n_e, 1, d), w1, w2)


def _combine_kernel(slots_ref, g_ref, x2_hbm, tw_hbm, ys_hbm, o_ref, xbuf_ref, wbuf_ref, buf_ref, sem, lsem,
                    *, tm, row_offset):
    b = pl.program_id(0)
    start = pl.multiple_of(row_offset + pl.program_id(1) * tm, 8)
    x_copy = pltpu.make_async_copy(x2_hbm.at[b, pl.ds(start, tm)], xbuf_ref, lsem)
    w_copy = pltpu.make_async_copy(tw_hbm.at[b, pl.ds(start, tm)], wbuf_ref, lsem)
    x_copy.start()
    w_copy.start()

    def issue(r, c):
        for kk in range(TOP_K):
            pltpu.make_async_copy(ys_hbm.at[pl.ds(slots_ref[r * TOP_K + kk], 1)],
                                  buf_ref.at[kk, pl.ds(r, 1)], sem).start()
        return c

    lax.fori_loop(0, tm, issue, 0, unroll=ISSUE_UNROLL)
    x_copy.wait()
    w_copy.wait()
    for kk in range(TOP_K):
        pltpu.make_async_copy(ys_hbm.at[pl.ds(0, tm)], buf_ref.at[kk], sem).wait()
    tw = wbuf_ref[...]
    acc = xbuf_ref[...]
    for kk in range(TOP_K):
        acc = acc + tw[:, kk:kk + 1] * buf_ref[kk]
    o_ref[...] = _rms(acc, g_ref[...])


def _combine(x2, tw, slots_flat, ys, g_final, *, tm, row_offset):
    bsz, t, d = x2.shape
    t_out = t - row_offset
    nt = t_out // tm
    grid_spec = pltpu.PrefetchScalarGridSpec(
        num_scalar_prefetch=0,
        grid=(bsz, nt),
        in_specs=[
            pl.BlockSpec((tm * TOP_K,), lambda b, i: (b * nt + i,), memory_space=pltpu.SMEM),
            pl.BlockSpec((1, d), lambda b, i: (0, 0)),
            pl.BlockSpec(memory_space=pl.ANY),
            pl.BlockSpec(memory_space=pl.ANY),
            pl.BlockSpec(memory_space=pl.ANY),
        ],
        out_specs=pl.BlockSpec((None, tm, d), lambda b, i: (b, i, 0)),
        scratch_shapes=[pltpu.VMEM((tm, d), F32), pltpu.VMEM((tm, LANES), F32),
                        pltpu.VMEM((TOP_K, tm, d), F32), pltpu.SemaphoreType.DMA, pltpu.SemaphoreType.DMA],
    )
    return pl.pallas_call(
        functools.partial(_combine_kernel, tm=tm, row_offset=row_offset),
        grid_spec=grid_spec,
        out_shape=jax.ShapeDtypeStruct((bsz, t_out, d), F32),
        compiler_params=_cparams(("arbitrary", "arbitrary")),
        name="moe_combine",
    )(slots_flat, g_final, x2, tw, ys)


def _pad_lanes(a, value=0.0):
    return jnp.pad(a, [(0, 0)] * (a.ndim - 1) + [(0, LANES - a.shape[-1])], constant_values=value)


def kernel(x_prompt, x_sample, cache_k, cache_v, cache_logf, state_ssm_re, state_ssm_im, meta_tokens, norm_mix_g, w_in, b_forget, ssm_a_re, ssm_a_im, ssm_log_dt, ssm_b_re, ssm_b_im, ssm_c_re, ssm_c_im, ssm_d, w_glu, b_glu, g_out_ssm, g_out_attn, w_out, norm_ffn_g, w_router, b_router, w_mlp1, b_mlp1, w_mlp2, b_mlp2, norm_final_g):
    depth = w_in.shape[0]
    assert depth == 1, "the routing tables below are built for a single trunk layer"
    l = 0
    bp, seq, d = x_prompt.shape
    bs, ts, _ = x_sample.shape
    n_heads = b_forget.shape[1]
    n_groups, ssm_n = ssm_a_re.shape[1:]
    n_experts = w_router.shape[2]
    w_ssm = n_groups * SSM_P
    w_attn = n_heads * HEAD_DIM
    gn = n_groups * ssm_n
    past = cache_k.shape[2]
    assert seq % ROW_TILE == 0 and (bs * ts) % ROW_TILE == 0

    meta = meta_tokens.astype(x_prompt.dtype)
    assert meta.shape[0] == N_META
    xs = x_sample
    tp = seq + N_META

    w_main = w_in[l][:, :w_ssm + 3 * w_attn].astype(BF16)
    w_f = _pad_lanes(w_in[l][:, w_ssm + 3 * w_attn:]).astype(BF16)
    b_f = _pad_lanes(b_forget[l][None, :])
    g_mix = norm_mix_g[l][None, :]
    ab_re, ab_im, bb_re, bb_im = _s5_params(ssm_a_re[l], ssm_a_im[l], ssm_log_dt[l], ssm_b_re[l], ssm_b_im[l])
    b_mat, c_mat = _s5_block_mats(bb_re, bb_im, ssm_c_re[l], ssm_c_im[l], n_groups)

    proj = functools.partial(_in_proj, n_heads=n_heads, w_ssm=w_ssm, w_attn=w_attn)
    up, kp, vp, lfp, qhp, khp, vhp, knp, bip = proj(x_prompt, g_mix, w_main, w_f, b_f, tm=ROW_TILE, meta=meta)
    us, ks, vs, lfs, qhs, khs, vhs, _, _ = proj(xs, g_mix, w_main, w_f, b_f, tm=ts)

    s5 = functools.partial(_s5_mixer, ab_re=ab_re, ab_im=ab_im, b_mat=b_mat, c_mat=c_mat,
                           d_skip=ssm_d[l].reshape(1, w_ssm), w_glu=w_glu[l].astype(BF16), b_glu=b_glu[l][None, :])
    zeros_state = jnp.zeros((bp, 1, gn), F32)
    ysp, hrp, hip = s5(up, zeros_state, zeros_state, tm=ROW_TILE, nb=bp)
    yss, hrs, his = s5(us, state_ssm_re[l].reshape(bs, 1, gn), state_ssm_im[l].reshape(bs, 1, gn), tm=ts, nb=1)

    yap = _fox_prompt(qhp, khp, vhp, knp, bip, tq=ATTN_Q_TILE)
    t_all = past + ts
    assert bs * n_heads == LANES
    tm_cum = max(m for m in range(8, ROW_TILE + 1, 8) if t_all % m == 0)
    lf_all = jnp.concatenate([cache_logf[l].astype(F32), lfs], axis=1)
    fcs = _cumsum_time(jnp.transpose(lf_all, (1, 0, 2)).reshape(1, t_all, LANES), tm=tm_cum)
    bias_s = -LOG2E * jnp.transpose(fcs.reshape(t_all, bs, n_heads), (1, 2, 0))[:, :, None, :]
    yas = _fox_sample(qhs, khs, vhs, cache_k[l].reshape(bs, past, w_attn), cache_v[l].reshape(bs, past, w_attn),
                      bias_s[..., :past], bias_s[..., past:])

    wr = _pad_lanes(w_router[l])
    wr_hi = wr.astype(BF16)
    wr_parts = jnp.stack([wr_hi, (wr - wr_hi.astype(F32)).astype(BF16)])
    merge = functools.partial(_merge_router, g_ssm=g_out_ssm[l][None, :], g_attn=g_out_attn[l][None, :],
                              w_out=w_out[l].astype(BF16), g_ffn=norm_ffn_g[l][None, :],
                              w_router=wr_parts, b_router=_pad_lanes(b_router[l][None, :], value=-1e30))
    x2p, tip, twp, tpp, cnt_p = merge(x_prompt, ysp, yap, cnt_in=jnp.zeros((1, LANES), F32), tm=ROW_TILE, meta=meta)
    x2s, tis, tws, tps, cnt = merge(xs, yss, yas, cnt_in=cnt_p, tm=ts)

    n_tok = bp * tp + bs * ts
    counts = cnt[0, :n_experts].astype(jnp.int32)
    tiles_per = (counts + EXPERT_TILE - 1) // EXPERT_TILE
    tile_end = jnp.cumsum(tiles_per)
    tile_start = tile_end - tiles_per
    n_tiles = (n_tok * TOP_K) // EXPERT_TILE + n_experts
    n_slots = n_tiles * EXPERT_TILE
    n_used = tile_end[-1:].astype(jnp.int32)
    tile_ids = jnp.minimum(jnp.arange(n_tiles, dtype=jnp.int32), n_used - 1)
    tile_expert = jnp.sum((tile_end[None, :] <= tile_ids[:, None]).astype(jnp.int32), axis=1)
    last_tile = jnp.where(tiles_per > 0, tile_end - 1, -1).astype(jnp.int32)
    slot_base = (tile_start * EXPERT_TILE).astype(F32)

    def slots_of(ids, pos):
        onehot = jax.nn.one_hot(ids[..., :TOP_K], n_experts, dtype=F32)
        base = jnp.einsum('btke,e->btk', onehot, slot_base, precision=lax.Precision.HIGHEST)
        return base.astype(jnp.int32) + pos[..., :TOP_K]

    slots_p = slots_of(tip, tpp)
    slots_s = slots_of(tis, tps)
    tpad = pl.cdiv(tp, ROW_TILE) * ROW_TILE
    slots_disp = jnp.concatenate([jnp.pad(slots_p, ((0, 0), (0, tpad - tp), (0, 0))).reshape(-1),
                                  slots_s.reshape(-1)])
    xs_sorted = _dispatch(x2p, x2s.reshape(bs * ts, d), norm_ffn_g[l][None, :], slots_disp, last_tile, n_used,
                          n_slots=n_slots, tm=ROW_TILE)
    ys_sorted = _experts(xs_sorted, tile_expert, n_used, tiles_per.astype(jnp.int32),
                         w_mlp1[l], b_mlp1[l], w_mlp2[l], b_mlp2[l])

    g_fin = norm_final_g[None, :]
    y_prompt = _combine(x2p, twp, slots_p[:, N_META:].reshape(-1), ys_sorted, g_fin, tm=EXPERT_TILE, row_offset=N_META)
    y_sample = _combine(x2s.reshape(1, bs * ts, d), tws.reshape(1, bs * ts, LANES), slots_s.reshape(-1), ys_sorted,
                        g_fin, tm=EXPERT_TILE, row_offset=0).reshape(bs, ts, d)

    hd = HEAD_DIM
    st = lambda a, b: a.reshape(1, b, n_groups, ssm_n)
    return (y_prompt, y_sample,
            kp.reshape(1, bp, tp, n_heads, hd), vp.reshape(1, bp, tp, n_heads, hd), lfp[None],
            st(hrp, bp), st(hip, bp),
            ks.reshape(1, bs, ts, n_heads, hd), vs.reshape(1, bs, ts, n_heads, hd), lfs[None],
            st(hrs, bs), st(his, bs))
```

```python
import functools
import math

import numpy as np
import jax
import jax.numpy as jnp
from jax import lax
from jax.experimental import pallas as pl
from jax.experimental.pallas import tpu as pltpu

F32 = jnp.float32
BF16 = jnp.bfloat16

LANES = 128
VMEM_LIMIT_BYTES = 56 * 1024 * 1024

N_META = 16
HEAD_DIM = 64
SSM_P = 16
SSM_N = 64
TOP_K = 4
SWIGLU_LIMIT = 7.0
SWIGLU_ALPHA = 1.702
RMS_EPS = 1e-6
LAMBDA_RE_MAX = -1e-4
LOG2E = math.log2(math.e)

ROW_TILE = 512
ATTN_Q_TILE = 512
EXPERT_TILE = 256
GROUP_SLAB = 8
BIAS_TERMS = 3
ISSUE_UNROLL = 8
UNDERFLOW_LOG2 = 160.0
NORM_SLACK = 1.001


def _cparams(sem):
    return pltpu.CompilerParams(dimension_semantics=sem, vmem_limit_bytes=VMEM_LIMIT_BYTES)


def _rms(v, g):
    return v * lax.rsqrt(jnp.mean(v * v, axis=-1, keepdims=True) + RMS_EPS) * g


def _timeline_rows(x_refs, i, n_meta):
    if not n_meta:
        return x_refs[0][...]
    meta_ref, prev_ref, cur_ref = x_refs
    head = jnp.where(i == 0, meta_ref[...], prev_ref[...])
    return jnp.concatenate([head, cur_ref[...][:cur_ref.shape[0] - n_meta]], axis=0)


def _timeline_specs(x, meta, tm):
    bsz, t_in, d = x.shape
    if meta is None:
        return [x], [pl.BlockSpec((None, tm, d), lambda b, i: (b, i, 0))], t_in
    n_meta = meta.shape[0]
    assert t_in % tm == 0 and tm % n_meta == 0
    per = tm // n_meta
    last = t_in // tm - 1
    specs = [pl.BlockSpec((n_meta, d), lambda b, i: (0, 0)),
             pl.BlockSpec((None, n_meta, d), lambda b, i: (b, jnp.maximum(i * per - 1, 0), 0)),
             pl.BlockSpec((None, tm, d), lambda b, i: (b, jnp.minimum(i, last), 0))]
    return [meta, x, x], specs, t_in + n_meta


def _split_bf16(a):
    pieces = []
    rest = a
    for _ in range(BIAS_TERMS):
        piece = rest.astype(BF16)
        pieces.append(piece)
        rest = rest - piece.astype(F32)
    return jnp.concatenate(pieces, axis=-1)


def _prefix_sum_rows(x):
    tm = x.shape[0]
    r = lax.broadcasted_iota(jnp.int32, (tm, tm), 0)
    c = lax.broadcasted_iota(jnp.int32, (tm, tm), 1)
    tri = jnp.where(c <= r, 1.0, 0.0).astype(BF16)
    s = jnp.dot(tri, _split_bf16(x), preferred_element_type=F32)
    return sum(s[:, t * LANES:(t + 1) * LANES] for t in range(BIAS_TERMS))


def _in_proj_kernel(*refs, t_valid, tm, n_heads, w_ssm, w_attn, n_meta):
    n_x = 3 if n_meta else 1
    (g_ref, w_ref, wf_ref, bf_ref, sel_ref,
     u_ref, k_ref, v_ref, lf_ref, qh_ref, kh_ref, vh_ref, kn_ref, bi_ref, carry_ref) = refs[n_x:]
    i = pl.program_id(1)

    @pl.when(i == 0)
    def _():
        carry_ref[...] = jnp.zeros_like(carry_ref)

    x = _timeline_rows(refs[:n_x], i, n_meta)
    h = _rms(x, g_ref[...]).astype(BF16)
    rows = i * tm + lax.broadcasted_iota(jnp.int32, (tm, 1), 0)
    valid = rows < t_valid
    z = jnp.where(valid, jnp.dot(h, w_ref[...], preferred_element_type=F32), 0.0)
    u_ref[...] = z[:, :w_ssm]
    k_ref[...] = z[:, w_ssm + w_attn:w_ssm + 2 * w_attn]
    v_ref[...] = z[:, w_ssm + 2 * w_attn:w_ssm + 3 * w_attn]

    zf = jnp.dot(h, wf_ref[...], preferred_element_type=F32) + bf_ref[...]
    lf = jnp.where(valid, jnp.minimum(zf, 0.0) - jnp.log1p(jnp.exp(-jnp.abs(zf))), 0.0)
    lf_ref[...] = lf[:, :n_heads]
    fc = _prefix_sum_rows(lf) + carry_ref[0:1, :]
    carry_ref[0:1, :] = fc[tm - 1:tm, :]

    bias = -LOG2E * fc
    bi_ref[...] = jnp.max(bias, axis=0, keepdims=True)
    tails = jnp.dot(_split_bf16(bias), sel_ref[...], preferred_element_type=F32)

    lane = lax.broadcasted_iota(jnp.int32, (tm, LANES), 1)
    lane1 = lax.broadcasted_iota(jnp.int32, (1, LANES), 1)
    low = lane < HEAD_DIM
    in_tail = (lane >= HEAD_DIM) & (lane < HEAD_DIM + BIAS_TERMS)
    q_tail = jnp.where(in_tail, 1.0, 0.0)
    v_tail = jnp.where(lane == HEAD_DIM, 1.0, 0.0)
    kn = jnp.zeros((1, LANES), F32)
    for pair in range(n_heads // 2):
        sl = slice(pair * LANES, (pair + 1) * LANES)
        qt = z[:, w_ssm:w_ssm + w_attn][:, sl] * (LOG2E * HEAD_DIM ** -0.5)
        kt = z[:, w_ssm + w_attn:w_ssm + 2 * w_attn][:, sl]
        vt = z[:, w_ssm + 2 * w_attn:w_ssm + 3 * w_attn][:, sl]
        for half in range(2):
            hh = 2 * pair + half
            if half:
                qt, kt, vt = (pltpu.roll(a, HEAD_DIM, 1) for a in (qt, kt, vt))
            k_tail = jnp.where(in_tail, pltpu.roll(tails, HEAD_DIM - BIAS_TERMS * hh, 1), 0.0)
            kb = jnp.where(low, kt, k_tail).astype(BF16)
            qh_ref[hh] = jnp.where(low, qt, q_tail).astype(BF16)
            kh_ref[hh] = kb
            vh_ref[hh] = jnp.where(low, vt, v_tail).astype(BF16)
            kf = jnp.where(low, kb.astype(F32), 0.0)
            ksq = jnp.max(jnp.sum(kf * kf, axis=-1, keepdims=True), axis=0, keepdims=True)
            kn = jnp.where(lane1 == hh, ksq, kn)
    kn_ref[...] = kn


def _bias_selector(n_heads):
    sel = np.zeros((BIAS_TERMS * LANES, LANES), np.float32)
    for hh in range(n_heads):
        for c in range(BIAS_TERMS):
            sel[c * LANES + hh, BIAS_TERMS * hh + c] = 1.0
    return jnp.asarray(sel, BF16)


def _in_proj(x, g, w_main, w_f, b_f, *, n_heads, w_ssm, w_attn, tm, meta=None):
    bsz, _, d = x.shape
    x_ops, x_specs, t = _timeline_specs(x, meta, tm)
    nt = pl.cdiv(t, tm)
    tp = nt * tm
    row = lambda b, i: (b, i, 0)
    const = lambda b, i: (0, 0)
    head = lambda b, i: (b, 0, i, 0)
    tile = lambda b, i: (b, i, 0, 0)
    sel = _bias_selector(n_heads)
    kern = functools.partial(_in_proj_kernel, t_valid=t, tm=tm, n_heads=n_heads, w_ssm=w_ssm, w_attn=w_attn,
                             n_meta=0 if meta is None else meta.shape[0])
    return pl.pallas_call(
        kern,
        grid=(bsz, nt),
        in_specs=x_specs + [
            pl.BlockSpec((1, d), const),
            pl.BlockSpec(w_main.shape, const),
            pl.BlockSpec(w_f.shape, const),
            pl.BlockSpec((1, LANES), const),
            pl.BlockSpec(sel.shape, const),
        ],
        out_specs=[
            pl.BlockSpec((None, tm, w_ssm), row),
            pl.BlockSpec((None, tm, w_attn), row),
            pl.BlockSpec((None, tm, w_attn), row),
            pl.BlockSpec((None, tm, n_heads), row),
            pl.BlockSpec((None, n_heads, tm, LANES), head),
            pl.BlockSpec((None, n_heads, tm, LANES), head),
            pl.BlockSpec((None, n_heads, tm, LANES), head),
            pl.BlockSpec((None, None, 1, LANES), tile),
            pl.BlockSpec((None, None, 1, LANES), tile),
        ],
        out_shape=[
            jax.ShapeDtypeStruct((bsz, t, w_ssm), F32),
            jax.ShapeDtypeStruct((bsz, t, w_attn), F32),
            jax.ShapeDtypeStruct((bsz, t, w_attn), F32),
            jax.ShapeDtypeStruct((bsz, t, n_heads), F32),
            jax.ShapeDtypeStruct((bsz, n_heads, tp, LANES), BF16),
            jax.ShapeDtypeStruct((bsz, n_heads, tp, LANES), BF16),
            jax.ShapeDtypeStruct((bsz, n_heads, tp, LANES), BF16),
            jax.ShapeDtypeStruct((bsz, nt, 1, LANES), F32),
            jax.ShapeDtypeStruct((bsz, nt, 1, LANES), F32),
        ],
        scratch_shapes=[pltpu.VMEM((8, LANES), F32)],
        compiler_params=_cparams(("arbitrary", "arbitrary")),
        name="in_proj",
    )(*x_ops, g, w_main, w_f, b_f, sel)


def _s5_param_kernel(are_ref, aim_ref, ldt_ref, bre_ref, bim_ref, abr_ref, abi_ref, bbr_ref, bbi_ref):
    lam_re = jnp.minimum(are_ref[...], LAMBDA_RE_MAX)
    lam_im = aim_ref[...]
    dt = jnp.exp(ldt_ref[...])
    mag = jnp.exp(lam_re * dt)
    ab_re = mag * jnp.cos(lam_im * dt)
    ab_im = mag * jnp.sin(lam_im * dt)
    abr_ref[...] = ab_re
    abi_ref[...] = ab_im
    nr = ab_re - 1.0
    ni = ab_im
    den = lam_re * lam_re + lam_im * lam_im
    cr = (nr * lam_re + ni * lam_im) / den
    ci = (ni * lam_re - nr * lam_im) / den
    b_re = bre_ref[...]
    b_im = bim_ref[...]
    bbr_ref[...] = cr * b_re - ci * b_im
    bbi_ref[...] = cr * b_im + ci * b_re


def _s5_params(a_re, a_im, log_dt, b_re, b_im):
    g, n = a_re.shape
    p = b_re.shape[-1]
    gn = g * n
    flat = lambda a: a.reshape(1, gn)
    ldt = jnp.broadcast_to(log_dt[:, None], (g, n)).reshape(1, gn)
    bt = lambda b: jnp.transpose(b, (2, 0, 1)).reshape(p, gn)
    return pl.pallas_call(
        _s5_param_kernel,
        out_shape=[jax.ShapeDtypeStruct((1, gn), F32), jax.ShapeDtypeStruct((1, gn), F32),
                   jax.ShapeDtypeStruct((p, gn), F32), jax.ShapeDtypeStruct((p, gn), F32)],
        name="s5_params",
    )(flat(a_re), flat(a_im), ldt, bt(b_re), bt(b_im))


def _s5_block_mats(bb_re, bb_im, c_re, c_im, n_groups):
    p = bb_re.shape[0]
    n = bb_re.shape[1] // n_groups
    s = n_groups // GROUP_SLAB
    eye = jnp.eye(GROUP_SLAB, dtype=F32)

    def in_blk(bb):
        b4 = bb.reshape(p, s, GROUP_SLAB, n)
        return jnp.einsum('qsgn,gh->sgqhn', b4, eye)

    b_mat = jnp.concatenate([in_blk(bb_re).reshape(s, GROUP_SLAB * p, GROUP_SLAB * n),
                             in_blk(bb_im).reshape(s, GROUP_SLAB * p, GROUP_SLAB * n)], axis=-1)

    def out_blk(c):
        c4 = c.reshape(s, GROUP_SLAB, p, n)
        return jnp.einsum('sgpn,gh->sgnhp', c4, eye)

    c_mat = jnp.concatenate([out_blk(c_re).reshape(s, GROUP_SLAB * n, GROUP_SLAB * p),
                             out_blk(-c_im).reshape(s, GROUP_SLAB * n, GROUP_SLAB * p)], axis=1)
    return b_mat.astype(BF16), c_mat.astype(BF16)


def _s5_kernel(u_ref, h0r_ref, h0i_ref, abr_ref, abi_ref, bm_ref, cm_ref, d_ref, wg_ref, bg_ref,
               y_ref, hr_out_ref, hi_out_ref, sre_ref, sim_ref, hre_ref, him_ref,
               *, t_valid, tm, nb, n_slabs, slab_in, slab_state):
    i = pl.program_id(1)
    nt = pl.num_programs(1)

    @pl.when(i == 0)
    def _():
        hre_ref[...] = h0r_ref[...]
        him_ref[...] = h0i_ref[...]

    for b in range(nb):
        ub = u_ref[b].astype(BF16)
        for s in range(n_slabs):
            z = jnp.dot(ub[:, s * slab_in:(s + 1) * slab_in], bm_ref[s], preferred_element_type=F32)
            sre_ref[b, :, s * slab_state:(s + 1) * slab_state] = z[:, :slab_state]
            sim_ref[b, :, s * slab_state:(s + 1) * slab_state] = z[:, slab_state:]

    a_re = abr_ref[...]
    a_im = abi_ref[...]
    last_row = (t_valid - 1) % tm

    def step(t, carry):
        out = []
        for b in range(nb):
            h_re, h_im = carry[2 * b], carry[2 * b + 1]
            b_re = sre_ref[b, pl.ds(t, 1), :]
            b_im = sim_ref[b, pl.ds(t, 1), :]
            n_re = a_re * h_re - a_im * h_im + b_re
            n_im = a_re * h_im + a_im * h_re + b_im
            sre_ref[b, pl.ds(t, 1), :] = n_re
            sim_ref[b, pl.ds(t, 1), :] = n_im
            out += [n_re, n_im]
        return tuple(out)

    init = tuple(ref[b] for b in range(nb) for ref in (hre_ref, him_ref))
    fin = lax.fori_loop(0, tm, step, init)
    for b in range(nb):
        hre_ref[b] = fin[2 * b]
        him_ref[b] = fin[2 * b + 1]

    @pl.when(i == nt - 1)
    def _():
        for b in range(nb):
            hr_out_ref[b] = sre_ref[b, last_row:last_row + 1, :]
            hi_out_ref[b] = sim_ref[b, last_row:last_row + 1, :]

    for b in range(nb):
        ys = []
        for s in range(n_slabs):
            sl = slice(s * slab_state, (s + 1) * slab_state)
            cm = cm_ref[s]
            y = jnp.dot(sre_ref[b, :, sl].astype(BF16), cm[:slab_state], preferred_element_type=F32)
            y = y + jnp.dot(sim_ref[b, :, sl].astype(BF16), cm[slab_state:], preferred_element_type=F32)
            ys.append(y)
        y = jnp.concatenate(ys, axis=-1) + d_ref[...] * u_ref[b]
        y = jax.nn.gelu(y)
        gate = jnp.dot(y.astype(BF16), wg_ref[...], preferred_element_type=F32) + bg_ref[...]
        y_ref[b] = y * jax.nn.sigmoid(gate)


def _s5_mixer(u, h0_re, h0_im, ab_re, ab_im, b_mat, c_mat, d_skip, w_glu, b_glu, *, tm, nb):
    bsz, t, w = u.shape
    assert bsz % nb == 0
    gn = ab_re.shape[-1]
    n_slabs = b_mat.shape[0]
    nt = pl.cdiv(t, tm)
    row = lambda b, i: (b, i, 0)
    st = lambda b, i: (b, 0, 0)
    c2 = lambda b, i: (0, 0)
    c3 = lambda b, i: (0, 0, 0)
    kern = functools.partial(_s5_kernel, t_valid=t, tm=tm, nb=nb, n_slabs=n_slabs,
                             slab_in=w // n_slabs, slab_state=gn // n_slabs)
    return pl.pallas_call(
        kern,
        grid=(bsz // nb, nt),
        in_specs=[
            pl.BlockSpec((nb, tm, w), row),
            pl.BlockSpec((nb, 1, gn), st),
            pl.BlockSpec((nb, 1, gn), st),
            pl.BlockSpec((1, gn), c2),
            pl.BlockSpec((1, gn), c2),
            pl.BlockSpec(b_mat.shape, c3),
            pl.BlockSpec(c_mat.shape, c3),
            pl.BlockSpec((1, w), c2),
            pl.BlockSpec((w, w), c2),
            pl.BlockSpec((1, w), c2),
        ],
        out_specs=[
            pl.BlockSpec((nb, tm, w), row),
            pl.BlockSpec((nb, 1, gn), st),
            pl.BlockSpec((nb, 1, gn), st),
        ],
        out_shape=[
            jax.ShapeDtypeStruct((bsz, t, w), F32),
            jax.ShapeDtypeStruct((bsz, 1, gn), F32),
            jax.ShapeDtypeStruct((bsz, 1, gn), F32),
        ],
        scratch_shapes=[pltpu.VMEM((nb, tm, gn), F32), pltpu.VMEM((nb, tm, gn), F32),
                        pltpu.VMEM((nb, 1, gn), F32), pltpu.VMEM((nb, 1, gn), F32)],
        compiler_params=_cparams(("arbitrary", "arbitrary")),
        name="s5_mixer",
    )(u, h0_re, h0_im, ab_re, ab_im, b_mat, c_mat, d_skip, w_glu, b_glu)


def _fox_prompt_kernel(kn_ref, bi_ref, qa_ref, qb_ref, ka_ref, kb_ref, va_ref, vb_ref, o_ref,
                       m_ref, acc_ref, qn_ref, g_ref, *, tq, tk, nk, n_heads):
    i = pl.program_id(2)
    heads = ((qa_ref, ka_ref, va_ref), (qb_ref, kb_ref, vb_ref))
    m_ref[...] = jnp.full_like(m_ref, -jnp.inf)
    acc_ref[...] = jnp.zeros_like(acc_ref)
    lane = lax.broadcasted_iota(jnp.int32, (tq, LANES), 1)
    for hh, (q_ref, _, _) in enumerate(heads):
        qf = jnp.where(lane < HEAD_DIM, q_ref[...].astype(F32), 0.0)
        qn_ref[hh] = jnp.sqrt(jnp.sum(qf * qf, axis=-1, keepdims=True)) * NORM_SLACK
    table = (pl.program_id(0) * n_heads + 2 * pl.program_id(1)) * nk

    def reaches(j):
        hit = False
        for hh in range(2):
            kn = kn_ref[table + hh * nk + j]
            bi = bi_ref[table + hh * nk + j]
            slack = jnp.min(m_ref[hh] - qn_ref[hh] * kn)
            hit = jnp.logical_or(hit, slack <= bi + UNDERFLOW_LOG2)
        return hit

    row0 = i * tq
    diag = row0 // tk

    def scores(j, masked):
        start = pl.multiple_of(j * tk, tk)
        out = []
        for q_ref, k_ref, _ in heads:
            g = lax.dot_general(q_ref[...], k_ref[pl.ds(start, tk), :], (((1,), (1,)), ((), ())),
                                preferred_element_type=F32)
            if masked:
                r = row0 + lax.broadcasted_iota(jnp.int32, (tq, tk), 0)
                c = start + lax.broadcasted_iota(jnp.int32, (tq, tk), 1)
                g = jnp.where(c <= r, g, -jnp.inf)
            out.append(g)
        return out

    def consume(j):
        start = pl.multiple_of(j * tk, tk)
        for hh, (_, _, v_ref) in enumerate(heads):
            g = g_ref[hh]
            m_old = m_ref[hh]
            m_new = jnp.maximum(m_old, jnp.max(g, axis=-1, keepdims=True))
            p = jnp.exp2(g - m_new)
            alpha = jnp.exp2(m_old - m_new)
            acc_ref[hh] = alpha * acc_ref[hh] + jnp.dot(p.astype(BF16), v_ref[pl.ds(start, tk), :],
                                                        preferred_element_type=F32)
            m_ref[hh] = m_new

    for hh, g in enumerate(scores(diag, True)):
        g_ref[hh] = g

    def body(c):
        t, _ = c
        cur = diag - t
        nxt = jnp.maximum(cur - 1, 0)
        g_next = scores(nxt, False)
        consume(cur)
        for hh, g in enumerate(g_next):
            g_ref[hh] = g
        return t + 1, jnp.logical_and(cur >= 1, reaches(nxt))

    lax.while_loop(lambda c: c[1], body, (jnp.int32(0), jnp.bool_(True)))
    outs = []
    for hh in range(2):
        acc = acc_ref[hh]
        outs.append(acc[:, :HEAD_DIM] / acc[:, HEAD_DIM:HEAD_DIM + 1])
    o_ref[...] = jnp.concatenate(outs, axis=-1)


def _fox_prompt(qh, kh, vh, kn_sq, bias_max, *, tq):
    bsz, nh, tp, wl = qh.shape
    nq = tp // tq
    nk = kn_sq.shape[1]
    tk = tp // nk
    assert tk % tq == 0
    stat = lambda a: lax.cummax(jnp.transpose(a[:, :, 0, :nh], (0, 2, 1)), axis=2).reshape(-1)
    kn_tab = stat(jnp.sqrt(kn_sq) * NORM_SLACK)
    bi_tab = stat(bias_max)
    qa = lambda b, p, i, kn, bi: (b, 2 * p, i, 0)
    qb = lambda b, p, i, kn, bi: (b, 2 * p + 1, i, 0)
    fa = lambda b, p, i, kn, bi: (b, 2 * p, 0, 0)
    fb = lambda b, p, i, kn, bi: (b, 2 * p + 1, 0, 0)
    q_spec = lambda im: pl.BlockSpec((None, None, tq, wl), im)
    kv_spec = lambda im: pl.BlockSpec((None, None, tp, wl), im)
    grid_spec = pltpu.PrefetchScalarGridSpec(
        num_scalar_prefetch=2,
        grid=(bsz, nh // 2, nq),
        in_specs=[q_spec(qa), q_spec(qb), kv_spec(fa), kv_spec(fb), kv_spec(fa), kv_spec(fb)],
        out_specs=pl.BlockSpec((None, tq, 2 * HEAD_DIM), lambda b, p, i, kn, bi: (b, i, p)),
        scratch_shapes=[pltpu.VMEM((2, tq, 1), F32), pltpu.VMEM((2, tq, wl), F32), pltpu.VMEM((2, tq, 1), F32),
                        pltpu.VMEM((2, tq, tk), F32)],
    )
    return pl.pallas_call(
        functools.partial(_fox_prompt_kernel, tq=tq, tk=tk, nk=nk, n_heads=nh),
        grid_spec=grid_spec,
        out_shape=jax.ShapeDtypeStruct((bsz, tp, nh * HEAD_DIM), F32),
        compiler_params=_cparams(("arbitrary", "arbitrary", "arbitrary")),
        name="fox_prompt",
    )(kn_tab, bi_tab, qh, qh, kh, kh, vh, vh)


def _fox_sample_kernel(q_ref, kn_ref, vn_ref, ck_ref, cv_ref, bc_ref, bn_ref, o_ref, *, n_heads):
    tq = q_ref.shape[1]
    r = lax.broadcasted_iota(jnp.int32, (tq, tq), 0)
    c = lax.broadcasted_iota(jnp.int32, (tq, tq), 1)
    outs = []
    dn = (((1,), (1,)), ((), ()))
    for hh in range(n_heads):
        sl = slice(hh * HEAD_DIM, (hh + 1) * HEAD_DIM)
        q = q_ref[hh][:, :HEAD_DIM]
        kn = kn_ref[hh][:, :HEAD_DIM]
        vn = vn_ref[hh][:, :HEAD_DIM]
        kc = ck_ref[:, sl].astype(BF16)
        vc = cv_ref[:, sl].astype(BF16)
        g_c = lax.dot_general(q, kc, dn, preferred_element_type=F32) + bc_ref[hh]
        g_n = lax.dot_general(q, kn, dn, preferred_element_type=F32) + bn_ref[hh]
        g_n = jnp.where(c <= r, g_n, -jnp.inf)
        m = jnp.maximum(jnp.max(g_c, axis=-1, keepdims=True), jnp.max(g_n, axis=-1, keepdims=True))
        p_c = jnp.exp2(g_c - m)
        p_n = jnp.exp2(g_n - m)
        den = jnp.sum(p_c, axis=-1, keepdims=True) + jnp.sum(p_n, axis=-1, keepdims=True)
        num = jnp.dot(p_c.astype(BF16), vc, preferred_element_type=F32)
        num = num + jnp.dot(p_n.astype(BF16), vn, preferred_element_type=F32)
        outs.append(num / den)
    o_ref[...] = jnp.concatenate(outs, axis=-1)


def _fox_sample(qh, kh, vh, cache_k, cache_v, bias_cache, bias_new, *, tq):
    _, nh, _, wl = qh.shape
    bsz, past, _ = cache_k.shape
    w = nh * HEAD_DIM
    b4 = lambda b: (b, 0, 0, 0)
    b3 = lambda b: (b, 0, 0)
    rows = lambda b: (0, 0, b, 0)
    return pl.pallas_call(
        functools.partial(_fox_sample_kernel, n_heads=nh),
        grid=(bsz,),
        in_specs=[pl.BlockSpec((None, nh, tq, wl), rows), pl.BlockSpec((None, nh, tq, wl), rows),
                  pl.BlockSpec((None, nh, tq, wl), rows),
                  pl.BlockSpec((None, past, w), b3), pl.BlockSpec((None, past, w), b3),
                  pl.BlockSpec((None, nh, 1, past), b4), pl.BlockSpec((None, nh, 1, tq), b4)],
        out_specs=pl.BlockSpec((None, tq, w), b3),
        out_shape=jax.ShapeDtypeStruct((bsz, tq, w), F32),
        compiler_params=_cparams(("arbitrary",)),
        name="fox_sample",
    )(qh, kh, vh, cache_k, cache_v, bias_cache, bias_new)


def _cumsum_kernel(x_ref, o_ref, carry_ref, *, tm):
    i = pl.program_id(1)

    @pl.when(i == 0)
    def _():
        carry_ref[...] = jnp.zeros_like(carry_ref)

    fc = _prefix_sum_rows(x_ref[...]) + carry_ref[0:1, :]
    o_ref[...] = fc
    carry_ref[0:1, :] = fc[tm - 1:tm, :]


def _cumsum_time(x, *, tm):
    bsz, t, w = x.shape
    row = lambda b, i: (b, i, 0)
    return pl.pallas_call(
        functools.partial(_cumsum_kernel, tm=tm),
        grid=(bsz, t // tm),
        in_specs=[pl.BlockSpec((None, tm, w), row)],
        out_specs=pl.BlockSpec((None, tm, w), row),
        out_shape=jax.ShapeDtypeStruct((bsz, t, w), F32),
        scratch_shapes=[pltpu.VMEM((8, w), F32)],
        compiler_params=_cparams(("arbitrary", "arbitrary")),
        name="cumsum_time",
    )(x)


def _merge_router_kernel(*refs, t_valid, tm, n_meta):
    n_x = 3 if n_meta else 1
    (ys_ref, ya_ref, gs_ref, ga_ref, wo_ref, gf_ref, wr_ref, br_ref, cnt_in_ref,
     x2_ref, ti_ref, tw_ref, tp_ref, cnt_out_ref, cnt_ref) = refs[n_x:]
    first = (pl.program_id(0) == 0) & (pl.program_id(1) == 0)

    @pl.when(first)
    def _():
        cnt_ref[...] = jnp.broadcast_to(cnt_in_ref[...], cnt_ref.shape)

    mix = jnp.concatenate([_rms(ys_ref[...], gs_ref[...]), _rms(ya_ref[...], ga_ref[...])], axis=-1)
    x = _timeline_rows(refs[:n_x], pl.program_id(1), n_meta)
    x2 = x + jnp.dot(mix.astype(BF16), wo_ref[...], preferred_element_type=F32)
    x2_ref[...] = x2
    h2 = _rms(x2, gf_ref[...])
    h_hi = h2.astype(BF16)
    h_lo = (h2 - h_hi.astype(F32)).astype(BF16)
    logits = (jnp.dot(h_hi, wr_ref[0], preferred_element_type=F32)
              + jnp.dot(h_lo, wr_ref[0], preferred_element_type=F32)
              + jnp.dot(h_hi, wr_ref[1], preferred_element_type=F32)) + br_ref[...]

    lane = lax.broadcasted_iota(jnp.int32, (tm, LANES), 1)
    rows = pl.program_id(1) * tm + lax.broadcasted_iota(jnp.int32, (tm, 1), 0)
    valid = rows < t_valid
    vals = logits
    top_v, top_i, sels = [], [], []
    for _ in range(TOP_K):
        mx = jnp.max(vals, axis=-1, keepdims=True)
        idx = jnp.min(jnp.where(vals == mx, lane, LANES), axis=-1, keepdims=True)
        sel = lane == idx
        vals = jnp.where(sel, -jnp.inf, vals)
        top_v.append(mx)
        top_i.append(idx)
        sels.append(sel)
    ex = [jnp.exp(v - top_v[0]) for v in top_v]
    den = ex[0] + ex[1] + ex[2] + ex[3]

    member = jnp.zeros((tm, LANES), F32)
    for sel in sels:
        member = member + jnp.where(valid, jnp.where(sel, 1.0, 0.0), 0.0)
    r = lax.broadcasted_iota(jnp.int32, (tm, tm), 0)
    c = lax.broadcasted_iota(jnp.int32, (tm, tm), 1)
    tri = (c < r).astype(BF16)
    before = jnp.dot(tri, member.astype(BF16), preferred_element_type=F32) + cnt_ref[0:1, :]
    cnt_new = cnt_ref[0:1, :] + jnp.sum(member, axis=0, keepdims=True)
    cnt_ref[0:1, :] = cnt_new
    cnt_out_ref[...] = cnt_new

    ti = jnp.zeros((tm, LANES), jnp.int32)
    tw = jnp.zeros((tm, LANES), F32)
    tpos = jnp.zeros((tm, LANES), F32)
    for kk in range(TOP_K):
        pos = jnp.sum(jnp.where(sels[kk], before, 0.0), axis=-1, keepdims=True)
        ti = jnp.where(lane == kk, top_i[kk], ti)
        tw = jnp.where(lane == kk, ex[kk] / den, tw)
        tpos = jnp.where(lane == kk, pos, tpos)
    ti_ref[...] = ti
    tw_ref[...] = tw
    tp_ref[...] = tpos.astype(jnp.int32)


def _merge_router(x, ys, ya, g_ssm, g_attn, w_out, g_ffn, w_router, b_router, cnt_in, *, tm, meta=None):
    bsz, _, d = x.shape
    x_ops, x_specs, t = _timeline_specs(x, meta, tm)
    w = ys.shape[-1]
    nt = pl.cdiv(t, tm)
    row = lambda b, i: (b, i, 0)
    c2 = lambda b, i: (0, 0)
    return pl.pallas_call(
        functools.partial(_merge_router_kernel, t_valid=t, tm=tm, n_meta=0 if meta is None else meta.shape[0]),
        grid=(bsz, nt),
        in_specs=x_specs + [
            pl.BlockSpec((None, tm, w), row),
            pl.BlockSpec((None, tm, w), row),
            pl.BlockSpec((1, w), c2), pl.BlockSpec((1, w), c2),
            pl.BlockSpec(w_out.shape, c2),
            pl.BlockSpec((1, d), c2),
            pl.BlockSpec(w_router.shape, lambda b, i: (0, 0, 0)),
            pl.BlockSpec((1, LANES), c2),
            pl.BlockSpec((1, LANES), c2),
        ],
        out_specs=[
            pl.BlockSpec((None, tm, d), row),
            pl.BlockSpec((None, tm, LANES), row),
            pl.BlockSpec((None, tm, LANES), row),
            pl.BlockSpec((None, tm, LANES), row),
            pl.BlockSpec((1, LANES), c2),
        ],
        out_shape=[
            jax.ShapeDtypeStruct((bsz, t, d), F32),
            jax.ShapeDtypeStruct((bsz, t, LANES), jnp.int32),
            jax.ShapeDtypeStruct((bsz, t, LANES), F32),
            jax.ShapeDtypeStruct((bsz, t, LANES), jnp.int32),
            jax.ShapeDtypeStruct((1, LANES), F32),
        ],
        scratch_shapes=[pltpu.VMEM((8, LANES), F32)],
        compiler_params=_cparams(("arbitrary", "arbitrary")),
        name="merge_router",
    )(*x_ops, ys, ya, g_ssm, g_attn, w_out, g_ffn, w_router, b_router, cnt_in)


def _dispatch_kernel(last_tile_ref, nu_ref, slots_ref, xa_ref, xb_ref, g_ref, xs_hbm, hbuf_ref, zero_ref, zsem, sem,
                     *, t_valid, nt_a, n_steps_a, tm, n_experts, n_tiles, tile):
    step = pl.program_id(0)

    def zero_tile(t):
        return pltpu.make_async_copy(zero_ref, xs_hbm.at[pl.ds(pl.multiple_of(t * tile, tile), tile)], zsem)

    @pl.when(step == 0)
    def _():
        zero_ref[...] = jnp.zeros_like(zero_ref)

        def start_tail(t, c):
            zero_tile(t).start()
            return c

        def wait_tail(t, c):
            zero_tile(t).wait()
            return c

        lax.fori_loop(nu_ref[0], n_tiles, start_tail, 0)
        lax.fori_loop(nu_ref[0], n_tiles, wait_tail, 0)
        for e in range(n_experts):
            lt = last_tile_ref[e]

            @pl.when(lt >= 0)
            def _():
                zero_tile(lt).start()
        for e in range(n_experts):
            lt = last_tile_ref[e]

            @pl.when(lt >= 0)
            def _():
                zero_tile(lt).wait()

    in_a = step < n_steps_a

    @pl.when(in_a)
    def _():
        hbuf_ref[...] = _rms(xa_ref[...], g_ref[...])

    @pl.when(jnp.logical_not(in_a))
    def _():
        hbuf_ref[...] = _rms(xb_ref[...], g_ref[...])

    def row_copy(r, kk):
        return pltpu.make_async_copy(hbuf_ref.at[pl.ds(r, 1)], xs_hbm.at[pl.ds(slots_ref[r * TOP_K + kk], 1)], sem)

    def issue(r, c):
        for kk in range(TOP_K):
            row_copy(r, kk).start()
        return c

    def drain(r, c):
        for kk in range(TOP_K):
            row_copy(r, kk).wait()
        return c

    n_valid = jnp.where(in_a, jnp.minimum(tm, t_valid - (step % nt_a) * tm), tm)

    @pl.when(n_valid == tm)
    def _():
        lax.fori_loop(0, tm, issue, 0, unroll=ISSUE_UNROLL)
        for _ in range(TOP_K):
            pltpu.make_async_copy(hbuf_ref, xs_hbm.at[pl.ds(0, tm)], sem).wait()

    @pl.when(n_valid < tm)
    def _():
        lax.fori_loop(0, n_valid, issue, 0)
        lax.fori_loop(0, n_valid, drain, 0)


def _dispatch(xa, xb, g_ffn, slots_flat, last_tile, n_used, *, n_slots, tm):
    bsz, t, d = xa.shape
    nt_a = pl.cdiv(t, tm)
    n_steps_a = bsz * nt_a
    n_steps_b = xb.shape[0] // tm
    n_experts = last_tile.shape[0]
    grid_spec = pltpu.PrefetchScalarGridSpec(
        num_scalar_prefetch=2,
        grid=(n_steps_a + n_steps_b,),
        in_specs=[
            pl.BlockSpec((tm * TOP_K,), lambda s, lt, nu: (s,), memory_space=pltpu.SMEM),
            pl.BlockSpec((None, tm, d), lambda s, lt, nu: (jnp.minimum(s, n_steps_a - 1) // nt_a,
                                                          jnp.minimum(s, n_steps_a - 1) % nt_a, 0)),
            pl.BlockSpec((tm, d), lambda s, lt, nu: (jnp.maximum(s - n_steps_a, 0), 0)),
            pl.BlockSpec((1, d), lambda s, lt, nu: (0, 0)),
        ],
        out_specs=pl.BlockSpec(memory_space=pl.ANY),
        scratch_shapes=[pltpu.VMEM((tm, d), F32), pltpu.VMEM((EXPERT_TILE, d), F32),
                        pltpu.SemaphoreType.DMA, pltpu.SemaphoreType.DMA],
    )
    return pl.pallas_call(
        functools.partial(_dispatch_kernel, t_valid=t, nt_a=nt_a, n_steps_a=n_steps_a, tm=tm, n_experts=n_experts,
                          n_tiles=n_slots // EXPERT_TILE, tile=EXPERT_TILE),
        grid_spec=grid_spec,
        out_shape=jax.ShapeDtypeStruct((n_slots, d), F32),
        compiler_params=_cparams(("arbitrary",)),
        name="moe_dispatch",
    )(last_tile, n_used, slots_flat, xa, xb, g_ffn)


def _expert_kernel(te_ref, nu_ref, tpe_ref, x_ref, b1_ref, b2_ref, w1_hbm, w2_hbm, y_ref,
                   w1f_ref, w2f_ref, w1b_ref, w2b_ref, grp_ref, sem1, sem2, *, d_ff):
    i = pl.program_id(0)
    used = nu_ref[0]
    e = te_ref[i]
    prev = te_ref[jnp.maximum(i - 1, 0)]
    changed = (i == 0) | (e != prev)

    def fetch(expert, slot):
        return (pltpu.make_async_copy(w1_hbm.at[expert], w1f_ref.at[slot], sem1.at[slot]),
                pltpu.make_async_copy(w2_hbm.at[expert], w2f_ref.at[slot], sem2.at[slot]))

    @pl.when(i == 0)
    def _():
        grp_ref[0] = 0
        for cp in fetch(e, 0):
            cp.start()

    @pl.when((i < used) & changed)
    def _():
        slot = grp_ref[0] % 2
        for cp in fetch(e, slot):
            cp.wait()
        w1b_ref[...] = w1f_ref[slot].astype(BF16)
        w2b_ref[...] = w2f_ref[slot].astype(BF16)
        nxt = i + tpe_ref[e]

        @pl.when(nxt < used)
        def _():
            for cp in fetch(te_ref[nxt], 1 - slot):
                cp.start()
        grp_ref[0] = grp_ref[0] + 1

    @pl.when(i >= used)
    def _():
        y_ref[...] = jnp.zeros_like(y_ref)

    @pl.when(i < used)
    def _():
        x = x_ref[...].astype(BF16)
        h = jnp.dot(x, w1b_ref[...], preferred_element_type=F32) + b1_ref[...]
        hg = jnp.minimum(h[:, :d_ff], SWIGLU_LIMIT)
        hl = jnp.clip(h[:, d_ff:], -SWIGLU_LIMIT, SWIGLU_LIMIT)
        act = (hl + 1.0) * (hg * jax.nn.sigmoid(SWIGLU_ALPHA * hg))
        y_ref[...] = jnp.dot(act.astype(BF16), w2b_ref[...], preferred_element_type=F32) + b2_ref[...]


def _experts(xs, tile_expert, n_used, tiles_per, w1, b1, w2, b2):
    s, d = xs.shape
    n_tiles = s // EXPERT_TILE
    n_e, _, two_ff = w1.shape
    d_ff = two_ff // 2

    def xmap(i, te, nu, tpe):
        return (i, 0)

    def bmap(i, te, nu, tpe):
        return (te[i], 0, 0)

    grid_spec = pltpu.PrefetchScalarGridSpec(
        num_scalar_prefetch=3,
        grid=(n_tiles,),
        in_specs=[
            pl.BlockSpec((EXPERT_TILE, d), xmap),
            pl.BlockSpec((None, 1, two_ff), bmap),
            pl.BlockSpec((None, 1, d), bmap),
            pl.BlockSpec(memory_space=pl.ANY),
            pl.BlockSpec(memory_space=pl.ANY),
        ],
        out_specs=pl.BlockSpec((EXPERT_TILE, d), xmap),
        scratch_shapes=[pltpu.VMEM((2, d, two_ff), F32), pltpu.VMEM((2, d_ff, d), F32),
                        pltpu.VMEM((d, two_ff), BF16), pltpu.VMEM((d_ff, d), BF16),
                        pltpu.SMEM((1,), jnp.int32),
                        pltpu.SemaphoreType.DMA((2,)), pltpu.SemaphoreType.DMA((2,))],
    )
    return pl.pallas_call(
        functools.partial(_expert_kernel, d_ff=d_ff),
        grid_spec=grid_spec,
        out_shape=jax.ShapeDtypeStruct((s, d), F32),
        compiler_params=_cparams(("arbitrary",)),
        name="moe_experts",
    )(tile_expert, n_used, tiles_per, xs, b1.reshape(n_e, 1, two_ff), b2.reshape(n_e, 1, d), w1, w2)


def _combine_kernel(slots_ref, g_ref, x2_hbm, tw_hbm, ys_hbm, o_ref, xbuf_ref, wbuf_ref, buf_ref, sem, lsem,
                    *, tm, row_offset):
    b = pl.program_id(0)
    start = pl.multiple_of(row_offset + pl.program_id(1) * tm, 8)
    x_copy = pltpu.make_async_copy(x2_hbm.at[b, pl.ds(start, tm)], xbuf_ref, lsem)
    w_copy = pltpu.make_async_copy(tw_hbm.at[b, pl.ds(start, tm)], wbuf_ref, lsem)
    x_copy.start()
    w_copy.start()

    def issue(r, c):
        for kk in range(TOP_K):
            pltpu.make_async_copy(ys_hbm.at[pl.ds(slots_ref[r * TOP_K + kk], 1)],
                                  buf_ref.at[kk, pl.ds(r, 1)], sem).start()
        return c

    lax.fori_loop(0, tm, issue, 0, unroll=ISSUE_UNROLL)
    x_copy.wait()
    w_copy.wait()
    for kk in range(TOP_K):
        pltpu.make_async_copy(ys_hbm.at[pl.ds(0, tm)], buf_ref.at[kk], sem).wait()
    tw = wbuf_ref[...]
    acc = xbuf_ref[...]
    for kk in range(TOP_K):
        acc = acc + tw[:, kk:kk + 1] * buf_ref[kk]
    o_ref[...] = _rms(acc, g_ref[...])


def _combine(x2, tw, slots_flat, ys, g_final, *, tm, row_offset):
    bsz, t, d = x2.shape
    t_out = t - row_offset
    nt = t_out // tm
    grid_spec = pltpu.PrefetchScalarGridSpec(
        num_scalar_prefetch=0,
        grid=(bsz, nt),
        in_specs=[
            pl.BlockSpec((tm * TOP_K,), lambda b, i: (b * nt + i,), memory_space=pltpu.SMEM),
            pl.BlockSpec((1, d), lambda b, i: (0, 0)),
            pl.BlockSpec(memory_space=pl.ANY),
            pl.BlockSpec(memory_space=pl.ANY),
            pl.BlockSpec(memory_space=pl.ANY),
        ],
        out_specs=pl.BlockSpec((None, tm, d), lambda b, i: (b, i, 0)),
        scratch_shapes=[pltpu.VMEM((tm, d), F32), pltpu.VMEM((tm, LANES), F32),
                        pltpu.VMEM((TOP_K, tm, d), F32), pltpu.SemaphoreType.DMA, pltpu.SemaphoreType.DMA],
    )
    return pl.pallas_call(
        functools.partial(_combine_kernel, tm=tm, row_offset=row_offset),
        grid_spec=grid_spec,
        out_shape=jax.ShapeDtypeStruct((bsz, t_out, d), F32),
        compiler_params=_cparams(("arbitrary", "arbitrary")),
        name="moe_combine",
    )(slots_flat, g_final, x2, tw, ys)


def _pad_lanes(a, value=0.0):
    return jnp.pad(a, [(0, 0)] * (a.ndim - 1) + [(0, LANES - a.shape[-1])], constant_values=value)


def kernel(x_prompt, x_sample, cache_k, cache_v, cache_logf, state_ssm_re, state_ssm_im, meta_tokens, norm_mix_g, w_in, b_forget, ssm_a_re, ssm_a_im, ssm_log_dt, ssm_b_re, ssm_b_im, ssm_c_re, ssm_c_im, ssm_d, w_glu, b_glu, g_out_ssm, g_out_attn, w_out, norm_ffn_g, w_router, b_router, w_mlp1, b_mlp1, w_mlp2, b_mlp2, norm_final_g):
    depth = w_in.shape[0]
    assert depth == 1, "the routing tables below are built for a single trunk layer"
    l = 0
    bp, seq, d = x_prompt.shape
    bs, ts, _ = x_sample.shape
    n_heads = b_forget.shape[1]
    n_groups, ssm_n = ssm_a_re.shape[1:]
    n_experts = w_router.shape[2]
    w_ssm = n_groups * SSM_P
    w_attn = n_heads * HEAD_DIM
    gn = n_groups * ssm_n
    past = cache_k.shape[2]
    assert seq % ROW_TILE == 0 and (bs * ts) % ROW_TILE == 0

    meta = meta_tokens.astype(x_prompt.dtype)
    assert meta.shape[0] == N_META
    xs = x_sample
    tp = seq + N_META

    w_main = w_in[l][:, :w_ssm + 3 * w_attn].astype(BF16)
    w_f = _pad_lanes(w_in[l][:, w_ssm + 3 * w_attn:]).astype(BF16)
    b_f = _pad_lanes(b_forget[l][None, :])
    g_mix = norm_mix_g[l][None, :]
    ab_re, ab_im, bb_re, bb_im = _s5_params(ssm_a_re[l], ssm_a_im[l], ssm_log_dt[l], ssm_b_re[l], ssm_b_im[l])
    b_mat, c_mat = _s5_block_mats(bb_re, bb_im, ssm_c_re[l], ssm_c_im[l], n_groups)

    proj = functools.partial(_in_proj, n_heads=n_heads, w_ssm=w_ssm, w_attn=w_attn)
    up, kp, vp, lfp, qhp, khp, vhp, knp, bip = proj(x_prompt, g_mix, w_main, w_f, b_f, tm=ROW_TILE, meta=meta)
    n_s = bs * ts
    us, ks, vs, lfs, qhs, khs, vhs, _, _ = proj(xs.reshape(1, n_s, d), g_mix, w_main, w_f, b_f, tm=n_s)
    us = us.reshape(bs, ts, w_ssm)
    lfs = lfs.reshape(bs, ts, n_heads)

    s5 = functools.partial(_s5_mixer, ab_re=ab_re, ab_im=ab_im, b_mat=b_mat, c_mat=c_mat,
                           d_skip=ssm_d[l].reshape(1, w_ssm), w_glu=w_glu[l].astype(BF16), b_glu=b_glu[l][None, :])
    zeros_state = jnp.zeros((bp, 1, gn), F32)
    ysp, hrp, hip = s5(up, zeros_state, zeros_state, tm=ROW_TILE, nb=bp)
    yss, hrs, his = s5(us, state_ssm_re[l].reshape(bs, 1, gn), state_ssm_im[l].reshape(bs, 1, gn), tm=ts, nb=1)

    yap = _fox_prompt(qhp, khp, vhp, knp, bip, tq=ATTN_Q_TILE)
    t_all = past + ts
    assert bs * n_heads == LANES
    tm_cum = max(m for m in range(8, ROW_TILE + 1, 8) if t_all % m == 0)
    lf_all = jnp.concatenate([cache_logf[l].astype(F32), lfs], axis=1)
    fcs = _cumsum_time(jnp.transpose(lf_all, (1, 0, 2)).reshape(1, t_all, LANES), tm=tm_cum)
    bias_s = -LOG2E * jnp.transpose(fcs.reshape(t_all, bs, n_heads), (1, 2, 0))[:, :, None, :]
    yas = _fox_sample(qhs, khs, vhs, cache_k[l].reshape(bs, past, w_attn).astype(BF16),
                      cache_v[l].reshape(bs, past, w_attn).astype(BF16), bias_s[..., :past], bias_s[..., past:], tq=ts)

    wr = _pad_lanes(w_router[l])
    wr_hi = wr.astype(BF16)
    wr_parts = jnp.stack([wr_hi, (wr - wr_hi.astype(F32)).astype(BF16)])
    merge = functools.partial(_merge_router, g_ssm=g_out_ssm[l][None, :], g_attn=g_out_attn[l][None, :],
                              w_out=w_out[l].astype(BF16), g_ffn=norm_ffn_g[l][None, :],
                              w_router=wr_parts, b_router=_pad_lanes(b_router[l][None, :], value=-1e30))
    x2p, tip, twp, tpp, cnt_p = merge(x_prompt, ysp, yap, cnt_in=jnp.zeros((1, LANES), F32), tm=ROW_TILE, meta=meta)
    x2s, tis, tws, tps, cnt = merge(xs.reshape(1, n_s, d), yss.reshape(1, n_s, w_ssm), yas.reshape(1, n_s, w_attn),
                                    cnt_in=cnt_p, tm=n_s)

    n_tok = bp * tp + bs * ts
    counts = cnt[0, :n_experts].astype(jnp.int32)
    tiles_per = (counts + EXPERT_TILE - 1) // EXPERT_TILE
    tile_end = jnp.cumsum(tiles_per)
    tile_start = tile_end - tiles_per
    n_tiles = (n_tok * TOP_K) // EXPERT_TILE + n_experts
    n_slots = n_tiles * EXPERT_TILE
    n_used = tile_end[-1:].astype(jnp.int32)
    tile_ids = jnp.minimum(jnp.arange(n_tiles, dtype=jnp.int32), n_used - 1)
    tile_expert = jnp.sum((tile_end[None, :] <= tile_ids[:, None]).astype(jnp.int32), axis=1)
    last_tile = jnp.where(tiles_per > 0, tile_end - 1, -1).astype(jnp.int32)
    slot_base = (tile_start * EXPERT_TILE).astype(F32)

    def slots_of(ids, pos):
        onehot = jax.nn.one_hot(ids[..., :TOP_K], n_experts, dtype=F32)
        base = jnp.einsum('btke,e->btk', onehot, slot_base, precision=lax.Precision.HIGHEST)
        return base.astype(jnp.int32) + pos[..., :TOP_K]

    slots_p = slots_of(tip, tpp)
    slots_s = slots_of(tis, tps)
    tpad = pl.cdiv(tp, ROW_TILE) * ROW_TILE
    slots_disp = jnp.concatenate([jnp.pad(slots_p, ((0, 0), (0, tpad - tp), (0, 0))).reshape(-1),
                                  slots_s.reshape(-1)])
    xs_sorted = _dispatch(x2p, x2s.reshape(bs * ts, d), norm_ffn_g[l][None, :], slots_disp, last_tile, n_used,
                          n_slots=n_slots, tm=ROW_TILE)
    ys_sorted = _experts(xs_sorted, tile_expert, n_used, tiles_per.astype(jnp.int32),
                         w_mlp1[l], b_mlp1[l], w_mlp2[l], b_mlp2[l])

    g_fin = norm_final_g[None, :]
    y_prompt = _combine(x2p, twp, slots_p[:, N_META:].reshape(-1), ys_sorted, g_fin, tm=EXPERT_TILE, row_offset=N_META)
    y_sample = _combine(x2s.reshape(1, bs * ts, d), tws.reshape(1, bs * ts, LANES), slots_s.reshape(-1), ys_sorted,
                        g_fin, tm=EXPERT_TILE, row_offset=0).reshape(bs, ts, d)

    hd = HEAD_DIM
    st = lambda a, b: a.reshape(1, b, n_groups, ssm_n)
    return (y_prompt, y_sample,
            kp.reshape(1, bp, tp, n_heads, hd), vp.reshape(1, bp, tp, n_heads, hd), lfp[None],
            st(hrp, bp), st(hip, bp),
            ks.reshape(1, bs, ts, n_heads, hd), vs.reshape(1, bs, ts, n_heads, hd), lfs[None],
            st(hrs, bs), st(his, bs))
```

```python
import functools
import math

import numpy as np
import jax
import jax.numpy as jnp
from jax import lax
from jax.experimental import pallas as pl
from jax.experimental.pallas import tpu as pltpu

F32 = jnp.float32
BF16 = jnp.bfloat16

LANES = 128
VMEM_LIMIT_BYTES = 56 * 1024 * 1024

N_META = 16
HEAD_DIM = 64
SSM_P = 16
SSM_N = 64
TOP_K = 4
SWIGLU_LIMIT = 7.0
SWIGLU_ALPHA = 1.702
RMS_EPS = 1e-6
LAMBDA_RE_MAX = -1e-4
LOG2E = math.log2(math.e)

ROW_TILE = 512
ATTN_Q_TILE = 512
EXPERT_TILE = 256
GROUP_SLAB = 8
BIAS_TERMS = 3
ISSUE_UNROLL = 8
UNDERFLOW_LOG2 = 152.0
NORM_SLACK = 1.001


def _cparams(sem):
    return pltpu.CompilerParams(dimension_semantics=sem, vmem_limit_bytes=VMEM_LIMIT_BYTES)


def _rms(v, g):
    return v * lax.rsqrt(jnp.mean(v * v, axis=-1, keepdims=True) + RMS_EPS) * g


def _timeline_rows(x_refs, i, n_meta):
    if not n_meta:
        return x_refs[0][...]
    meta_ref, prev_ref, cur_ref = x_refs
    head = jnp.where(i == 0, meta_ref[...], prev_ref[...])
    return jnp.concatenate([head, cur_ref[...][:cur_ref.shape[0] - n_meta]], axis=0)


def _timeline_specs(x, meta, tm):
    bsz, t_in, d = x.shape
    if meta is None:
        return [x], [pl.BlockSpec((None, tm, d), lambda b, i: (b, i, 0))], t_in
    n_meta = meta.shape[0]
    assert t_in % tm == 0 and tm % n_meta == 0
    per = tm // n_meta
    last = t_in // tm - 1
    specs = [pl.BlockSpec((n_meta, d), lambda b, i: (0, 0)),
             pl.BlockSpec((None, n_meta, d), lambda b, i: (b, jnp.maximum(i * per - 1, 0), 0)),
             pl.BlockSpec((None, tm, d), lambda b, i: (b, jnp.minimum(i, last), 0))]
    return [meta, x, x], specs, t_in + n_meta


def _split_bf16(a):
    pieces = []
    rest = a
    for _ in range(BIAS_TERMS):
        piece = rest.astype(BF16)
        pieces.append(piece)
        rest = rest - piece.astype(F32)
    return jnp.concatenate(pieces, axis=-1)


def _prefix_sum_rows(x):
    tm = x.shape[0]
    r = lax.broadcasted_iota(jnp.int32, (tm, tm), 0)
    c = lax.broadcasted_iota(jnp.int32, (tm, tm), 1)
    tri = jnp.where(c <= r, 1.0, 0.0).astype(BF16)
    s = jnp.dot(tri, _split_bf16(x), preferred_element_type=F32)
    return sum(s[:, t * LANES:(t + 1) * LANES] for t in range(BIAS_TERMS))


def _in_proj_kernel(*refs, t_valid, tm, n_heads, w_ssm, w_attn, n_meta):
    n_x = 3 if n_meta else 1
    (g_ref, w_ref, wf_ref, bf_ref, sel_ref,
     u_ref, k_ref, v_ref, lf_ref, qh_ref, kh_ref, vh_ref, kn_ref, bi_ref, carry_ref) = refs[n_x:]
    i = pl.program_id(1)

    @pl.when(i == 0)
    def _():
        carry_ref[...] = jnp.zeros_like(carry_ref)

    x = _timeline_rows(refs[:n_x], i, n_meta)
    h = _rms(x, g_ref[...]).astype(BF16)
    rows = i * tm + lax.broadcasted_iota(jnp.int32, (tm, 1), 0)
    valid = rows < t_valid
    z = jnp.where(valid, jnp.dot(h, w_ref[...], preferred_element_type=F32), 0.0)
    u_ref[...] = z[:, :w_ssm]
    k_ref[...] = z[:, w_ssm + w_attn:w_ssm + 2 * w_attn]
    v_ref[...] = z[:, w_ssm + 2 * w_attn:w_ssm + 3 * w_attn]

    zf = jnp.dot(h, wf_ref[...], preferred_element_type=F32) + bf_ref[...]
    lf = jnp.where(valid, jnp.minimum(zf, 0.0) - jnp.log1p(jnp.exp(-jnp.abs(zf))), 0.0)
    lf_ref[...] = lf[:, :n_heads]
    fc = _prefix_sum_rows(lf) + carry_ref[0:1, :]
    carry_ref[0:1, :] = fc[tm - 1:tm, :]

    bias = -LOG2E * fc
    bi_ref[...] = jnp.max(bias, axis=0, keepdims=True)
    tails = jnp.dot(_split_bf16(bias), sel_ref[...], preferred_element_type=F32)

    lane = lax.broadcasted_iota(jnp.int32, (tm, LANES), 1)
    lane1 = lax.broadcasted_iota(jnp.int32, (1, LANES), 1)
    low = lane < HEAD_DIM
    in_tail = (lane >= HEAD_DIM) & (lane < HEAD_DIM + BIAS_TERMS)
    q_tail = jnp.where(in_tail, 1.0, 0.0)
    v_tail = jnp.where(lane == HEAD_DIM, 1.0, 0.0)
    kn = jnp.zeros((1, LANES), F32)
    for pair in range(n_heads // 2):
        sl = slice(pair * LANES, (pair + 1) * LANES)
        qt = z[:, w_ssm:w_ssm + w_attn][:, sl] * (LOG2E * HEAD_DIM ** -0.5)
        kt = z[:, w_ssm + w_attn:w_ssm + 2 * w_attn][:, sl]
        vt = z[:, w_ssm + 2 * w_attn:w_ssm + 3 * w_attn][:, sl]
        for half in range(2):
            hh = 2 * pair + half
            if half:
                qt, kt, vt = (pltpu.roll(a, HEAD_DIM, 1) for a in (qt, kt, vt))
            k_tail = jnp.where(in_tail, pltpu.roll(tails, HEAD_DIM - BIAS_TERMS * hh, 1), 0.0)
            kb = jnp.where(low, kt, k_tail).astype(BF16)
            qh_ref[hh] = jnp.where(low, qt, q_tail).astype(BF16)
            kh_ref[hh] = kb
            vh_ref[hh] = jnp.where(low, vt, v_tail).astype(BF16)
            kf = jnp.where(low, kb.astype(F32), 0.0)
            ksq = jnp.max(jnp.sum(kf * kf, axis=-1, keepdims=True), axis=0, keepdims=True)
            kn = jnp.where(lane1 == hh, ksq, kn)
    kn_ref[...] = kn


def _bias_selector(n_heads):
    sel = np.zeros((BIAS_TERMS * LANES, LANES), np.float32)
    for hh in range(n_heads):
        for c in range(BIAS_TERMS):
            sel[c * LANES + hh, BIAS_TERMS * hh + c] = 1.0
    return jnp.asarray(sel, BF16)


def _in_proj(x, g, w_main, w_f, b_f, *, n_heads, w_ssm, w_attn, tm, meta=None):
    bsz, _, d = x.shape
    x_ops, x_specs, t = _timeline_specs(x, meta, tm)
    nt = pl.cdiv(t, tm)
    tp = nt * tm
    row = lambda b, i: (b, i, 0)
    const = lambda b, i: (0, 0)
    head = lambda b, i: (b, 0, i, 0)
    tile = lambda b, i: (b, i, 0, 0)
    sel = _bias_selector(n_heads)
    kern = functools.partial(_in_proj_kernel, t_valid=t, tm=tm, n_heads=n_heads, w_ssm=w_ssm, w_attn=w_attn,
                             n_meta=0 if meta is None else meta.shape[0])
    return pl.pallas_call(
        kern,
        grid=(bsz, nt),
        in_specs=x_specs + [
            pl.BlockSpec((1, d), const),
            pl.BlockSpec(w_main.shape, const),
            pl.BlockSpec(w_f.shape, const),
            pl.BlockSpec((1, LANES), const),
            pl.BlockSpec(sel.shape, const),
        ],
        out_specs=[
            pl.BlockSpec((None, tm, w_ssm), row),
            pl.BlockSpec((None, tm, w_attn), row),
            pl.BlockSpec((None, tm, w_attn), row),
            pl.BlockSpec((None, tm, n_heads), row),
            pl.BlockSpec((None, n_heads, tm, LANES), head),
            pl.BlockSpec((None, n_heads, tm, LANES), head),
            pl.BlockSpec((None, n_heads, tm, LANES), head),
            pl.BlockSpec((None, None, 1, LANES), tile),
            pl.BlockSpec((None, None, 1, LANES), tile),
        ],
        out_shape=[
            jax.ShapeDtypeStruct((bsz, t, w_ssm), F32),
            jax.ShapeDtypeStruct((bsz, t, w_attn), F32),
            jax.ShapeDtypeStruct((bsz, t, w_attn), F32),
            jax.ShapeDtypeStruct((bsz, t, n_heads), F32),
            jax.ShapeDtypeStruct((bsz, n_heads, tp, LANES), BF16),
            jax.ShapeDtypeStruct((bsz, n_heads, tp, LANES), BF16),
            jax.ShapeDtypeStruct((bsz, n_heads, tp, LANES), BF16),
            jax.ShapeDtypeStruct((bsz, nt, 1, LANES), F32),
            jax.ShapeDtypeStruct((bsz, nt, 1, LANES), F32),
        ],
        scratch_shapes=[pltpu.VMEM((8, LANES), F32)],
        compiler_params=_cparams(("arbitrary", "arbitrary")),
        name="in_proj",
    )(*x_ops, g, w_main, w_f, b_f, sel)


def _s5_param_kernel(are_ref, aim_ref, ldt_ref, bre_ref, bim_ref, abr_ref, abi_ref, bbr_ref, bbi_ref):
    lam_re = jnp.minimum(are_ref[...], LAMBDA_RE_MAX)
    lam_im = aim_ref[...]
    dt = jnp.exp(ldt_ref[...])
    mag = jnp.exp(lam_re * dt)
    ab_re = mag * jnp.cos(lam_im * dt)
    ab_im = mag * jnp.sin(lam_im * dt)
    abr_ref[...] = ab_re
    abi_ref[...] = ab_im
    nr = ab_re - 1.0
    ni = ab_im
    den = lam_re * lam_re + lam_im * lam_im
    cr = (nr * lam_re + ni * lam_im) / den
    ci = (ni * lam_re - nr * lam_im) / den
    b_re = bre_ref[...]
    b_im = bim_ref[...]
    bbr_ref[...] = cr * b_re - ci * b_im
    bbi_ref[...] = cr * b_im + ci * b_re


def _s5_params(a_re, a_im, log_dt, b_re, b_im):
    g, n = a_re.shape
    p = b_re.shape[-1]
    gn = g * n
    flat = lambda a: a.reshape(1, gn)
    ldt = jnp.broadcast_to(log_dt[:, None], (g, n)).reshape(1, gn)
    bt = lambda b: jnp.transpose(b, (2, 0, 1)).reshape(p, gn)
    return pl.pallas_call(
        _s5_param_kernel,
        out_shape=[jax.ShapeDtypeStruct((1, gn), F32), jax.ShapeDtypeStruct((1, gn), F32),
                   jax.ShapeDtypeStruct((p, gn), F32), jax.ShapeDtypeStruct((p, gn), F32)],
        name="s5_params",
    )(flat(a_re), flat(a_im), ldt, bt(b_re), bt(b_im))


def _s5_block_mats(bb_re, bb_im, c_re, c_im, n_groups):
    p = bb_re.shape[0]
    n = bb_re.shape[1] // n_groups
    s = n_groups // GROUP_SLAB
    eye = jnp.eye(GROUP_SLAB, dtype=F32)

    def in_blk(bb):
        b4 = bb.reshape(p, s, GROUP_SLAB, n)
        return jnp.einsum('qsgn,gh->sgqhn', b4, eye)

    b_mat = jnp.concatenate([in_blk(bb_re).reshape(s, GROUP_SLAB * p, GROUP_SLAB * n),
                             in_blk(bb_im).reshape(s, GROUP_SLAB * p, GROUP_SLAB * n)], axis=-1)

    def out_blk(c):
        c4 = c.reshape(s, GROUP_SLAB, p, n)
        return jnp.einsum('sgpn,gh->sgnhp', c4, eye)

    c_mat = jnp.concatenate([out_blk(c_re).reshape(s, GROUP_SLAB * n, GROUP_SLAB * p),
                             out_blk(-c_im).reshape(s, GROUP_SLAB * n, GROUP_SLAB * p)], axis=1)
    return b_mat.astype(BF16), c_mat.astype(BF16)


def _s5_kernel(u_ref, h0r_ref, h0i_ref, abr_ref, abi_ref, bm_ref, cm_ref, d_ref, wg_ref, bg_ref,
               y_ref, hr_out_ref, hi_out_ref, sre_ref, sim_ref, hre_ref, him_ref,
               *, t_valid, tm, nb, n_slabs, slab_in, slab_state):
    i = pl.program_id(1)
    nt = pl.num_programs(1)

    @pl.when(i == 0)
    def _():
        hre_ref[...] = h0r_ref[...]
        him_ref[...] = h0i_ref[...]

    for b in range(nb):
        ub = u_ref[b].astype(BF16)
        for s in range(n_slabs):
            z = jnp.dot(ub[:, s * slab_in:(s + 1) * slab_in], bm_ref[s], preferred_element_type=F32)
            sre_ref[b, :, s * slab_state:(s + 1) * slab_state] = z[:, :slab_state]
            sim_ref[b, :, s * slab_state:(s + 1) * slab_state] = z[:, slab_state:]

    a_re = abr_ref[...]
    a_im = abi_ref[...]
    last_row = (t_valid - 1) % tm

    def step(t, carry):
        out = []
        for b in range(nb):
            h_re, h_im = carry[2 * b], carry[2 * b + 1]
            b_re = sre_ref[b, pl.ds(t, 1), :]
            b_im = sim_ref[b, pl.ds(t, 1), :]
            n_re = a_re * h_re - a_im * h_im + b_re
            n_im = a_re * h_im + a_im * h_re + b_im
            sre_ref[b, pl.ds(t, 1), :] = n_re
            sim_ref[b, pl.ds(t, 1), :] = n_im
            out += [n_re, n_im]
        return tuple(out)

    init = tuple(ref[b] for b in range(nb) for ref in (hre_ref, him_ref))
    fin = lax.fori_loop(0, tm, step, init)
    for b in range(nb):
        hre_ref[b] = fin[2 * b]
        him_ref[b] = fin[2 * b + 1]

    @pl.when(i == nt - 1)
    def _():
        for b in range(nb):
            hr_out_ref[b] = sre_ref[b, last_row:last_row + 1, :]
            hi_out_ref[b] = sim_ref[b, last_row:last_row + 1, :]

    for b in range(nb):
        ys = []
        for s in range(n_slabs):
            sl = slice(s * slab_state, (s + 1) * slab_state)
            cm = cm_ref[s]
            y = jnp.dot(sre_ref[b, :, sl].astype(BF16), cm[:slab_state], preferred_element_type=F32)
            y = y + jnp.dot(sim_ref[b, :, sl].astype(BF16), cm[slab_state:], preferred_element_type=F32)
            ys.append(y)
        y = jnp.concatenate(ys, axis=-1) + d_ref[...] * u_ref[b]
        y = jax.nn.gelu(y)
        gate = jnp.dot(y.astype(BF16), wg_ref[...], preferred_element_type=F32) + bg_ref[...]
        y_ref[b] = y * jax.nn.sigmoid(gate)


def _s5_mixer(u, h0_re, h0_im, ab_re, ab_im, b_mat, c_mat, d_skip, w_glu, b_glu, *, tm, nb):
    bsz, t, w = u.shape
    assert bsz % nb == 0
    gn = ab_re.shape[-1]
    n_slabs = b_mat.shape[0]
    nt = pl.cdiv(t, tm)
    row = lambda b, i: (b, i, 0)
    st = lambda b, i: (b, 0, 0)
    c2 = lambda b, i: (0, 0)
    c3 = lambda b, i: (0, 0, 0)
    kern = functools.partial(_s5_kernel, t_valid=t, tm=tm, nb=nb, n_slabs=n_slabs,
                             slab_in=w // n_slabs, slab_state=gn // n_slabs)
    return pl.pallas_call(
        kern,
        grid=(bsz // nb, nt),
        in_specs=[
            pl.BlockSpec((nb, tm, w), row),
            pl.BlockSpec((nb, 1, gn), st),
            pl.BlockSpec((nb, 1, gn), st),
            pl.BlockSpec((1, gn), c2),
            pl.BlockSpec((1, gn), c2),
            pl.BlockSpec(b_mat.shape, c3),
            pl.BlockSpec(c_mat.shape, c3),
            pl.BlockSpec((1, w), c2),
            pl.BlockSpec((w, w), c2),
            pl.BlockSpec((1, w), c2),
        ],
        out_specs=[
            pl.BlockSpec((nb, tm, w), row),
            pl.BlockSpec((nb, 1, gn), st),
            pl.BlockSpec((nb, 1, gn), st),
        ],
        out_shape=[
            jax.ShapeDtypeStruct((bsz, t, w), F32),
            jax.ShapeDtypeStruct((bsz, 1, gn), F32),
            jax.ShapeDtypeStruct((bsz, 1, gn), F32),
        ],
        scratch_shapes=[pltpu.VMEM((nb, tm, gn), F32), pltpu.VMEM((nb, tm, gn), F32),
                        pltpu.VMEM((nb, 1, gn), F32), pltpu.VMEM((nb, 1, gn), F32)],
        compiler_params=_cparams(("arbitrary", "arbitrary")),
        name="s5_mixer",
    )(u, h0_re, h0_im, ab_re, ab_im, b_mat, c_mat, d_skip, w_glu, b_glu)


def _fox_prompt_kernel(kn_ref, bi_ref, qa_ref, qb_ref, ka_ref, kb_ref, va_ref, vb_ref, o_ref,
                       m_ref, acc_ref, qn_ref, g_ref, *, tq, tk, nk, n_heads):
    i = pl.program_id(2)
    heads = ((qa_ref, ka_ref, va_ref), (qb_ref, kb_ref, vb_ref))
    m_ref[...] = jnp.full_like(m_ref, -jnp.inf)
    acc_ref[...] = jnp.zeros_like(acc_ref)
    lane = lax.broadcasted_iota(jnp.int32, (tq, LANES), 1)
    for hh, (q_ref, _, _) in enumerate(heads):
        qf = jnp.where(lane < HEAD_DIM, q_ref[...].astype(F32), 0.0)
        qn_ref[hh] = jnp.sqrt(jnp.sum(qf * qf, axis=-1, keepdims=True)) * NORM_SLACK
    table = (pl.program_id(0) * n_heads + 2 * pl.program_id(1)) * nk

    def reaches(j):
        hit = False
        for hh in range(2):
            kn = kn_ref[table + hh * nk + j]
            bi = bi_ref[table + hh * nk + j]
            slack = jnp.min(m_ref[hh] - qn_ref[hh] * kn)
            hit = jnp.logical_or(hit, slack <= bi + UNDERFLOW_LOG2)
        return hit

    row0 = i * tq
    diag = row0 // tk

    def scores(j, masked):
        start = pl.multiple_of(j * tk, tk)
        out = []
        for q_ref, k_ref, _ in heads:
            g = lax.dot_general(q_ref[...], k_ref[pl.ds(start, tk), :], (((1,), (1,)), ((), ())),
                                preferred_element_type=F32)
            if masked:
                r = row0 + lax.broadcasted_iota(jnp.int32, (tq, tk), 0)
                c = start + lax.broadcasted_iota(jnp.int32, (tq, tk), 1)
                g = jnp.where(c <= r, g, -jnp.inf)
            out.append(g)
        return out

    def consume(j):
        start = pl.multiple_of(j * tk, tk)
        for hh, (_, _, v_ref) in enumerate(heads):
            g = g_ref[hh]
            m_old = m_ref[hh]
            m_new = jnp.maximum(m_old, jnp.max(g, axis=-1, keepdims=True))
            p = jnp.exp2(g - m_new)
            alpha = jnp.exp2(m_old - m_new)
            acc_ref[hh] = alpha * acc_ref[hh] + jnp.dot(p.astype(BF16), v_ref[pl.ds(start, tk), :],
                                                        preferred_element_type=F32)
            m_ref[hh] = m_new

    for hh, g in enumerate(scores(diag, True)):
        g_ref[hh] = g

    def body(c):
        t, _ = c
        cur = diag - t
        nxt = jnp.maximum(cur - 1, 0)
        consume(cur)
        g_next = scores(nxt, False)
        for hh, g in enumerate(g_next):
            g_ref[hh] = g
        return t + 1, jnp.logical_and(cur >= 1, reaches(nxt))

    lax.while_loop(lambda c: c[1], body, (jnp.int32(0), jnp.bool_(True)))
    outs = []
    for hh in range(2):
        acc = acc_ref[hh]
        outs.append(acc[:, :HEAD_DIM] / acc[:, HEAD_DIM:HEAD_DIM + 1])
    o_ref[...] = jnp.concatenate(outs, axis=-1)


def _fox_prompt(qh, kh, vh, kn_sq, bias_max, *, tq):
    bsz, nh, tp, wl = qh.shape
    nq = tp // tq
    nk = kn_sq.shape[1]
    tk = tp // nk
    assert tk % tq == 0
    stat = lambda a: lax.cummax(jnp.transpose(a[:, :, 0, :nh], (0, 2, 1)), axis=2).reshape(-1)
    kn_tab = stat(jnp.sqrt(kn_sq) * NORM_SLACK)
    bi_tab = stat(bias_max)
    qa = lambda b, p, i, kn, bi: (b, 2 * p, i, 0)
    qb = lambda b, p, i, kn, bi: (b, 2 * p + 1, i, 0)
    fa = lambda b, p, i, kn, bi: (b, 2 * p, 0, 0)
    fb = lambda b, p, i, kn, bi: (b, 2 * p + 1, 0, 0)
    q_spec = lambda im: pl.BlockSpec((None, None, tq, wl), im)
    kv_spec = lambda im: pl.BlockSpec((None, None, tp, wl), im)
    grid_spec = pltpu.PrefetchScalarGridSpec(
        num_scalar_prefetch=2,
        grid=(bsz, nh // 2, nq),
        in_specs=[q_spec(qa), q_spec(qb), kv_spec(fa), kv_spec(fb), kv_spec(fa), kv_spec(fb)],
        out_specs=pl.BlockSpec((None, tq, 2 * HEAD_DIM), lambda b, p, i, kn, bi: (b, i, p)),
        scratch_shapes=[pltpu.VMEM((2, tq, 1), F32), pltpu.VMEM((2, tq, wl), F32), pltpu.VMEM((2, tq, 1), F32),
                        pltpu.VMEM((2, tq, tk), F32)],
    )
    return pl.pallas_call(
        functools.partial(_fox_prompt_kernel, tq=tq, tk=tk, nk=nk, n_heads=nh),
        grid_spec=grid_spec,
        out_shape=jax.ShapeDtypeStruct((bsz, tp, nh * HEAD_DIM), F32),
        compiler_params=_cparams(("arbitrary", "arbitrary", "arbitrary")),
        name="fox_prompt",
    )(kn_tab, bi_tab, qh, qh, kh, kh, vh, vh)


def _fox_sample_kernel(q_ref, kn_ref, vn_ref, ck_ref, cv_ref, bc_ref, bn_ref, o_ref, *, n_heads):
    tq = q_ref.shape[1]
    r = lax.broadcasted_iota(jnp.int32, (tq, tq), 0)
    c = lax.broadcasted_iota(jnp.int32, (tq, tq), 1)
    outs = []
    dn = (((1,), (1,)), ((), ()))
    for hh in range(n_heads):
        sl = slice(hh * HEAD_DIM, (hh + 1) * HEAD_DIM)
        q = q_ref[hh][:, :HEAD_DIM]
        kn = kn_ref[hh][:, :HEAD_DIM]
        vn = vn_ref[hh][:, :HEAD_DIM]
        kc = ck_ref[:, sl].astype(BF16)
        vc = cv_ref[:, sl].astype(BF16)
        g_c = lax.dot_general(q, kc, dn, preferred_element_type=F32) + bc_ref[hh]
        g_n = lax.dot_general(q, kn, dn, preferred_element_type=F32) + bn_ref[hh]
        g_n = jnp.where(c <= r, g_n, -jnp.inf)
        m = jnp.maximum(jnp.max(g_c, axis=-1, keepdims=True), jnp.max(g_n, axis=-1, keepdims=True))
        p_c = jnp.exp2(g_c - m)
        p_n = jnp.exp2(g_n - m)
        den = jnp.sum(p_c, axis=-1, keepdims=True) + jnp.sum(p_n, axis=-1, keepdims=True)
        num = jnp.dot(p_c.astype(BF16), vc, preferred_element_type=F32)
        num = num + jnp.dot(p_n.astype(BF16), vn, preferred_element_type=F32)
        outs.append(num / den)
    o_ref[...] = jnp.concatenate(outs, axis=-1)


def _fox_sample(qh, kh, vh, cache_k, cache_v, bias_cache, bias_new, *, tq):
    _, nh, _, wl = qh.shape
    bsz, past, _ = cache_k.shape
    w = nh * HEAD_DIM
    b4 = lambda b: (b, 0, 0, 0)
    b3 = lambda b: (b, 0, 0)
    rows = lambda b: (0, 0, b, 0)
    return pl.pallas_call(
        functools.partial(_fox_sample_kernel, n_heads=nh),
        grid=(bsz,),
        in_specs=[pl.BlockSpec((None, nh, tq, wl), rows), pl.BlockSpec((None, nh, tq, wl), rows),
                  pl.BlockSpec((None, nh, tq, wl), rows),
                  pl.BlockSpec((None, past, w), b3), pl.BlockSpec((None, past, w), b3),
                  pl.BlockSpec((None, nh, 1, past), b4), pl.BlockSpec((None, nh, 1, tq), b4)],
        out_specs=pl.BlockSpec((None, tq, w), b3),
        out_shape=jax.ShapeDtypeStruct((bsz, tq, w), F32),
        compiler_params=_cparams(("arbitrary",)),
        name="fox_sample",
    )(qh, kh, vh, cache_k, cache_v, bias_cache, bias_new)


def _cumsum_kernel(x_ref, o_ref, carry_ref, *, tm):
    i = pl.program_id(1)

    @pl.when(i == 0)
    def _():
        carry_ref[...] = jnp.zeros_like(carry_ref)

    fc = _prefix_sum_rows(x_ref[...]) + carry_ref[0:1, :]
    o_ref[...] = fc
    carry_ref[0:1, :] = fc[tm - 1:tm, :]


def _cumsum_time(x, *, tm):
    bsz, t, w = x.shape
    row = lambda b, i: (b, i, 0)
    return pl.pallas_call(
        functools.partial(_cumsum_kernel, tm=tm),
        grid=(bsz, t // tm),
        in_specs=[pl.BlockSpec((None, tm, w), row)],
        out_specs=pl.BlockSpec((None, tm, w), row),
        out_shape=jax.ShapeDtypeStruct((bsz, t, w), F32),
        scratch_shapes=[pltpu.VMEM((8, w), F32)],
        compiler_params=_cparams(("arbitrary", "arbitrary")),
        name="cumsum_time",
    )(x)


def _merge_router_kernel(*refs, t_valid, tm, n_meta):
    n_x = 3 if n_meta else 1
    (ys_ref, ya_ref, gs_ref, ga_ref, wo_ref, gf_ref, wr_ref, br_ref, cnt_in_ref,
     x2_ref, ti_ref, tw_ref, tp_ref, cnt_out_ref, cnt_ref) = refs[n_x:]
    first = (pl.program_id(0) == 0) & (pl.program_id(1) == 0)

    @pl.when(first)
    def _():
        cnt_ref[...] = jnp.broadcast_to(cnt_in_ref[...], cnt_ref.shape)

    mix = jnp.concatenate([_rms(ys_ref[...], gs_ref[...]), _rms(ya_ref[...], ga_ref[...])], axis=-1)
    x = _timeline_rows(refs[:n_x], pl.program_id(1), n_meta)
    x2 = x + jnp.dot(mix.astype(BF16), wo_ref[...], preferred_element_type=F32)
    x2_ref[...] = x2
    h2 = _rms(x2, gf_ref[...])
    h_hi = h2.astype(BF16)
    h_lo = (h2 - h_hi.astype(F32)).astype(BF16)
    logits = (jnp.dot(h_hi, wr_ref[0], preferred_element_type=F32)
              + jnp.dot(h_lo, wr_ref[0], preferred_element_type=F32)
              + jnp.dot(h_hi, wr_ref[1], preferred_element_type=F32)) + br_ref[...]

    lane = lax.broadcasted_iota(jnp.int32, (tm, LANES), 1)
    rows = pl.program_id(1) * tm + lax.broadcasted_iota(jnp.int32, (tm, 1), 0)
    valid = rows < t_valid
    vals = logits
    top_v, top_i, sels = [], [], []
    for _ in range(TOP_K):
        mx = jnp.max(vals, axis=-1, keepdims=True)
        idx = jnp.min(jnp.where(vals == mx, lane, LANES), axis=-1, keepdims=True)
        sel = lane == idx
        vals = jnp.where(sel, -jnp.inf, vals)
        top_v.append(mx)
        top_i.append(idx)
        sels.append(sel)
    ex = [jnp.exp(v - top_v[0]) for v in top_v]
    den = ex[0] + ex[1] + ex[2] + ex[3]

    member = jnp.zeros((tm, LANES), F32)
    for sel in sels:
        member = member + jnp.where(valid, jnp.where(sel, 1.0, 0.0), 0.0)
    r = lax.broadcasted_iota(jnp.int32, (tm, tm), 0)
    c = lax.broadcasted_iota(jnp.int32, (tm, tm), 1)
    tri = (c < r).astype(BF16)
    before = jnp.dot(tri, member.astype(BF16), preferred_element_type=F32) + cnt_ref[0:1, :]
    cnt_new = cnt_ref[0:1, :] + jnp.sum(member, axis=0, keepdims=True)
    cnt_ref[0:1, :] = cnt_new
    cnt_out_ref[...] = cnt_new

    ti = jnp.zeros((tm, LANES), jnp.int32)
    tw = jnp.zeros((tm, LANES), F32)
    tpos = jnp.zeros((tm, LANES), F32)
    for kk in range(TOP_K):
        pos = jnp.sum(jnp.where(sels[kk], before, 0.0), axis=-1, keepdims=True)
        ti = jnp.where(lane == kk, top_i[kk], ti)
        tw = jnp.where(lane == kk, ex[kk] / den, tw)
        tpos = jnp.where(lane == kk, pos, tpos)
    ti_ref[...] = ti
    tw_ref[...] = tw
    tp_ref[...] = tpos.astype(jnp.int32)


def _merge_router(x, ys, ya, g_ssm, g_attn, w_out, g_ffn, w_router, b_router, cnt_in, *, tm, meta=None):
    bsz, _, d = x.shape
    x_ops, x_specs, t = _timeline_specs(x, meta, tm)
    w = ys.shape[-1]
    nt = pl.cdiv(t, tm)
    row = lambda b, i: (b, i, 0)
    c2 = lambda b, i: (0, 0)
    return pl.pallas_call(
        functools.partial(_merge_router_kernel, t_valid=t, tm=tm, n_meta=0 if meta is None else meta.shape[0]),
        grid=(bsz, nt),
        in_specs=x_specs + [
            pl.BlockSpec((None, tm, w), row),
            pl.BlockSpec((None, tm, w), row),
            pl.BlockSpec((1, w), c2), pl.BlockSpec((1, w), c2),
            pl.BlockSpec(w_out.shape, c2),
            pl.BlockSpec((1, d), c2),
            pl.BlockSpec(w_router.shape, lambda b, i: (0, 0, 0)),
            pl.BlockSpec((1, LANES), c2),
            pl.BlockSpec((1, LANES), c2),
        ],
        out_specs=[
            pl.BlockSpec((None, tm, d), row),
            pl.BlockSpec((None, tm, LANES), row),
            pl.BlockSpec((None, tm, LANES), row),
            pl.BlockSpec((None, tm, LANES), row),
            pl.BlockSpec((1, LANES), c2),
        ],
        out_shape=[
            jax.ShapeDtypeStruct((bsz, t, d), F32),
            jax.ShapeDtypeStruct((bsz, t, LANES), jnp.int32),
            jax.ShapeDtypeStruct((bsz, t, LANES), F32),
            jax.ShapeDtypeStruct((bsz, t, LANES), jnp.int32),
            jax.ShapeDtypeStruct((1, LANES), F32),
        ],
        scratch_shapes=[pltpu.VMEM((8, LANES), F32)],
        compiler_params=_cparams(("arbitrary", "arbitrary")),
        name="merge_router",
    )(*x_ops, ys, ya, g_ssm, g_attn, w_out, g_ffn, w_router, b_router, cnt_in)


def _dispatch_kernel(last_tile_ref, nu_ref, slots_ref, xa_ref, xb_ref, g_ref, xs_hbm, hbuf_ref, zero_ref, zsem, sem,
                     *, t_valid, nt_a, n_steps_a, tm, n_experts, n_tiles, tile):
    step = pl.program_id(0)

    def zero_tile(t):
        return pltpu.make_async_copy(zero_ref, xs_hbm.at[pl.ds(pl.multiple_of(t * tile, tile), tile)], zsem)

    @pl.when(step == 0)
    def _():
        zero_ref[...] = jnp.zeros_like(zero_ref)

        def start_tail(t, c):
            zero_tile(t).start()
            return c

        def wait_tail(t, c):
            zero_tile(t).wait()
            return c

        lax.fori_loop(nu_ref[0], n_tiles, start_tail, 0)
        lax.fori_loop(nu_ref[0], n_tiles, wait_tail, 0)
        for e in range(n_experts):
            lt = last_tile_ref[e]

            @pl.when(lt >= 0)
            def _():
                zero_tile(lt).start()
        for e in range(n_experts):
            lt = last_tile_ref[e]

            @pl.when(lt >= 0)
            def _():
                zero_tile(lt).wait()

    in_a = step < n_steps_a

    @pl.when(in_a)
    def _():
        hbuf_ref[...] = _rms(xa_ref[...], g_ref[...])

    @pl.when(jnp.logical_not(in_a))
    def _():
        hbuf_ref[...] = _rms(xb_ref[...], g_ref[...])

    def row_copy(r, kk):
        return pltpu.make_async_copy(hbuf_ref.at[pl.ds(r, 1)], xs_hbm.at[pl.ds(slots_ref[r * TOP_K + kk], 1)], sem)

    def issue(r, c):
        for kk in range(TOP_K):
            row_copy(r, kk).start()
        return c

    def drain(r, c):
        for kk in range(TOP_K):
            row_copy(r, kk).wait()
        return c

    n_valid = jnp.where(in_a, jnp.minimum(tm, t_valid - (step % nt_a) * tm), tm)

    @pl.when(n_valid == tm)
    def _():
        lax.fori_loop(0, tm, issue, 0, unroll=ISSUE_UNROLL)
        for _ in range(TOP_K):
            pltpu.make_async_copy(hbuf_ref, xs_hbm.at[pl.ds(0, tm)], sem).wait()

    @pl.when(n_valid < tm)
    def _():
        lax.fori_loop(0, n_valid, issue, 0)
        lax.fori_loop(0, n_valid, drain, 0)


def _dispatch(xa, xb, g_ffn, slots_flat, last_tile, n_used, *, n_slots, tm):
    bsz, t, d = xa.shape
    nt_a = pl.cdiv(t, tm)
    n_steps_a = bsz * nt_a
    n_steps_b = xb.shape[0] // tm
    n_experts = last_tile.shape[0]
    grid_spec = pltpu.PrefetchScalarGridSpec(
        num_scalar_prefetch=2,
        grid=(n_steps_a + n_steps_b,),
        in_specs=[
            pl.BlockSpec((tm * TOP_K,), lambda s, lt, nu: (s,), memory_space=pltpu.SMEM),
            pl.BlockSpec((None, tm, d), lambda s, lt, nu: (jnp.minimum(s, n_steps_a - 1) // nt_a,
                                                          jnp.minimum(s, n_steps_a - 1) % nt_a, 0)),
            pl.BlockSpec((tm, d), lambda s, lt, nu: (jnp.maximum(s - n_steps_a, 0), 0)),
            pl.BlockSpec((1, d), lambda s, lt, nu: (0, 0)),
        ],
        out_specs=pl.BlockSpec(memory_space=pl.ANY),
        scratch_shapes=[pltpu.VMEM((tm, d), F32), pltpu.VMEM((EXPERT_TILE, d), F32),
                        pltpu.SemaphoreType.DMA, pltpu.SemaphoreType.DMA],
    )
    return pl.pallas_call(
        functools.partial(_dispatch_kernel, t_valid=t, nt_a=nt_a, n_steps_a=n_steps_a, tm=tm, n_experts=n_experts,
                          n_tiles=n_slots // EXPERT_TILE, tile=EXPERT_TILE),
        grid_spec=grid_spec,
        out_shape=jax.ShapeDtypeStruct((n_slots, d), F32),
        compiler_params=_cparams(("arbitrary",)),
        name="moe_dispatch",
    )(last_tile, n_used, slots_flat, xa, xb, g_ffn)


def _expert_kernel(te_ref, nu_ref, tpe_ref, x_ref, b1_ref, b2_ref, w1_hbm, w2_hbm, y_ref,
                   w1f_ref, w2f_ref, w1b_ref, w2b_ref, grp_ref, sem1, sem2, *, d_ff):
    i = pl.program_id(0)
    used = nu_ref[0]
    e = te_ref[i]
    prev = te_ref[jnp.maximum(i - 1, 0)]
    changed = (i == 0) | (e != prev)

    def fetch(expert, slot):
        return (pltpu.make_async_copy(w1_hbm.at[expert], w1f_ref.at[slot], sem1.at[slot]),
                pltpu.make_async_copy(w2_hbm.at[expert], w2f_ref.at[slot], sem2.at[slot]))

    @pl.when(i == 0)
    def _():
        grp_ref[0] = 0
        for cp in fetch(e, 0):
            cp.start()

    @pl.when((i < used) & changed)
    def _():
        slot = grp_ref[0] % 2
        for cp in fetch(e, slot):
            cp.wait()
        w1b_ref[...] = w1f_ref[slot].astype(BF16)
        w2b_ref[...] = w2f_ref[slot].astype(BF16)
        nxt = i + tpe_ref[e]

        @pl.when(nxt < used)
        def _():
            for cp in fetch(te_ref[nxt], 1 - slot):
                cp.start()
        grp_ref[0] = grp_ref[0] + 1

    @pl.when(i >= used)
    def _():
        y_ref[...] = jnp.zeros_like(y_ref)

    @pl.when(i < used)
    def _():
        x = x_ref[...].astype(BF16)
        h = jnp.dot(x, w1b_ref[...], preferred_element_type=F32) + b1_ref[...]
        hg = jnp.minimum(h[:, :d_ff], SWIGLU_LIMIT)
        hl = jnp.clip(h[:, d_ff:], -SWIGLU_LIMIT, SWIGLU_LIMIT)
        act = (hl + 1.0) * (hg * jax.nn.sigmoid(SWIGLU_ALPHA * hg))
        y_ref[...] = jnp.dot(act.astype(BF16), w2b_ref[...], preferred_element_type=F32) + b2_ref[...]


def _experts(xs, tile_expert, n_used, tiles_per, w1, b1, w2, b2):
    s, d = xs.shape
    n_tiles = s // EXPERT_TILE
    n_e, _, two_ff = w1.shape
    d_ff = two_ff // 2

    def xmap(i, te, nu, tpe):
        return (i, 0)

    def bmap(i, te, nu, tpe):
        return (te[i], 0, 0)

    grid_spec = pltpu.PrefetchScalarGridSpec(
        num_scalar_prefetch=3,
        grid=(n_tiles,),
        in_specs=[
            pl.BlockSpec((EXPERT_TILE, d), xmap),
            pl.BlockSpec((None, 1, two_ff), bmap),
            pl.BlockSpec((None, 1, d), bmap),
            pl.BlockSpec(memory_space=pl.ANY),
            pl.BlockSpec(memory_space=pl.ANY),
        ],
        out_specs=pl.BlockSpec((EXPERT_TILE, d), xmap),
        scratch_shapes=[pltpu.VMEM((2, d, two_ff), F32), pltpu.VMEM((2, d_ff, d), F32),
                        pltpu.VMEM((d, two_ff), BF16), pltpu.VMEM((d_ff, d), BF16),
                        pltpu.SMEM((1,), jnp.int32),
                        pltpu.SemaphoreType.DMA((2,)), pltpu.SemaphoreType.DMA((2,))],
    )
    return pl.pallas_call(
        functools.partial(_expert_kernel, d_ff=d_ff),
        grid_spec=grid_spec,
        out_shape=jax.ShapeDtypeStruct((s, d), F32),
        compiler_params=_cparams(("arbitrary",)),
        name="moe_experts",
    )(tile_expert, n_used, tiles_per, xs, b1.reshape(n_e, 1, two_ff), b2.reshape(n_e, 1, d), w1, w2)


def _combine_kernel(slots_ref, g_ref, x2_hbm, tw_hbm, ys_hbm, o_ref, xbuf_ref, wbuf_ref, buf_ref, sem, lsem,
                    *, tm, row_offset):
    b = pl.program_id(0)
    start = pl.multiple_of(row_offset + pl.program_id(1) * tm, 8)
    x_copy = pltpu.make_async_copy(x2_hbm.at[b, pl.ds(start, tm)], xbuf_ref, lsem)
    w_copy = pltpu.make_async_copy(tw_hbm.at[b, pl.ds(start, tm)], wbuf_ref, lsem)
    x_copy.start()
    w_copy.start()

    def issue(r, c):
        for kk in range(TOP_K):
            pltpu.make_async_copy(ys_hbm.at[pl.ds(slots_ref[r * TOP_K + kk], 1)],
                                  buf_ref.at[kk, pl.ds(r, 1)], sem).start()
        return c

    lax.fori_loop(0, tm, issue, 0, unroll=ISSUE_UNROLL)
    x_copy.wait()
    w_copy.wait()
    for kk in range(TOP_K):
        pltpu.make_async_copy(ys_hbm.at[pl.ds(0, tm)], buf_ref.at[kk], sem).wait()
    tw = wbuf_ref[...]
    acc = xbuf_ref[...]
    for kk in range(TOP_K):
        acc = acc + tw[:, kk:kk + 1] * buf_ref[kk]
    o_ref[...] = _rms(acc, g_ref[...])


def _combine(x2, tw, slots_flat, ys, g_final, *, tm, row_offset):
    bsz, t, d = x2.shape
    t_out = t - row_offset
    nt = t_out // tm
    grid_spec = pltpu.PrefetchScalarGridSpec(
        num_scalar_prefetch=0,
        grid=(bsz, nt),
        in_specs=[
            pl.BlockSpec((tm * TOP_K,), lambda b, i: (b * nt + i,), memory_space=pltpu.SMEM),
            pl.BlockSpec((1, d), lambda b, i: (0, 0)),
            pl.BlockSpec(memory_space=pl.ANY),
            pl.BlockSpec(memory_space=pl.ANY),
            pl.BlockSpec(memory_space=pl.ANY),
        ],
        out_specs=pl.BlockSpec((None, tm, d), lambda b, i: (b, i, 0)),
        scratch_shapes=[pltpu.VMEM((tm, d), F32), pltpu.VMEM((tm, LANES), F32),
                        pltpu.VMEM((TOP_K, tm, d), F32), pltpu.SemaphoreType.DMA, pltpu.SemaphoreType.DMA],
    )
    return pl.pallas_call(
        functools.partial(_combine_kernel, tm=tm, row_offset=row_offset),
        grid_spec=grid_spec,
        out_shape=jax.ShapeDtypeStruct((bsz, t_out, d), F32),
        compiler_params=_cparams(("arbitrary", "arbitrary")),
        name="moe_combine",
    )(slots_flat, g_final, x2, tw, ys)


def _pad_lanes(a, value=0.0):
    return jnp.pad(a, [(0, 0)] * (a.ndim - 1) + [(0, LANES - a.shape[-1])], constant_values=value)


def kernel(x_prompt, x_sample, cache_k, cache_v, cache_logf, state_ssm_re, state_ssm_im, meta_tokens, norm_mix_g, w_in, b_forget, ssm_a_re, ssm_a_im, ssm_log_dt, ssm_b_re, ssm_b_im, ssm_c_re, ssm_c_im, ssm_d, w_glu, b_glu, g_out_ssm, g_out_attn, w_out, norm_ffn_g, w_router, b_router, w_mlp1, b_mlp1, w_mlp2, b_mlp2, norm_final_g):
    depth = w_in.shape[0]
    assert depth == 1, "the routing tables below are built for a single trunk layer"
    l = 0
    bp, seq, d = x_prompt.shape
    bs, ts, _ = x_sample.shape
    n_heads = b_forget.shape[1]
    n_groups, ssm_n = ssm_a_re.shape[1:]
    n_experts = w_router.shape[2]
    w_ssm = n_groups * SSM_P
    w_attn = n_heads * HEAD_DIM
    gn = n_groups * ssm_n
    past = cache_k.shape[2]
    assert seq % ROW_TILE == 0 and (bs * ts) % ROW_TILE == 0

    meta = meta_tokens.astype(x_prompt.dtype)
    assert meta.shape[0] == N_META
    xs = x_sample
    tp = seq + N_META

    w_main = w_in[l][:, :w_ssm + 3 * w_attn].astype(BF16)
    w_f = _pad_lanes(w_in[l][:, w_ssm + 3 * w_attn:]).astype(BF16)
    b_f = _pad_lanes(b_forget[l][None, :])
    g_mix = norm_mix_g[l][None, :]
    ab_re, ab_im, bb_re, bb_im = _s5_params(ssm_a_re[l], ssm_a_im[l], ssm_log_dt[l], ssm_b_re[l], ssm_b_im[l])
    b_mat, c_mat = _s5_block_mats(bb_re, bb_im, ssm_c_re[l], ssm_c_im[l], n_groups)

    proj = functools.partial(_in_proj, n_heads=n_heads, w_ssm=w_ssm, w_attn=w_attn)
    up, kp, vp, lfp, qhp, khp, vhp, knp, bip = proj(x_prompt, g_mix, w_main, w_f, b_f, tm=ROW_TILE, meta=meta)
    n_s = bs * ts
    us, ks, vs, lfs, qhs, khs, vhs, _, _ = proj(xs.reshape(1, n_s, d), g_mix, w_main, w_f, b_f, tm=n_s)
    us = us.reshape(bs, ts, w_ssm)
    lfs = lfs.reshape(bs, ts, n_heads)

    s5 = functools.partial(_s5_mixer, ab_re=ab_re, ab_im=ab_im, b_mat=b_mat, c_mat=c_mat,
                           d_skip=ssm_d[l].reshape(1, w_ssm), w_glu=w_glu[l].astype(BF16), b_glu=b_glu[l][None, :])
    zeros_state = jnp.zeros((bp, 1, gn), F32)
    ysp, hrp, hip = s5(up, zeros_state, zeros_state, tm=ROW_TILE, nb=bp)
    yss, hrs, his = s5(us, state_ssm_re[l].reshape(bs, 1, gn), state_ssm_im[l].reshape(bs, 1, gn), tm=ts, nb=1)

    yap = _fox_prompt(qhp, khp, vhp, knp, bip, tq=ATTN_Q_TILE)
    t_all = past + ts
    assert bs * n_heads == LANES
    tm_cum = max(m for m in range(8, ROW_TILE + 1, 8) if t_all % m == 0)
    lf_all = jnp.concatenate([cache_logf[l].astype(F32), lfs], axis=1)
    fcs = _cumsum_time(jnp.transpose(lf_all, (1, 0, 2)).reshape(1, t_all, LANES), tm=tm_cum)
    bias_s = -LOG2E * jnp.transpose(fcs.reshape(t_all, bs, n_heads), (1, 2, 0))[:, :, None, :]
    yas = _fox_sample(qhs, khs, vhs, cache_k[l].reshape(bs, past, w_attn), cache_v[l].reshape(bs, past, w_attn),
                      bias_s[..., :past], bias_s[..., past:], tq=ts)

    wr = _pad_lanes(w_router[l])
    wr_hi = wr.astype(BF16)
    wr_parts = jnp.stack([wr_hi, (wr - wr_hi.astype(F32)).astype(BF16)])
    merge = functools.partial(_merge_router, g_ssm=g_out_ssm[l][None, :], g_attn=g_out_attn[l][None, :],
                              w_out=w_out[l].astype(BF16), g_ffn=norm_ffn_g[l][None, :],
                              w_router=wr_parts, b_router=_pad_lanes(b_router[l][None, :], value=-1e30))
    x2p, tip, twp, tpp, cnt_p = merge(x_prompt, ysp, yap, cnt_in=jnp.zeros((1, LANES), F32), tm=ROW_TILE, meta=meta)
    x2s, tis, tws, tps, cnt = merge(xs.reshape(1, n_s, d), yss.reshape(1, n_s, w_ssm), yas.reshape(1, n_s, w_attn),
                                    cnt_in=cnt_p, tm=n_s)

    n_tok = bp * tp + bs * ts
    counts = cnt[0, :n_experts].astype(jnp.int32)
    tiles_per = (counts + EXPERT_TILE - 1) // EXPERT_TILE
    tile_end = jnp.cumsum(tiles_per)
    tile_start = tile_end - tiles_per
    n_tiles = (n_tok * TOP_K) // EXPERT_TILE + n_experts
    n_slots = n_tiles * EXPERT_TILE
    n_used = tile_end[-1:].astype(jnp.int32)
    tile_ids = jnp.minimum(jnp.arange(n_tiles, dtype=jnp.int32), n_used - 1)
    tile_expert = jnp.sum((tile_end[None, :] <= tile_ids[:, None]).astype(jnp.int32), axis=1)
    last_tile = jnp.where(tiles_per > 0, tile_end - 1, -1).astype(jnp.int32)
    slot_base = (tile_start * EXPERT_TILE).astype(F32)

    def slots_of(ids, pos):
        onehot = jax.nn.one_hot(ids[..., :TOP_K], n_experts, dtype=F32)
        base = jnp.einsum('btke,e->btk', onehot, slot_base, precision=lax.Precision.HIGHEST)
        return base.astype(jnp.int32) + pos[..., :TOP_K]

    slots_p = slots_of(tip, tpp)
    slots_s = slots_of(tis, tps)
    tpad = pl.cdiv(tp, ROW_TILE) * ROW_TILE
    slots_disp = jnp.concatenate([jnp.pad(slots_p, ((0, 0), (0, tpad - tp), (0, 0))).reshape(-1),
                                  slots_s.reshape(-1)])
    xs_sorted = _dispatch(x2p, x2s.reshape(bs * ts, d), norm_ffn_g[l][None, :], slots_disp, last_tile, n_used,
                          n_slots=n_slots, tm=ROW_TILE)
    ys_sorted = _experts(xs_sorted, tile_expert, n_used, tiles_per.astype(jnp.int32),
                         w_mlp1[l], b_mlp1[l], w_mlp2[l], b_mlp2[l])

    g_fin = norm_final_g[None, :]
    y_prompt = _combine(x2p, twp, slots_p[:, N_META:].reshape(-1), ys_sorted, g_fin, tm=ROW_TILE, row_offset=N_META)
    y_sample = _combine(x2s.reshape(1, bs * ts, d), tws.reshape(1, bs * ts, LANES), slots_s.reshape(-1), ys_sorted,
                        g_fin, tm=ROW_TILE, row_offset=0).reshape(bs, ts, d)

    hd = HEAD_DIM
    st = lambda a, b: a.reshape(1, b, n_groups, ssm_n)
    return (y_prompt, y_sample,
            kp.reshape(1, bp, tp, n_heads, hd), vp.reshape(1, bp, tp, n_heads, hd), lfp[None],
            st(hrp, bp), st(hip, bp),
            ks.reshape(1, bs, ts, n_heads, hd), vs.reshape(1, bs, ts, n_heads, hd), lfs[None],
            st(hrs, bs), st(his, bs))
```

```python
import functools
import math

import numpy as np
import jax
import jax.numpy as jnp
from jax import lax
from jax.experimental import pallas as pl
from jax.experimental.pallas import tpu as pltpu

F32 = jnp.float32
BF16 = jnp.bfloat16

LANES = 128
VMEM_LIMIT_BYTES = 56 * 1024 * 1024

N_META = 16
HEAD_DIM = 64
SSM_P = 16
SSM_N = 64
TOP_K = 4
SWIGLU_LIMIT = 7.0
SWIGLU_ALPHA = 1.702
RMS_EPS = 1e-6
LAMBDA_RE_MAX = -1e-4
LOG2E = math.log2(math.e)

ROW_TILE = 512
ATTN_Q_TILE = 512
EXPERT_TILE = 512
GROUP_SLAB = 8
BIAS_TERMS = 3
ISSUE_UNROLL = 8
UNDERFLOW_LOG2 = 152.0
NORM_SLACK = 1.001


def _cparams(sem):
    return pltpu.CompilerParams(dimension_semantics=sem, vmem_limit_bytes=VMEM_LIMIT_BYTES)


def _rms(v, g):
    return v * lax.rsqrt(jnp.mean(v * v, axis=-1, keepdims=True) + RMS_EPS) * g


def _timeline_rows(x_refs, i, n_meta):
    if not n_meta:
        return x_refs[0][...]
    meta_ref, prev_ref, cur_ref = x_refs
    head = jnp.where(i == 0, meta_ref[...], prev_ref[...])
    return jnp.concatenate([head, cur_ref[...][:cur_ref.shape[0] - n_meta]], axis=0)


def _timeline_specs(x, meta, tm):
    bsz, t_in, d = x.shape
    if meta is None:
        return [x], [pl.BlockSpec((None, tm, d), lambda b, i: (b, i, 0))], t_in
    n_meta = meta.shape[0]
    assert t_in % tm == 0 and tm % n_meta == 0
    per = tm // n_meta
    last = t_in // tm - 1
    specs = [pl.BlockSpec((n_meta, d), lambda b, i: (0, 0)),
             pl.BlockSpec((None, n_meta, d), lambda b, i: (b, jnp.maximum(i * per - 1, 0), 0)),
             pl.BlockSpec((None, tm, d), lambda b, i: (b, jnp.minimum(i, last), 0))]
    return [meta, x, x], specs, t_in + n_meta


def _split_bf16(a):
    pieces = []
    rest = a
    for _ in range(BIAS_TERMS):
        piece = rest.astype(BF16)
        pieces.append(piece)
        rest = rest - piece.astype(F32)
    return jnp.concatenate(pieces, axis=-1)


def _prefix_sum_rows(x):
    tm = x.shape[0]
    r = lax.broadcasted_iota(jnp.int32, (tm, tm), 0)
    c = lax.broadcasted_iota(jnp.int32, (tm, tm), 1)
    tri = jnp.where(c <= r, 1.0, 0.0).astype(BF16)
    s = jnp.dot(tri, _split_bf16(x), preferred_element_type=F32)
    return sum(s[:, t * LANES:(t + 1) * LANES] for t in range(BIAS_TERMS))


def _in_proj_kernel(*refs, t_valid, tm, n_heads, w_ssm, w_attn, n_meta):
    n_x = 3 if n_meta else 1
    (g_ref, w_ref, wf_ref, bf_ref, sel_ref,
     u_ref, k_ref, v_ref, lf_ref, qh_ref, kh_ref, vh_ref, kn_ref, bi_ref, carry_ref) = refs[n_x:]
    i = pl.program_id(1)

    @pl.when(i == 0)
    def _():
        carry_ref[...] = jnp.zeros_like(carry_ref)

    x = _timeline_rows(refs[:n_x], i, n_meta)
    h = _rms(x, g_ref[...]).astype(BF16)
    rows = i * tm + lax.broadcasted_iota(jnp.int32, (tm, 1), 0)
    valid = rows < t_valid
    z = jnp.where(valid, jnp.dot(h, w_ref[...], preferred_element_type=F32), 0.0)
    u_ref[...] = z[:, :w_ssm]
    k_ref[...] = z[:, w_ssm + w_attn:w_ssm + 2 * w_attn]
    v_ref[...] = z[:, w_ssm + 2 * w_attn:w_ssm + 3 * w_attn]

    zf = jnp.dot(h, wf_ref[...], preferred_element_type=F32) + bf_ref[...]
    lf = jnp.where(valid, jnp.minimum(zf, 0.0) - jnp.log1p(jnp.exp(-jnp.abs(zf))), 0.0)
    lf_ref[...] = lf[:, :n_heads]
    fc = _prefix_sum_rows(lf) + carry_ref[0:1, :]
    carry_ref[0:1, :] = fc[tm - 1:tm, :]

    bias = -LOG2E * fc
    bi_ref[...] = jnp.max(bias, axis=0, keepdims=True)
    tails = jnp.dot(_split_bf16(bias), sel_ref[...], preferred_element_type=F32)

    lane = lax.broadcasted_iota(jnp.int32, (tm, LANES), 1)
    lane1 = lax.broadcasted_iota(jnp.int32, (1, LANES), 1)
    low = lane < HEAD_DIM
    in_tail = (lane >= HEAD_DIM) & (lane < HEAD_DIM + BIAS_TERMS)
    q_tail = jnp.where(in_tail, 1.0, 0.0)
    v_tail = jnp.where(lane == HEAD_DIM, 1.0, 0.0)
    kn = jnp.zeros((1, LANES), F32)
    for pair in range(n_heads // 2):
        sl = slice(pair * LANES, (pair + 1) * LANES)
        qt = z[:, w_ssm:w_ssm + w_attn][:, sl] * (LOG2E * HEAD_DIM ** -0.5)
        kt = z[:, w_ssm + w_attn:w_ssm + 2 * w_attn][:, sl]
        vt = z[:, w_ssm + 2 * w_attn:w_ssm + 3 * w_attn][:, sl]
        for half in range(2):
            hh = 2 * pair + half
            if half:
                qt, kt, vt = (pltpu.roll(a, HEAD_DIM, 1) for a in (qt, kt, vt))
            k_tail = jnp.where(in_tail, pltpu.roll(tails, HEAD_DIM - BIAS_TERMS * hh, 1), 0.0)
            kb = jnp.where(low, kt, k_tail).astype(BF16)
            qh_ref[hh] = jnp.where(low, qt, q_tail).astype(BF16)
            kh_ref[hh] = kb
            vh_ref[hh] = jnp.where(low, vt, v_tail).astype(BF16)
            kf = jnp.where(low, kb.astype(F32), 0.0)
            ksq = jnp.max(jnp.sum(kf * kf, axis=-1, keepdims=True), axis=0, keepdims=True)
            kn = jnp.where(lane1 == hh, ksq, kn)
    kn_ref[...] = kn


def _bias_selector(n_heads):
    sel = np.zeros((BIAS_TERMS * LANES, LANES), np.float32)
    for hh in range(n_heads):
        for c in range(BIAS_TERMS):
            sel[c * LANES + hh, BIAS_TERMS * hh + c] = 1.0
    return jnp.asarray(sel, BF16)


def _in_proj(x, g, w_main, w_f, b_f, *, n_heads, w_ssm, w_attn, tm, meta=None):
    bsz, _, d = x.shape
    x_ops, x_specs, t = _timeline_specs(x, meta, tm)
    nt = pl.cdiv(t, tm)
    tp = nt * tm
    row = lambda b, i: (b, i, 0)
    const = lambda b, i: (0, 0)
    head = lambda b, i: (b, 0, i, 0)
    tile = lambda b, i: (b, i, 0, 0)
    sel = _bias_selector(n_heads)
    kern = functools.partial(_in_proj_kernel, t_valid=t, tm=tm, n_heads=n_heads, w_ssm=w_ssm, w_attn=w_attn,
                             n_meta=0 if meta is None else meta.shape[0])
    return pl.pallas_call(
        kern,
        grid=(bsz, nt),
        in_specs=x_specs + [
            pl.BlockSpec((1, d), const),
            pl.BlockSpec(w_main.shape, const),
            pl.BlockSpec(w_f.shape, const),
            pl.BlockSpec((1, LANES), const),
            pl.BlockSpec(sel.shape, const),
        ],
        out_specs=[
            pl.BlockSpec((None, tm, w_ssm), row),
            pl.BlockSpec((None, tm, w_attn), row),
            pl.BlockSpec((None, tm, w_attn), row),
            pl.BlockSpec((None, tm, n_heads), row),
            pl.BlockSpec((None, n_heads, tm, LANES), head),
            pl.BlockSpec((None, n_heads, tm, LANES), head),
            pl.BlockSpec((None, n_heads, tm, LANES), head),
            pl.BlockSpec((None, None, 1, LANES), tile),
            pl.BlockSpec((None, None, 1, LANES), tile),
        ],
        out_shape=[
            jax.ShapeDtypeStruct((bsz, t, w_ssm), F32),
            jax.ShapeDtypeStruct((bsz, t, w_attn), F32),
            jax.ShapeDtypeStruct((bsz, t, w_attn), F32),
            jax.ShapeDtypeStruct((bsz, t, n_heads), F32),
            jax.ShapeDtypeStruct((bsz, n_heads, tp, LANES), BF16),
            jax.ShapeDtypeStruct((bsz, n_heads, tp, LANES), BF16),
            jax.ShapeDtypeStruct((bsz, n_heads, tp, LANES), BF16),
            jax.ShapeDtypeStruct((bsz, nt, 1, LANES), F32),
            jax.ShapeDtypeStruct((bsz, nt, 1, LANES), F32),
        ],
        scratch_shapes=[pltpu.VMEM((8, LANES), F32)],
        compiler_params=_cparams(("arbitrary", "arbitrary")),
        name="in_proj",
    )(*x_ops, g, w_main, w_f, b_f, sel)


def _s5_param_kernel(are_ref, aim_ref, ldt_ref, bre_ref, bim_ref, abr_ref, abi_ref, bbr_ref, bbi_ref):
    lam_re = jnp.minimum(are_ref[...], LAMBDA_RE_MAX)
    lam_im = aim_ref[...]
    dt = jnp.exp(ldt_ref[...])
    mag = jnp.exp(lam_re * dt)
    ab_re = mag * jnp.cos(lam_im * dt)
    ab_im = mag * jnp.sin(lam_im * dt)
    abr_ref[...] = ab_re
    abi_ref[...] = ab_im
    nr = ab_re - 1.0
    ni = ab_im
    den = lam_re * lam_re + lam_im * lam_im
    cr = (nr * lam_re + ni * lam_im) / den
    ci = (ni * lam_re - nr * lam_im) / den
    b_re = bre_ref[...]
    b_im = bim_ref[...]
    bbr_ref[...] = cr * b_re - ci * b_im
    bbi_ref[...] = cr * b_im + ci * b_re


def _s5_params(a_re, a_im, log_dt, b_re, b_im):
    g, n = a_re.shape
    p = b_re.shape[-1]
    gn = g * n
    flat = lambda a: a.reshape(1, gn)
    ldt = jnp.broadcast_to(log_dt[:, None], (g, n)).reshape(1, gn)
    bt = lambda b: jnp.transpose(b, (2, 0, 1)).reshape(p, gn)
    return pl.pallas_call(
        _s5_param_kernel,
        out_shape=[jax.ShapeDtypeStruct((1, gn), F32), jax.ShapeDtypeStruct((1, gn), F32),
                   jax.ShapeDtypeStruct((p, gn), F32), jax.ShapeDtypeStruct((p, gn), F32)],
        name="s5_params",
    )(flat(a_re), flat(a_im), ldt, bt(b_re), bt(b_im))


def _s5_block_mats(bb_re, bb_im, c_re, c_im, n_groups):
    p = bb_re.shape[0]
    n = bb_re.shape[1] // n_groups
    s = n_groups // GROUP_SLAB
    eye = jnp.eye(GROUP_SLAB, dtype=F32)

    def in_blk(bb):
        b4 = bb.reshape(p, s, GROUP_SLAB, n)
        return jnp.einsum('qsgn,gh->sgqhn', b4, eye)

    b_mat = jnp.concatenate([in_blk(bb_re).reshape(s, GROUP_SLAB * p, GROUP_SLAB * n),
                             in_blk(bb_im).reshape(s, GROUP_SLAB * p, GROUP_SLAB * n)], axis=-1)

    def out_blk(c):
        c4 = c.reshape(s, GROUP_SLAB, p, n)
        return jnp.einsum('sgpn,gh->sgnhp', c4, eye)

    c_mat = jnp.concatenate([out_blk(c_re).reshape(s, GROUP_SLAB * n, GROUP_SLAB * p),
                             out_blk(-c_im).reshape(s, GROUP_SLAB * n, GROUP_SLAB * p)], axis=1)
    return b_mat.astype(BF16), c_mat.astype(BF16)


def _s5_kernel(u_ref, h0r_ref, h0i_ref, abr_ref, abi_ref, bm_ref, cm_ref, d_ref, wg_ref, bg_ref,
               y_ref, hr_out_ref, hi_out_ref, sre_ref, sim_ref, hre_ref, him_ref,
               *, t_valid, tm, nb, n_slabs, slab_in, slab_state):
    i = pl.program_id(1)
    nt = pl.num_programs(1)

    @pl.when(i == 0)
    def _():
        hre_ref[...] = h0r_ref[...]
        him_ref[...] = h0i_ref[...]

    for b in range(nb):
        ub = u_ref[b].astype(BF16)
        for s in range(n_slabs):
            z = jnp.dot(ub[:, s * slab_in:(s + 1) * slab_in], bm_ref[s], preferred_element_type=F32)
            sre_ref[b, :, s * slab_state:(s + 1) * slab_state] = z[:, :slab_state]
            sim_ref[b, :, s * slab_state:(s + 1) * slab_state] = z[:, slab_state:]

    a_re = abr_ref[...]
    a_im = abi_ref[...]
    last_row = (t_valid - 1) % tm

    def step(t, carry):
        out = []
        for b in range(nb):
            h_re, h_im = carry[2 * b], carry[2 * b + 1]
            b_re = sre_ref[b, pl.ds(t, 1), :]
            b_im = sim_ref[b, pl.ds(t, 1), :]
            n_re = a_re * h_re - a_im * h_im + b_re
            n_im = a_re * h_im + a_im * h_re + b_im
            sre_ref[b, pl.ds(t, 1), :] = n_re
            sim_ref[b, pl.ds(t, 1), :] = n_im
            out += [n_re, n_im]
        return tuple(out)

    init = tuple(ref[b] for b in range(nb) for ref in (hre_ref, him_ref))
    fin = lax.fori_loop(0, tm, step, init)
    for b in range(nb):
        hre_ref[b] = fin[2 * b]
        him_ref[b] = fin[2 * b + 1]

    @pl.when(i == nt - 1)
    def _():
        for b in range(nb):
            hr_out_ref[b] = sre_ref[b, last_row:last_row + 1, :]
            hi_out_ref[b] = sim_ref[b, last_row:last_row + 1, :]

    for b in range(nb):
        ys = []
        for s in range(n_slabs):
            sl = slice(s * slab_state, (s + 1) * slab_state)
            cm = cm_ref[s]
            y = jnp.dot(sre_ref[b, :, sl].astype(BF16), cm[:slab_state], preferred_element_type=F32)
            y = y + jnp.dot(sim_ref[b, :, sl].astype(BF16), cm[slab_state:], preferred_element_type=F32)
            ys.append(y)
        y = jnp.concatenate(ys, axis=-1) + d_ref[...] * u_ref[b]
        y = jax.nn.gelu(y)
        gate = jnp.dot(y.astype(BF16), wg_ref[...], preferred_element_type=F32) + bg_ref[...]
        y_ref[b] = y * jax.nn.sigmoid(gate)


def _s5_mixer(u, h0_re, h0_im, ab_re, ab_im, b_mat, c_mat, d_skip, w_glu, b_glu, *, tm, nb):
    bsz, t, w = u.shape
    assert bsz % nb == 0
    gn = ab_re.shape[-1]
    n_slabs = b_mat.shape[0]
    nt = pl.cdiv(t, tm)
    row = lambda b, i: (b, i, 0)
    st = lambda b, i: (b, 0, 0)
    c2 = lambda b, i: (0, 0)
    c3 = lambda b, i: (0, 0, 0)
    kern = functools.partial(_s5_kernel, t_valid=t, tm=tm, nb=nb, n_slabs=n_slabs,
                             slab_in=w // n_slabs, slab_state=gn // n_slabs)
    return pl.pallas_call(
        kern,
        grid=(bsz // nb, nt),
        in_specs=[
            pl.BlockSpec((nb, tm, w), row),
            pl.BlockSpec((nb, 1, gn), st),
            pl.BlockSpec((nb, 1, gn), st),
            pl.BlockSpec((1, gn), c2),
            pl.BlockSpec((1, gn), c2),
            pl.BlockSpec(b_mat.shape, c3),
            pl.BlockSpec(c_mat.shape, c3),
            pl.BlockSpec((1, w), c2),
            pl.BlockSpec((w, w), c2),
            pl.BlockSpec((1, w), c2),
        ],
        out_specs=[
            pl.BlockSpec((nb, tm, w), row),
            pl.BlockSpec((nb, 1, gn), st),
            pl.BlockSpec((nb, 1, gn), st),
        ],
        out_shape=[
            jax.ShapeDtypeStruct((bsz, t, w), F32),
            jax.ShapeDtypeStruct((bsz, 1, gn), F32),
            jax.ShapeDtypeStruct((bsz, 1, gn), F32),
        ],
        scratch_shapes=[pltpu.VMEM((nb, tm, gn), F32), pltpu.VMEM((nb, tm, gn), F32),
                        pltpu.VMEM((nb, 1, gn), F32), pltpu.VMEM((nb, 1, gn), F32)],
        compiler_params=_cparams(("arbitrary", "arbitrary")),
        name="s5_mixer",
    )(u, h0_re, h0_im, ab_re, ab_im, b_mat, c_mat, d_skip, w_glu, b_glu)


def _fox_prompt_kernel(kn_ref, bi_ref, qa_ref, qb_ref, ka_ref, kb_ref, va_ref, vb_ref, o_ref,
                       m_ref, acc_ref, qn_ref, g_ref, *, tq, tk, nk, n_heads):
    i = pl.program_id(2)
    heads = ((qa_ref, ka_ref, va_ref), (qb_ref, kb_ref, vb_ref))
    m_ref[...] = jnp.full_like(m_ref, -jnp.inf)
    acc_ref[...] = jnp.zeros_like(acc_ref)
    lane = lax.broadcasted_iota(jnp.int32, (tq, LANES), 1)
    for hh, (q_ref, _, _) in enumerate(heads):
        qf = jnp.where(lane < HEAD_DIM, q_ref[...].astype(F32), 0.0)
        qn_ref[hh] = jnp.sqrt(jnp.sum(qf * qf, axis=-1, keepdims=True)) * NORM_SLACK
    table = (pl.program_id(0) * n_heads + 2 * pl.program_id(1)) * nk

    def reaches(j):
        hit = False
        for hh in range(2):
            kn = kn_ref[table + hh * nk + j]
            bi = bi_ref[table + hh * nk + j]
            slack = jnp.min(m_ref[hh] - qn_ref[hh] * kn)
            hit = jnp.logical_or(hit, slack <= bi + UNDERFLOW_LOG2)
        return hit

    row0 = i * tq
    diag = row0 // tk

    def scores(j, masked):
        start = pl.multiple_of(j * tk, tk)
        out = []
        for q_ref, k_ref, _ in heads:
            g = lax.dot_general(q_ref[...], k_ref[pl.ds(start, tk), :], (((1,), (1,)), ((), ())),
                                preferred_element_type=F32)
            if masked:
                r = row0 + lax.broadcasted_iota(jnp.int32, (tq, tk), 0)
                c = start + lax.broadcasted_iota(jnp.int32, (tq, tk), 1)
                g = jnp.where(c <= r, g, -jnp.inf)
            out.append(g)
        return out

    def consume(j):
        start = pl.multiple_of(j * tk, tk)
        for hh, (_, _, v_ref) in enumerate(heads):
            g = g_ref[hh]
            m_old = m_ref[hh]
            m_new = jnp.maximum(m_old, jnp.max(g, axis=-1, keepdims=True))
            p = jnp.exp2(g - m_new)
            alpha = jnp.exp2(m_old - m_new)
            acc_ref[hh] = alpha * acc_ref[hh] + jnp.dot(p.astype(BF16), v_ref[pl.ds(start, tk), :],
                                                        preferred_element_type=F32)
            m_ref[hh] = m_new

    for hh, g in enumerate(scores(diag, True)):
        g_ref[hh] = g

    def body(c):
        t, _ = c
        cur = diag - t
        nxt = jnp.maximum(cur - 1, 0)
        consume(cur)
        g_next = scores(nxt, False)
        for hh, g in enumerate(g_next):
            g_ref[hh] = g
        return t + 1, jnp.logical_and(cur >= 1, reaches(nxt))

    lax.while_loop(lambda c: c[1], body, (jnp.int32(0), jnp.bool_(True)))
    outs = []
    for hh in range(2):
        acc = acc_ref[hh]
        outs.append(acc[:, :HEAD_DIM] / acc[:, HEAD_DIM:HEAD_DIM + 1])
    o_ref[...] = jnp.concatenate(outs, axis=-1)


def _fox_prompt(qh, kh, vh, kn_sq, bias_max, *, tq):
    bsz, nh, tp, wl = qh.shape
    nq = tp // tq
    nk = kn_sq.shape[1]
    tk = tp // nk
    assert tk % tq == 0
    stat = lambda a: lax.cummax(jnp.transpose(a[:, :, 0, :nh], (0, 2, 1)), axis=2).reshape(-1)
    kn_tab = stat(jnp.sqrt(kn_sq) * NORM_SLACK)
    bi_tab = stat(bias_max)
    qa = lambda b, p, i, kn, bi: (b, 2 * p, i, 0)
    qb = lambda b, p, i, kn, bi: (b, 2 * p + 1, i, 0)
    fa = lambda b, p, i, kn, bi: (b, 2 * p, 0, 0)
    fb = lambda b, p, i, kn, bi: (b, 2 * p + 1, 0, 0)
    q_spec = lambda im: pl.BlockSpec((None, None, tq, wl), im)
    kv_spec = lambda im: pl.BlockSpec((None, None, tp, wl), im)
    grid_spec = pltpu.PrefetchScalarGridSpec(
        num_scalar_prefetch=2,
        grid=(bsz, nh // 2, nq),
        in_specs=[q_spec(qa), q_spec(qb), kv_spec(fa), kv_spec(fb), kv_spec(fa), kv_spec(fb)],
        out_specs=pl.BlockSpec((None, tq, 2 * HEAD_DIM), lambda b, p, i, kn, bi: (b, i, p)),
        scratch_shapes=[pltpu.VMEM((2, tq, 1), F32), pltpu.VMEM((2, tq, wl), F32), pltpu.VMEM((2, tq, 1), F32),
                        pltpu.VMEM((2, tq, tk), F32)],
    )
    return pl.pallas_call(
        functools.partial(_fox_prompt_kernel, tq=tq, tk=tk, nk=nk, n_heads=nh),
        grid_spec=grid_spec,
        out_shape=jax.ShapeDtypeStruct((bsz, tp, nh * HEAD_DIM), F32),
        compiler_params=_cparams(("arbitrary", "arbitrary", "arbitrary")),
        name="fox_prompt",
    )(kn_tab, bi_tab, qh, qh, kh, kh, vh, vh)


def _fox_sample_kernel(q_ref, kn_ref, vn_ref, ck_ref, cv_ref, bc_ref, bn_ref, o_ref, *, n_heads):
    tq = q_ref.shape[1]
    r = lax.broadcasted_iota(jnp.int32, (tq, tq), 0)
    c = lax.broadcasted_iota(jnp.int32, (tq, tq), 1)
    outs = []
    dn = (((1,), (1,)), ((), ()))
    for hh in range(n_heads):
        sl = slice(hh * HEAD_DIM, (hh + 1) * HEAD_DIM)
        q = q_ref[hh][:, :HEAD_DIM]
        kn = kn_ref[hh][:, :HEAD_DIM]
        vn = vn_ref[hh][:, :HEAD_DIM]
        kc = ck_ref[:, sl].astype(BF16)
        vc = cv_ref[:, sl].astype(BF16)
        g_c = lax.dot_general(q, kc, dn, preferred_element_type=F32) + bc_ref[hh]
        g_n = lax.dot_general(q, kn, dn, preferred_element_type=F32) + bn_ref[hh]
        g_n = jnp.where(c <= r, g_n, -jnp.inf)
        m = jnp.maximum(jnp.max(g_c, axis=-1, keepdims=True), jnp.max(g_n, axis=-1, keepdims=True))
        p_c = jnp.exp2(g_c - m)
        p_n = jnp.exp2(g_n - m)
        den = jnp.sum(p_c, axis=-1, keepdims=True) + jnp.sum(p_n, axis=-1, keepdims=True)
        num = jnp.dot(p_c.astype(BF16), vc, preferred_element_type=F32)
        num = num + jnp.dot(p_n.astype(BF16), vn, preferred_element_type=F32)
        outs.append(num / den)
    o_ref[...] = jnp.concatenate(outs, axis=-1)


def _fox_sample(qh, kh, vh, cache_k, cache_v, bias_cache, bias_new, *, tq):
    _, nh, _, wl = qh.shape
    bsz, past, _ = cache_k.shape
    w = nh * HEAD_DIM
    b4 = lambda b: (b, 0, 0, 0)
    b3 = lambda b: (b, 0, 0)
    rows = lambda b: (0, 0, b, 0)
    return pl.pallas_call(
        functools.partial(_fox_sample_kernel, n_heads=nh),
        grid=(bsz,),
        in_specs=[pl.BlockSpec((None, nh, tq, wl), rows), pl.BlockSpec((None, nh, tq, wl), rows),
                  pl.BlockSpec((None, nh, tq, wl), rows),
                  pl.BlockSpec((None, past, w), b3), pl.BlockSpec((None, past, w), b3),
                  pl.BlockSpec((None, nh, 1, past), b4), pl.BlockSpec((None, nh, 1, tq), b4)],
        out_specs=pl.BlockSpec((None, tq, w), b3),
        out_shape=jax.ShapeDtypeStruct((bsz, tq, w), F32),
        compiler_params=_cparams(("arbitrary",)),
        name="fox_sample",
    )(qh, kh, vh, cache_k, cache_v, bias_cache, bias_new)


def _cumsum_kernel(x_ref, o_ref, carry_ref, *, tm):
    i = pl.program_id(1)

    @pl.when(i == 0)
    def _():
        carry_ref[...] = jnp.zeros_like(carry_ref)

    fc = _prefix_sum_rows(x_ref[...]) + carry_ref[0:1, :]
    o_ref[...] = fc
    carry_ref[0:1, :] = fc[tm - 1:tm, :]


def _cumsum_time(x, *, tm):
    bsz, t, w = x.shape
    row = lambda b, i: (b, i, 0)
    return pl.pallas_call(
        functools.partial(_cumsum_kernel, tm=tm),
        grid=(bsz, t // tm),
        in_specs=[pl.BlockSpec((None, tm, w), row)],
        out_specs=pl.BlockSpec((None, tm, w), row),
        out_shape=jax.ShapeDtypeStruct((bsz, t, w), F32),
        scratch_shapes=[pltpu.VMEM((8, w), F32)],
        compiler_params=_cparams(("arbitrary", "arbitrary")),
        name="cumsum_time",
    )(x)


def _merge_router_kernel(*refs, t_valid, tm, n_meta):
    n_x = 3 if n_meta else 1
    (ys_ref, ya_ref, gs_ref, ga_ref, wo_ref, gf_ref, wr_ref, br_ref, cnt_in_ref,
     x2_ref, ti_ref, tw_ref, tp_ref, cnt_out_ref, cnt_ref) = refs[n_x:]
    first = (pl.program_id(0) == 0) & (pl.program_id(1) == 0)

    @pl.when(first)
    def _():
        cnt_ref[...] = jnp.broadcast_to(cnt_in_ref[...], cnt_ref.shape)

    mix = jnp.concatenate([_rms(ys_ref[...], gs_ref[...]), _rms(ya_ref[...], ga_ref[...])], axis=-1)
    x = _timeline_rows(refs[:n_x], pl.program_id(1), n_meta)
    x2 = x + jnp.dot(mix.astype(BF16), wo_ref[...], preferred_element_type=F32)
    x2_ref[...] = x2
    h2 = _rms(x2, gf_ref[...])
    h_hi = h2.astype(BF16)
    h_lo = (h2 - h_hi.astype(F32)).astype(BF16)
    logits = (jnp.dot(h_hi, wr_ref[0], preferred_element_type=F32)
              + jnp.dot(h_lo, wr_ref[0], preferred_element_type=F32)
              + jnp.dot(h_hi, wr_ref[1], preferred_element_type=F32)) + br_ref[...]

    lane = lax.broadcasted_iota(jnp.int32, (tm, LANES), 1)
    rows = pl.program_id(1) * tm + lax.broadcasted_iota(jnp.int32, (tm, 1), 0)
    valid = rows < t_valid
    vals = logits
    top_v, top_i, sels = [], [], []
    for _ in range(TOP_K):
        mx = jnp.max(vals, axis=-1, keepdims=True)
        idx = jnp.min(jnp.where(vals == mx, lane, LANES), axis=-1, keepdims=True)
        sel = lane == idx
        vals = jnp.where(sel, -jnp.inf, vals)
        top_v.append(mx)
        top_i.append(idx)
        sels.append(sel)
    ex = [jnp.exp(v - top_v[0]) for v in top_v]
    den = ex[0] + ex[1] + ex[2] + ex[3]

    member = jnp.zeros((tm, LANES), F32)
    for sel in sels:
        member = member + jnp.where(valid, jnp.where(sel, 1.0, 0.0), 0.0)
    r = lax.broadcasted_iota(jnp.int32, (tm, tm), 0)
    c = lax.broadcasted_iota(jnp.int32, (tm, tm), 1)
    tri = (c < r).astype(BF16)
    before = jnp.dot(tri, member.astype(BF16), preferred_element_type=F32) + cnt_ref[0:1, :]
    cnt_new = cnt_ref[0:1, :] + jnp.sum(member, axis=0, keepdims=True)
    cnt_ref[0:1, :] = cnt_new
    cnt_out_ref[...] = cnt_new

    ti = jnp.zeros((tm, LANES), jnp.int32)
    tw = jnp.zeros((tm, LANES), F32)
    tpos = jnp.zeros((tm, LANES), F32)
    for kk in range(TOP_K):
        pos = jnp.sum(jnp.where(sels[kk], before, 0.0), axis=-1, keepdims=True)
        ti = jnp.where(lane == kk, top_i[kk], ti)
        tw = jnp.where(lane == kk, ex[kk] / den, tw)
        tpos = jnp.where(lane == kk, pos, tpos)
    ti_ref[...] = ti
    tw_ref[...] = tw
    tp_ref[...] = tpos.astype(jnp.int32)


def _merge_router(x, ys, ya, g_ssm, g_attn, w_out, g_ffn, w_router, b_router, cnt_in, *, tm, meta=None):
    bsz, _, d = x.shape
    x_ops, x_specs, t = _timeline_specs(x, meta, tm)
    w = ys.shape[-1]
    nt = pl.cdiv(t, tm)
    row = lambda b, i: (b, i, 0)
    c2 = lambda b, i: (0, 0)
    return pl.pallas_call(
        functools.partial(_merge_router_kernel, t_valid=t, tm=tm, n_meta=0 if meta is None else meta.shape[0]),
        grid=(bsz, nt),
        in_specs=x_specs + [
            pl.BlockSpec((None, tm, w), row),
            pl.BlockSpec((None, tm, w), row),
            pl.BlockSpec((1, w), c2), pl.BlockSpec((1, w), c2),
            pl.BlockSpec(w_out.shape, c2),
            pl.BlockSpec((1, d), c2),
            pl.BlockSpec(w_router.shape, lambda b, i: (0, 0, 0)),
            pl.BlockSpec((1, LANES), c2),
            pl.BlockSpec((1, LANES), c2),
        ],
        out_specs=[
            pl.BlockSpec((None, tm, d), row),
            pl.BlockSpec((None, tm, LANES), row),
            pl.BlockSpec((None, tm, LANES), row),
            pl.BlockSpec((None, tm, LANES), row),
            pl.BlockSpec((1, LANES), c2),
        ],
        out_shape=[
            jax.ShapeDtypeStruct((bsz, t, d), F32),
            jax.ShapeDtypeStruct((bsz, t, LANES), jnp.int32),
            jax.ShapeDtypeStruct((bsz, t, LANES), F32),
            jax.ShapeDtypeStruct((bsz, t, LANES), jnp.int32),
            jax.ShapeDtypeStruct((1, LANES), F32),
        ],
        scratch_shapes=[pltpu.VMEM((8, LANES), F32)],
        compiler_params=_cparams(("arbitrary", "arbitrary")),
        name="merge_router",
    )(*x_ops, ys, ya, g_ssm, g_attn, w_out, g_ffn, w_router, b_router, cnt_in)


def _dispatch_kernel(last_tile_ref, nu_ref, slots_ref, xa_ref, xb_ref, g_ref, xs_hbm, hbuf_ref, zero_ref, zsem, sem,
                     *, t_valid, nt_a, n_steps_a, tm, n_experts, n_tiles, tile):
    step = pl.program_id(0)

    def zero_tile(t):
        return pltpu.make_async_copy(zero_ref, xs_hbm.at[pl.ds(pl.multiple_of(t * tile, tile), tile)], zsem)

    @pl.when(step == 0)
    def _():
        zero_ref[...] = jnp.zeros_like(zero_ref)

        def start_tail(t, c):
            zero_tile(t).start()
            return c

        def wait_tail(t, c):
            zero_tile(t).wait()
            return c

        lax.fori_loop(nu_ref[0], n_tiles, start_tail, 0)
        lax.fori_loop(nu_ref[0], n_tiles, wait_tail, 0)
        for e in range(n_experts):
            lt = last_tile_ref[e]

            @pl.when(lt >= 0)
            def _():
                zero_tile(lt).start()
        for e in range(n_experts):
            lt = last_tile_ref[e]

            @pl.when(lt >= 0)
            def _():
                zero_tile(lt).wait()

    in_a = step < n_steps_a

    @pl.when(in_a)
    def _():
        hbuf_ref[...] = _rms(xa_ref[...], g_ref[...])

    @pl.when(jnp.logical_not(in_a))
    def _():
        hbuf_ref[...] = _rms(xb_ref[...], g_ref[...])

    def row_copy(r, kk):
        return pltpu.make_async_copy(hbuf_ref.at[pl.ds(r, 1)], xs_hbm.at[pl.ds(slots_ref[r * TOP_K + kk], 1)], sem)

    def issue(r, c):
        for kk in range(TOP_K):
            row_copy(r, kk).start()
        return c

    def drain(r, c):
        for kk in range(TOP_K):
            row_copy(r, kk).wait()
        return c

    n_valid = jnp.where(in_a, jnp.minimum(tm, t_valid - (step % nt_a) * tm), tm)

    @pl.when(n_valid == tm)
    def _():
        lax.fori_loop(0, tm, issue, 0, unroll=ISSUE_UNROLL)
        for _ in range(TOP_K):
            pltpu.make_async_copy(hbuf_ref, xs_hbm.at[pl.ds(0, tm)], sem).wait()

    @pl.when(n_valid < tm)
    def _():
        lax.fori_loop(0, n_valid, issue, 0)
        lax.fori_loop(0, n_valid, drain, 0)


def _dispatch(xa, xb, g_ffn, slots_flat, last_tile, n_used, *, n_slots, tm):
    bsz, t, d = xa.shape
    nt_a = pl.cdiv(t, tm)
    n_steps_a = bsz * nt_a
    n_steps_b = xb.shape[0] // tm
    n_experts = last_tile.shape[0]
    grid_spec = pltpu.PrefetchScalarGridSpec(
        num_scalar_prefetch=2,
        grid=(n_steps_a + n_steps_b,),
        in_specs=[
            pl.BlockSpec((tm * TOP_K,), lambda s, lt, nu: (s,), memory_space=pltpu.SMEM),
            pl.BlockSpec((None, tm, d), lambda s, lt, nu: (jnp.minimum(s, n_steps_a - 1) // nt_a,
                                                          jnp.minimum(s, n_steps_a - 1) % nt_a, 0)),
            pl.BlockSpec((tm, d), lambda s, lt, nu: (jnp.maximum(s - n_steps_a, 0), 0)),
            pl.BlockSpec((1, d), lambda s, lt, nu: (0, 0)),
        ],
        out_specs=pl.BlockSpec(memory_space=pl.ANY),
        scratch_shapes=[pltpu.VMEM((tm, d), F32), pltpu.VMEM((EXPERT_TILE, d), F32),
                        pltpu.SemaphoreType.DMA, pltpu.SemaphoreType.DMA],
    )
    return pl.pallas_call(
        functools.partial(_dispatch_kernel, t_valid=t, nt_a=nt_a, n_steps_a=n_steps_a, tm=tm, n_experts=n_experts,
                          n_tiles=n_slots // EXPERT_TILE, tile=EXPERT_TILE),
        grid_spec=grid_spec,
        out_shape=jax.ShapeDtypeStruct((n_slots, d), F32),
        compiler_params=_cparams(("arbitrary",)),
        name="moe_dispatch",
    )(last_tile, n_used, slots_flat, xa, xb, g_ffn)


def _expert_kernel(te_ref, nu_ref, tpe_ref, x_ref, b1_ref, b2_ref, w1_hbm, w2_hbm, y_ref,
                   w1f_ref, w2f_ref, w1b_ref, w2b_ref, grp_ref, sem1, sem2, *, d_ff):
    i = pl.program_id(0)
    used = nu_ref[0]
    e = te_ref[i]
    prev = te_ref[jnp.maximum(i - 1, 0)]
    changed = (i == 0) | (e != prev)

    def fetch(expert, slot):
        return (pltpu.make_async_copy(w1_hbm.at[expert], w1f_ref.at[slot], sem1.at[slot]),
                pltpu.make_async_copy(w2_hbm.at[expert], w2f_ref.at[slot], sem2.at[slot]))

    @pl.when(i == 0)
    def _():
        grp_ref[0] = 0
        for cp in fetch(e, 0):
            cp.start()

    @pl.when((i < used) & changed)
    def _():
        slot = grp_ref[0] % 2
        for cp in fetch(e, slot):
            cp.wait()
        w1b_ref[...] = w1f_ref[slot].astype(BF16)
        w2b_ref[...] = w2f_ref[slot].astype(BF16)
        nxt = i + tpe_ref[e]

        @pl.when(nxt < used)
        def _():
            for cp in fetch(te_ref[nxt], 1 - slot):
                cp.start()
        grp_ref[0] = grp_ref[0] + 1

    @pl.when(i >= used)
    def _():
        y_ref[...] = jnp.zeros_like(y_ref)

    @pl.when(i < used)
    def _():
        x = x_ref[...].astype(BF16)
        h = jnp.dot(x, w1b_ref[...], preferred_element_type=F32) + b1_ref[...]
        hg = jnp.minimum(h[:, :d_ff], SWIGLU_LIMIT)
        hl = jnp.clip(h[:, d_ff:], -SWIGLU_LIMIT, SWIGLU_LIMIT)
        act = (hl + 1.0) * (hg * jax.nn.sigmoid(SWIGLU_ALPHA * hg))
        y_ref[...] = jnp.dot(act.astype(BF16), w2b_ref[...], preferred_element_type=F32) + b2_ref[...]


def _experts(xs, tile_expert, n_used, tiles_per, w1, b1, w2, b2):
    s, d = xs.shape
    n_tiles = s // EXPERT_TILE
    n_e, _, two_ff = w1.shape
    d_ff = two_ff // 2

    def xmap(i, te, nu, tpe):
        return (i, 0)

    def bmap(i, te, nu, tpe):
        return (te[i], 0, 0)

    grid_spec = pltpu.PrefetchScalarGridSpec(
        num_scalar_prefetch=3,
        grid=(n_tiles,),
        in_specs=[
            pl.BlockSpec((EXPERT_TILE, d), xmap),
            pl.BlockSpec((None, 1, two_ff), bmap),
            pl.BlockSpec((None, 1, d), bmap),
            pl.BlockSpec(memory_space=pl.ANY),
            pl.BlockSpec(memory_space=pl.ANY),
        ],
        out_specs=pl.BlockSpec((EXPERT_TILE, d), xmap),
        scratch_shapes=[pltpu.VMEM((2, d, two_ff), F32), pltpu.VMEM((2, d_ff, d), F32),
                        pltpu.VMEM((d, two_ff), BF16), pltpu.VMEM((d_ff, d), BF16),
                        pltpu.SMEM((1,), jnp.int32),
                        pltpu.SemaphoreType.DMA((2,)), pltpu.SemaphoreType.DMA((2,))],
    )
    return pl.pallas_call(
        functools.partial(_expert_kernel, d_ff=d_ff),
        grid_spec=grid_spec,
        out_shape=jax.ShapeDtypeStruct((s, d), F32),
        compiler_params=_cparams(("arbitrary",)),
        name="moe_experts",
    )(tile_expert, n_used, tiles_per, xs, b1.reshape(n_e, 1, two_ff), b2.reshape(n_e, 1, d), w1, w2)


def _combine_kernel(slots_ref, g_ref, x2_hbm, tw_hbm, ys_hbm, o_ref, xbuf_ref, wbuf_ref, buf_ref, sem, lsem,
                    *, tm, row_offset):
    b = pl.program_id(0)
    start = pl.multiple_of(row_offset + pl.program_id(1) * tm, 8)
    x_copy = pltpu.make_async_copy(x2_hbm.at[b, pl.ds(start, tm)], xbuf_ref, lsem)
    w_copy = pltpu.make_async_copy(tw_hbm.at[b, pl.ds(start, tm)], wbuf_ref, lsem)
    x_copy.start()
    w_copy.start()

    def issue(r, c):
        for kk in range(TOP_K):
            pltpu.make_async_copy(ys_hbm.at[pl.ds(slots_ref[r * TOP_K + kk], 1)],
                                  buf_ref.at[kk, pl.ds(r, 1)], sem).start()
        return c

    lax.fori_loop(0, tm, issue, 0, unroll=ISSUE_UNROLL)
    x_copy.wait()
    w_copy.wait()
    for kk in range(TOP_K):
        pltpu.make_async_copy(ys_hbm.at[pl.ds(0, tm)], buf_ref.at[kk], sem).wait()
    tw = wbuf_ref[...]
    acc = xbuf_ref[...]
    for kk in range(TOP_K):
        acc = acc + tw[:, kk:kk + 1] * buf_ref[kk]
    o_ref[...] = _rms(acc, g_ref[...])


def _combine(x2, tw, slots_flat, ys, g_final, *, tm, row_offset):
    bsz, t, d = x2.shape
    t_out = t - row_offset
    nt = t_out // tm
    grid_spec = pltpu.PrefetchScalarGridSpec(
        num_scalar_prefetch=0,
        grid=(bsz, nt),
        in_specs=[
            pl.BlockSpec((tm * TOP_K,), lambda b, i: (b * nt + i,), memory_space=pltpu.SMEM),
            pl.BlockSpec((1, d), lambda b, i: (0, 0)),
            pl.BlockSpec(memory_space=pl.ANY),
            pl.BlockSpec(memory_space=pl.ANY),
            pl.BlockSpec(memory_space=pl.ANY),
        ],
        out_specs=pl.BlockSpec((None, tm, d), lambda b, i: (b, i, 0)),
        scratch_shapes=[pltpu.VMEM((tm, d), F32), pltpu.VMEM((tm, LANES), F32),
                        pltpu.VMEM((TOP_K, tm, d), F32), pltpu.SemaphoreType.DMA, pltpu.SemaphoreType.DMA],
    )
    return pl.pallas_call(
        functools.partial(_combine_kernel, tm=tm, row_offset=row_offset),
        grid_spec=grid_spec,
        out_shape=jax.ShapeDtypeStruct((bsz, t_out, d), F32),
        compiler_params=_cparams(("arbitrary", "arbitrary")),
        name="moe_combine",
    )(slots_flat, g_final, x2, tw, ys)


def _pad_lanes(a, value=0.0):
    return jnp.pad(a, [(0, 0)] * (a.ndim - 1) + [(0, LANES - a.shape[-1])], constant_values=value)


def kernel(x_prompt, x_sample, cache_k, cache_v, cache_logf, state_ssm_re, state_ssm_im, meta_tokens, norm_mix_g, w_in, b_forget, ssm_a_re, ssm_a_im, ssm_log_dt, ssm_b_re, ssm_b_im, ssm_c_re, ssm_c_im, ssm_d, w_glu, b_glu, g_out_ssm, g_out_attn, w_out, norm_ffn_g, w_router, b_router, w_mlp1, b_mlp1, w_mlp2, b_mlp2, norm_final_g):
    depth = w_in.shape[0]
    assert depth == 1, "the routing tables below are built for a single trunk layer"
    l = 0
    bp, seq, d = x_prompt.shape
    bs, ts, _ = x_sample.shape
    n_heads = b_forget.shape[1]
    n_groups, ssm_n = ssm_a_re.shape[1:]
    n_experts = w_router.shape[2]
    w_ssm = n_groups * SSM_P
    w_attn = n_heads * HEAD_DIM
    gn = n_groups * ssm_n
    past = cache_k.shape[2]
    assert seq % ROW_TILE == 0 and (bs * ts) % ROW_TILE == 0

    meta = meta_tokens.astype(x_prompt.dtype)
    assert meta.shape[0] == N_META
    xs = x_sample
    tp = seq + N_META

    w_main = w_in[l][:, :w_ssm + 3 * w_attn].astype(BF16)
    w_f = _pad_lanes(w_in[l][:, w_ssm + 3 * w_attn:]).astype(BF16)
    b_f = _pad_lanes(b_forget[l][None, :])
    g_mix = norm_mix_g[l][None, :]
    ab_re, ab_im, bb_re, bb_im = _s5_params(ssm_a_re[l], ssm_a_im[l], ssm_log_dt[l], ssm_b_re[l], ssm_b_im[l])
    b_mat, c_mat = _s5_block_mats(bb_re, bb_im, ssm_c_re[l], ssm_c_im[l], n_groups)

    proj = functools.partial(_in_proj, n_heads=n_heads, w_ssm=w_ssm, w_attn=w_attn)
    up, kp, vp, lfp, qhp, khp, vhp, knp, bip = proj(x_prompt, g_mix, w_main, w_f, b_f, tm=ROW_TILE, meta=meta)
    n_s = bs * ts
    us, ks, vs, lfs, qhs, khs, vhs, _, _ = proj(xs.reshape(1, n_s, d), g_mix, w_main, w_f, b_f, tm=n_s)
    us = us.reshape(bs, ts, w_ssm)
    lfs = lfs.reshape(bs, ts, n_heads)

    s5 = functools.partial(_s5_mixer, ab_re=ab_re, ab_im=ab_im, b_mat=b_mat, c_mat=c_mat,
                           d_skip=ssm_d[l].reshape(1, w_ssm), w_glu=w_glu[l].astype(BF16), b_glu=b_glu[l][None, :])
    zeros_state = jnp.zeros((bp, 1, gn), F32)
    ysp, hrp, hip = s5(up, zeros_state, zeros_state, tm=ROW_TILE, nb=bp)
    yss, hrs, his = s5(us, state_ssm_re[l].reshape(bs, 1, gn), state_ssm_im[l].reshape(bs, 1, gn), tm=ts, nb=1)

    yap = _fox_prompt(qhp, khp, vhp, knp, bip, tq=ATTN_Q_TILE)
    t_all = past + ts
    assert bs * n_heads == LANES
    tm_cum = max(m for m in range(8, ROW_TILE + 1, 8) if t_all % m == 0)
    lf_all = jnp.concatenate([cache_logf[l].astype(F32), lfs], axis=1)
    fcs = _cumsum_time(jnp.transpose(lf_all, (1, 0, 2)).reshape(1, t_all, LANES), tm=tm_cum)
    bias_s = -LOG2E * jnp.transpose(fcs.reshape(t_all, bs, n_heads), (1, 2, 0))[:, :, None, :]
    yas = _fox_sample(qhs, khs, vhs, cache_k[l].reshape(bs, past, w_attn), cache_v[l].reshape(bs, past, w_attn),
                      bias_s[..., :past], bias_s[..., past:], tq=ts)

    wr = _pad_lanes(w_router[l])
    wr_hi = wr.astype(BF16)
    wr_parts = jnp.stack([wr_hi, (wr - wr_hi.astype(F32)).astype(BF16)])
    merge = functools.partial(_merge_router, g_ssm=g_out_ssm[l][None, :], g_attn=g_out_attn[l][None, :],
                              w_out=w_out[l].astype(BF16), g_ffn=norm_ffn_g[l][None, :],
                              w_router=wr_parts, b_router=_pad_lanes(b_router[l][None, :], value=-1e30))
    x2p, tip, twp, tpp, cnt_p = merge(x_prompt, ysp, yap, cnt_in=jnp.zeros((1, LANES), F32), tm=ROW_TILE, meta=meta)
    x2s, tis, tws, tps, cnt = merge(xs.reshape(1, n_s, d), yss.reshape(1, n_s, w_ssm), yas.reshape(1, n_s, w_attn),
                                    cnt_in=cnt_p, tm=n_s)

    n_tok = bp * tp + bs * ts
    counts = cnt[0, :n_experts].astype(jnp.int32)
    tiles_per = (counts + EXPERT_TILE - 1) // EXPERT_TILE
    tile_end = jnp.cumsum(tiles_per)
    tile_start = tile_end - tiles_per
    n_tiles = (n_tok * TOP_K) // EXPERT_TILE + n_experts
    n_slots = n_tiles * EXPERT_TILE
    n_used = tile_end[-1:].astype(jnp.int32)
    tile_ids = jnp.minimum(jnp.arange(n_tiles, dtype=jnp.int32), n_used - 1)
    tile_expert = jnp.sum((tile_end[None, :] <= tile_ids[:, None]).astype(jnp.int32), axis=1)
    last_tile = jnp.where(tiles_per > 0, tile_end - 1, -1).astype(jnp.int32)
    slot_base = (tile_start * EXPERT_TILE).astype(F32)

    def slots_of(ids, pos):
        onehot = jax.nn.one_hot(ids[..., :TOP_K], n_experts, dtype=F32)
        base = jnp.einsum('btke,e->btk', onehot, slot_base, precision=lax.Precision.HIGHEST)
        return base.astype(jnp.int32) + pos[..., :TOP_K]

    slots_p = slots_of(tip, tpp)
    slots_s = slots_of(tis, tps)
    tpad = pl.cdiv(tp, ROW_TILE) * ROW_TILE
    slots_disp = jnp.concatenate([jnp.pad(slots_p, ((0, 0), (0, tpad - tp), (0, 0))).reshape(-1),
                                  slots_s.reshape(-1)])
    xs_sorted = _dispatch(x2p, x2s.reshape(bs * ts, d), norm_ffn_g[l][None, :], slots_disp, last_tile, n_used,
                          n_slots=n_slots, tm=ROW_TILE)
    ys_sorted = _experts(xs_sorted, tile_expert, n_used, tiles_per.astype(jnp.int32),
                         w_mlp1[l], b_mlp1[l], w_mlp2[l], b_mlp2[l])

    g_fin = norm_final_g[None, :]
    y_prompt = _combine(x2p, twp, slots_p[:, N_META:].reshape(-1), ys_sorted, g_fin, tm=ROW_TILE, row_offset=N_META)
    y_sample = _combine(x2s.reshape(1, bs * ts, d), tws.reshape(1, bs * ts, LANES), slots_s.reshape(-1), ys_sorted,
                        g_fin, tm=ROW_TILE, row_offset=0).reshape(bs, ts, d)

    hd = HEAD_DIM
    st = lambda a, b: a.reshape(1, b, n_groups, ssm_n)
    return (y_prompt, y_sample,
            kp.reshape(1, bp, tp, n_heads, hd), vp.reshape(1, bp, tp, n_heads, hd), lfp[None],
            st(hrp, bp), st(hip, bp),
            ks.reshape(1, bs, ts, n_heads, hd), vs.reshape(1, bs, ts, n_heads, hd), lfs[None],
            st(hrs, bs), st(his, bs))
```

```python
import functools
import math

import numpy as np
import jax
import jax.numpy as jnp
from jax import lax
from jax.experimental import pallas as pl
from jax.experimental.pallas import tpu as pltpu

F32 = jnp.float32
BF16 = jnp.bfloat16

LANES = 128
VMEM_LIMIT_BYTES = 56 * 1024 * 1024

N_META = 16
HEAD_DIM = 64
SSM_P = 16
SSM_N = 64
TOP_K = 4
SWIGLU_LIMIT = 7.0
SWIGLU_ALPHA = 1.702
RMS_EPS = 1e-6
LAMBDA_RE_MAX = -1e-4
LOG2E = math.log2(math.e)

ROW_TILE = 512
ATTN_Q_TILE = 512
EXPERT_TILE = 512
GROUP_SLAB = 8
BIAS_TERMS = 3
ISSUE_UNROLL = 8
UNDERFLOW_LOG2 = 152.0
NORM_SLACK = 1.001


def _cparams(sem):
    return pltpu.CompilerParams(dimension_semantics=sem, vmem_limit_bytes=VMEM_LIMIT_BYTES)


def _rms(v, g):
    return v * lax.rsqrt(jnp.mean(v * v, axis=-1, keepdims=True) + RMS_EPS) * g


def _timeline_rows(x_refs, i, n_meta):
    if not n_meta:
        return x_refs[0][...]
    meta_ref, prev_ref, cur_ref = x_refs
    head = jnp.where(i == 0, meta_ref[...], prev_ref[...])
    return jnp.concatenate([head, cur_ref[...][:cur_ref.shape[0] - n_meta]], axis=0)


def _timeline_specs(x, meta, tm):
    bsz, t_in, d = x.shape
    if meta is None:
        return [x], [pl.BlockSpec((None, tm, d), lambda b, i: (b, i, 0))], t_in
    n_meta = meta.shape[0]
    assert t_in % tm == 0 and tm % n_meta == 0
    per = tm // n_meta
    last = t_in // tm - 1
    specs = [pl.BlockSpec((n_meta, d), lambda b, i: (0, 0)),
             pl.BlockSpec((None, n_meta, d), lambda b, i: (b, jnp.maximum(i * per - 1, 0), 0)),
             pl.BlockSpec((None, tm, d), lambda b, i: (b, jnp.minimum(i, last), 0))]
    return [meta, x, x], specs, t_in + n_meta


def _split_bf16(a):
    pieces = []
    rest = a
    for _ in range(BIAS_TERMS):
        piece = rest.astype(BF16)
        pieces.append(piece)
        rest = rest - piece.astype(F32)
    return jnp.concatenate(pieces, axis=-1)


def _prefix_sum_rows(x):
    tm = x.shape[0]
    r = lax.broadcasted_iota(jnp.int32, (tm, tm), 0)
    c = lax.broadcasted_iota(jnp.int32, (tm, tm), 1)
    tri = jnp.where(c <= r, 1.0, 0.0).astype(BF16)
    s = jnp.dot(tri, _split_bf16(x), preferred_element_type=F32)
    return sum(s[:, t * LANES:(t + 1) * LANES] for t in range(BIAS_TERMS))


def _in_proj_kernel(*refs, t_valid, tm, n_heads, w_ssm, w_attn, n_meta):
    n_x = 3 if n_meta else 1
    (g_ref, w_ref, wf_ref, bf_ref, sel_ref,
     u_ref, k_ref, v_ref, lf_ref, qh_ref, kh_ref, vh_ref, kn_ref, bi_ref, carry_ref) = refs[n_x:]
    i = pl.program_id(1)

    @pl.when(i == 0)
    def _():
        carry_ref[...] = jnp.zeros_like(carry_ref)

    x = _timeline_rows(refs[:n_x], i, n_meta)
    h = _rms(x, g_ref[...]).astype(BF16)
    rows = i * tm + lax.broadcasted_iota(jnp.int32, (tm, 1), 0)
    valid = rows < t_valid
    z = jnp.where(valid, jnp.dot(h, w_ref[...], preferred_element_type=F32), 0.0)
    u_ref[...] = z[:, :w_ssm]
    k_ref[...] = z[:, w_ssm + w_attn:w_ssm + 2 * w_attn]
    v_ref[...] = z[:, w_ssm + 2 * w_attn:w_ssm + 3 * w_attn]

    zf = jnp.dot(h, wf_ref[...], preferred_element_type=F32) + bf_ref[...]
    lf = jnp.where(valid, jnp.minimum(zf, 0.0) - jnp.log1p(jnp.exp(-jnp.abs(zf))), 0.0)
    lf_ref[...] = lf[:, :n_heads]
    fc = _prefix_sum_rows(lf) + carry_ref[0:1, :]
    carry_ref[0:1, :] = fc[tm - 1:tm, :]

    bias = -LOG2E * fc
    bi_ref[...] = jnp.max(bias, axis=0, keepdims=True)
    tails = jnp.dot(_split_bf16(bias), sel_ref[...], preferred_element_type=F32)

    lane = lax.broadcasted_iota(jnp.int32, (tm, LANES), 1)
    lane1 = lax.broadcasted_iota(jnp.int32, (1, LANES), 1)
    low = lane < HEAD_DIM
    in_tail = (lane >= HEAD_DIM) & (lane < HEAD_DIM + BIAS_TERMS)
    q_tail = jnp.where(in_tail, 1.0, 0.0)
    v_tail = jnp.where(lane == HEAD_DIM, 1.0, 0.0)
    kn = jnp.zeros((1, LANES), F32)
    for pair in range(n_heads // 2):
        sl = slice(pair * LANES, (pair + 1) * LANES)
        qt = z[:, w_ssm:w_ssm + w_attn][:, sl] * (LOG2E * HEAD_DIM ** -0.5)
        kt = z[:, w_ssm + w_attn:w_ssm + 2 * w_attn][:, sl]
        vt = z[:, w_ssm + 2 * w_attn:w_ssm + 3 * w_attn][:, sl]
        for half in range(2):
            hh = 2 * pair + half
            if half:
                qt, kt, vt = (pltpu.roll(a, HEAD_DIM, 1) for a in (qt, kt, vt))
            k_tail = jnp.where(in_tail, pltpu.roll(tails, HEAD_DIM - BIAS_TERMS * hh, 1), 0.0)
            kb = jnp.where(low, kt, k_tail).astype(BF16)
            qh_ref[hh] = jnp.where(low, qt, q_tail).astype(BF16)
            kh_ref[hh] = kb
            vh_ref[hh] = jnp.where(low, vt, v_tail).astype(BF16)
            kf = jnp.where(low, kb.astype(F32), 0.0)
            ksq = jnp.max(jnp.sum(kf * kf, axis=-1, keepdims=True), axis=0, keepdims=True)
            kn = jnp.where(lane1 == hh, ksq, kn)
    kn_ref[...] = kn


def _bias_selector(n_heads):
    sel = np.zeros((BIAS_TERMS * LANES, LANES), np.float32)
    for hh in range(n_heads):
        for c in range(BIAS_TERMS):
            sel[c * LANES + hh, BIAS_TERMS * hh + c] = 1.0
    return jnp.asarray(sel, BF16)


def _in_proj(x, g, w_main, w_f, b_f, *, n_heads, w_ssm, w_attn, tm, meta=None):
    bsz, _, d = x.shape
    x_ops, x_specs, t = _timeline_specs(x, meta, tm)
    nt = pl.cdiv(t, tm)
    tp = nt * tm
    row = lambda b, i: (b, i, 0)
    const = lambda b, i: (0, 0)
    head = lambda b, i: (b, 0, i, 0)
    tile = lambda b, i: (b, i, 0, 0)
    sel = _bias_selector(n_heads)
    kern = functools.partial(_in_proj_kernel, t_valid=t, tm=tm, n_heads=n_heads, w_ssm=w_ssm, w_attn=w_attn,
                             n_meta=0 if meta is None else meta.shape[0])
    return pl.pallas_call(
        kern,
        grid=(bsz, nt),
        in_specs=x_specs + [
            pl.BlockSpec((1, d), const),
            pl.BlockSpec(w_main.shape, const),
            pl.BlockSpec(w_f.shape, const),
            pl.BlockSpec((1, LANES), const),
            pl.BlockSpec(sel.shape, const),
        ],
        out_specs=[
            pl.BlockSpec((None, tm, w_ssm), row),
            pl.BlockSpec((None, tm, w_attn), row),
            pl.BlockSpec((None, tm, w_attn), row),
            pl.BlockSpec((None, tm, n_heads), row),
            pl.BlockSpec((None, n_heads, tm, LANES), head),
            pl.BlockSpec((None, n_heads, tm, LANES), head),
            pl.BlockSpec((None, n_heads, tm, LANES), head),
            pl.BlockSpec((None, None, 1, LANES), tile),
            pl.BlockSpec((None, None, 1, LANES), tile),
        ],
        out_shape=[
            jax.ShapeDtypeStruct((bsz, t, w_ssm), F32),
            jax.ShapeDtypeStruct((bsz, t, w_attn), F32),
            jax.ShapeDtypeStruct((bsz, t, w_attn), F32),
            jax.ShapeDtypeStruct((bsz, t, n_heads), F32),
            jax.ShapeDtypeStruct((bsz, n_heads, tp, LANES), BF16),
            jax.ShapeDtypeStruct((bsz, n_heads, tp, LANES), BF16),
            jax.ShapeDtypeStruct((bsz, n_heads, tp, LANES), BF16),
            jax.ShapeDtypeStruct((bsz, nt, 1, LANES), F32),
            jax.ShapeDtypeStruct((bsz, nt, 1, LANES), F32),
        ],
        scratch_shapes=[pltpu.VMEM((8, LANES), F32)],
        compiler_params=_cparams(("arbitrary", "arbitrary")),
        name="in_proj",
    )(*x_ops, g, w_main, w_f, b_f, sel)


def _s5_param_kernel(are_ref, aim_ref, ldt_ref, bre_ref, bim_ref, abr_ref, abi_ref, bbr_ref, bbi_ref):
    lam_re = jnp.minimum(are_ref[...], LAMBDA_RE_MAX)
    lam_im = aim_ref[...]
    dt = jnp.exp(ldt_ref[...])
    mag = jnp.exp(lam_re * dt)
    ab_re = mag * jnp.cos(lam_im * dt)
    ab_im = mag * jnp.sin(lam_im * dt)
    abr_ref[...] = ab_re
    abi_ref[...] = ab_im
    nr = ab_re - 1.0
    ni = ab_im
    den = lam_re * lam_re + lam_im * lam_im
    cr = (nr * lam_re + ni * lam_im) / den
    ci = (ni * lam_re - nr * lam_im) / den
    b_re = bre_ref[...]
    b_im = bim_ref[...]
    bbr_ref[...] = cr * b_re - ci * b_im
    bbi_ref[...] = cr * b_im + ci * b_re


def _s5_params(a_re, a_im, log_dt, b_re, b_im):
    g, n = a_re.shape
    p = b_re.shape[-1]
    gn = g * n
    flat = lambda a: a.reshape(1, gn)
    ldt = jnp.broadcast_to(log_dt[:, None], (g, n)).reshape(1, gn)
    bt = lambda b: jnp.transpose(b, (2, 0, 1)).reshape(p, gn)
    return pl.pallas_call(
        _s5_param_kernel,
        out_shape=[jax.ShapeDtypeStruct((1, gn), F32), jax.ShapeDtypeStruct((1, gn), F32),
                   jax.ShapeDtypeStruct((p, gn), F32), jax.ShapeDtypeStruct((p, gn), F32)],
        name="s5_params",
    )(flat(a_re), flat(a_im), ldt, bt(b_re), bt(b_im))


def _s5_block_mats(bb_re, bb_im, c_re, c_im, n_groups):
    p = bb_re.shape[0]
    n = bb_re.shape[1] // n_groups
    s = n_groups // GROUP_SLAB
    eye = jnp.eye(GROUP_SLAB, dtype=F32)

    def in_blk(bb):
        b4 = bb.reshape(p, s, GROUP_SLAB, n)
        return jnp.einsum('qsgn,gh->sgqhn', b4, eye)

    b_mat = jnp.concatenate([in_blk(bb_re).reshape(s, GROUP_SLAB * p, GROUP_SLAB * n),
                             in_blk(bb_im).reshape(s, GROUP_SLAB * p, GROUP_SLAB * n)], axis=-1)

    def out_blk(c):
        c4 = c.reshape(s, GROUP_SLAB, p, n)
        return jnp.einsum('sgpn,gh->sgnhp', c4, eye)

    c_mat = jnp.concatenate([out_blk(c_re).reshape(s, GROUP_SLAB * n, GROUP_SLAB * p),
                             out_blk(-c_im).reshape(s, GROUP_SLAB * n, GROUP_SLAB * p)], axis=1)
    return b_mat.astype(BF16), c_mat.astype(BF16)


def _s5_kernel(u_ref, h0r_ref, h0i_ref, abr_ref, abi_ref, bm_ref, cm_ref, d_ref, wg_ref, bg_ref,
               y_ref, hr_out_ref, hi_out_ref, sre_ref, sim_ref, hre_ref, him_ref,
               *, t_valid, tm, nb, n_slabs, slab_in, slab_state):
    i = pl.program_id(1)
    nt = pl.num_programs(1)

    @pl.when(i == 0)
    def _():
        hre_ref[...] = h0r_ref[...]
        him_ref[...] = h0i_ref[...]

    for b in range(nb):
        ub = u_ref[b].astype(BF16)
        for s in range(n_slabs):
            z = jnp.dot(ub[:, s * slab_in:(s + 1) * slab_in], bm_ref[s], preferred_element_type=F32)
            sre_ref[b, :, s * slab_state:(s + 1) * slab_state] = z[:, :slab_state]
            sim_ref[b, :, s * slab_state:(s + 1) * slab_state] = z[:, slab_state:]

    a_re = abr_ref[...]
    a_im = abi_ref[...]
    last_row = (t_valid - 1) % tm

    def step(t, carry):
        out = []
        for b in range(nb):
            h_re, h_im = carry[2 * b], carry[2 * b + 1]
            b_re = sre_ref[b, pl.ds(t, 1), :]
            b_im = sim_ref[b, pl.ds(t, 1), :]
            n_re = a_re * h_re - a_im * h_im + b_re
            n_im = a_re * h_im + a_im * h_re + b_im
            sre_ref[b, pl.ds(t, 1), :] = n_re
            sim_ref[b, pl.ds(t, 1), :] = n_im
            out += [n_re, n_im]
        return tuple(out)

    init = tuple(ref[b] for b in range(nb) for ref in (hre_ref, him_ref))
    fin = lax.fori_loop(0, tm, step, init)
    for b in range(nb):
        hre_ref[b] = fin[2 * b]
        him_ref[b] = fin[2 * b + 1]

    @pl.when(i == nt - 1)
    def _():
        for b in range(nb):
            hr_out_ref[b] = sre_ref[b, last_row:last_row + 1, :]
            hi_out_ref[b] = sim_ref[b, last_row:last_row + 1, :]

    for b in range(nb):
        ys = []
        for s in range(n_slabs):
            sl = slice(s * slab_state, (s + 1) * slab_state)
            cm = cm_ref[s]
            y = jnp.dot(sre_ref[b, :, sl].astype(BF16), cm[:slab_state], preferred_element_type=F32)
            y = y + jnp.dot(sim_ref[b, :, sl].astype(BF16), cm[slab_state:], preferred_element_type=F32)
            ys.append(y)
        y = jnp.concatenate(ys, axis=-1) + d_ref[...] * u_ref[b]
        y = jax.nn.gelu(y)
        gate = jnp.dot(y.astype(BF16), wg_ref[...], preferred_element_type=F32) + bg_ref[...]
        y_ref[b] = y * jax.nn.sigmoid(gate)


def _s5_mixer(u, h0_re, h0_im, ab_re, ab_im, b_mat, c_mat, d_skip, w_glu, b_glu, *, tm, nb):
    bsz, t, w = u.shape
    assert bsz % nb == 0
    gn = ab_re.shape[-1]
    n_slabs = b_mat.shape[0]
    nt = pl.cdiv(t, tm)
    row = lambda b, i: (b, i, 0)
    st = lambda b, i: (b, 0, 0)
    c2 = lambda b, i: (0, 0)
    c3 = lambda b, i: (0, 0, 0)
    kern = functools.partial(_s5_kernel, t_valid=t, tm=tm, nb=nb, n_slabs=n_slabs,
                             slab_in=w // n_slabs, slab_state=gn // n_slabs)
    return pl.pallas_call(
        kern,
        grid=(bsz // nb, nt),
        in_specs=[
            pl.BlockSpec((nb, tm, w), row),
            pl.BlockSpec((nb, 1, gn), st),
            pl.BlockSpec((nb, 1, gn), st),
            pl.BlockSpec((1, gn), c2),
            pl.BlockSpec((1, gn), c2),
            pl.BlockSpec(b_mat.shape, c3),
            pl.BlockSpec(c_mat.shape, c3),
            pl.BlockSpec((1, w), c2),
            pl.BlockSpec((w, w), c2),
            pl.BlockSpec((1, w), c2),
        ],
        out_specs=[
            pl.BlockSpec((nb, tm, w), row),
            pl.BlockSpec((nb, 1, gn), st),
            pl.BlockSpec((nb, 1, gn), st),
        ],
        out_shape=[
            jax.ShapeDtypeStruct((bsz, t, w), F32),
            jax.ShapeDtypeStruct((bsz, 1, gn), F32),
            jax.ShapeDtypeStruct((bsz, 1, gn), F32),
        ],
        scratch_shapes=[pltpu.VMEM((nb, tm, gn), F32), pltpu.VMEM((nb, tm, gn), F32),
                        pltpu.VMEM((nb, 1, gn), F32), pltpu.VMEM((nb, 1, gn), F32)],
        compiler_params=_cparams(("arbitrary", "arbitrary")),
        name="s5_mixer",
    )(u, h0_re, h0_im, ab_re, ab_im, b_mat, c_mat, d_skip, w_glu, b_glu)


def _fox_prompt_kernel(kn_ref, bi_ref, qa_ref, qb_ref, ka_ref, kb_ref, va_ref, vb_ref, o_ref,
                       m_ref, acc_ref, qn_ref, g_ref, *, tq, tk, nk, n_heads):
    i = pl.program_id(2)
    heads = ((qa_ref, ka_ref, va_ref), (qb_ref, kb_ref, vb_ref))
    m_ref[...] = jnp.full_like(m_ref, -jnp.inf)
    acc_ref[...] = jnp.zeros_like(acc_ref)
    lane = lax.broadcasted_iota(jnp.int32, (tq, LANES), 1)
    for hh, (q_ref, _, _) in enumerate(heads):
        qf = jnp.where(lane < HEAD_DIM, q_ref[...].astype(F32), 0.0)
        qn_ref[hh] = jnp.sqrt(jnp.sum(qf * qf, axis=-1, keepdims=True)) * NORM_SLACK
    table = (pl.program_id(0) * n_heads + 2 * pl.program_id(1)) * nk

    def reaches(j):
        hit = False
        for hh in range(2):
            kn = kn_ref[table + hh * nk + j]
            bi = bi_ref[table + hh * nk + j]
            slack = jnp.min(m_ref[hh] - qn_ref[hh] * kn)
            hit = jnp.logical_or(hit, slack <= bi + UNDERFLOW_LOG2)
        return hit

    row0 = i * tq
    diag = row0 // tk

    def scores(j, masked):
        start = pl.multiple_of(j * tk, tk)
        out = []
        for q_ref, k_ref, _ in heads:
            g = lax.dot_general(q_ref[...], k_ref[pl.ds(start, tk), :], (((1,), (1,)), ((), ())),
                                preferred_element_type=F32)
            if masked:
                r = row0 + lax.broadcasted_iota(jnp.int32, (tq, tk), 0)
                c = start + lax.broadcasted_iota(jnp.int32, (tq, tk), 1)
                g = jnp.where(c <= r, g, -jnp.inf)
            out.append(g)
        return out

    def consume(j):
        start = pl.multiple_of(j * tk, tk)
        for hh, (_, _, v_ref) in enumerate(heads):
            g = g_ref[hh]
            m_old = m_ref[hh]
            m_new = jnp.maximum(m_old, jnp.max(g, axis=-1, keepdims=True))
            p = jnp.exp2(g - m_new)
            alpha = jnp.exp2(m_old - m_new)
            acc_ref[hh] = alpha * acc_ref[hh] + jnp.dot(p.astype(BF16), v_ref[pl.ds(start, tk), :],
                                                        preferred_element_type=F32)
            m_ref[hh] = m_new

    for hh, g in enumerate(scores(diag, True)):
        g_ref[hh] = g

    def body(c):
        t, _ = c
        cur = diag - t
        nxt = jnp.maximum(cur - 1, 0)
        consume(cur)
        g_next = scores(nxt, False)
        for hh, g in enumerate(g_next):
            g_ref[hh] = g
        return t + 1, jnp.logical_and(cur >= 1, reaches(nxt))

    lax.while_loop(lambda c: c[1], body, (jnp.int32(0), jnp.bool_(True)))
    outs = []
    for hh in range(2):
        acc = acc_ref[hh]
        outs.append(acc[:, :HEAD_DIM] / acc[:, HEAD_DIM:HEAD_DIM + 1])
    o_ref[...] = jnp.concatenate(outs, axis=-1)


def _fox_prompt(qh, kh, vh, kn_sq, bias_max, *, tq):
    bsz, nh, tp, wl = qh.shape
    nq = tp // tq
    nk = kn_sq.shape[1]
    tk = tp // nk
    assert tk % tq == 0
    stat = lambda a: lax.cummax(jnp.transpose(a[:, :, 0, :nh], (0, 2, 1)), axis=2).reshape(-1)
    kn_tab = stat(jnp.sqrt(kn_sq) * NORM_SLACK)
    bi_tab = stat(bias_max)
    qa = lambda b, p, i, kn, bi: (b, 2 * p, i, 0)
    qb = lambda b, p, i, kn, bi: (b, 2 * p + 1, i, 0)
    fa = lambda b, p, i, kn, bi: (b, 2 * p, 0, 0)
    fb = lambda b, p, i, kn, bi: (b, 2 * p + 1, 0, 0)
    q_spec = lambda im: pl.BlockSpec((None, None, tq, wl), im)
    kv_spec = lambda im: pl.BlockSpec((None, None, tp, wl), im)
    grid_spec = pltpu.PrefetchScalarGridSpec(
        num_scalar_prefetch=2,
        grid=(bsz, nh // 2, nq),
        in_specs=[q_spec(qa), q_spec(qb), kv_spec(fa), kv_spec(fb), kv_spec(fa), kv_spec(fb)],
        out_specs=pl.BlockSpec((None, tq, 2 * HEAD_DIM), lambda b, p, i, kn, bi: (b, i, p)),
        scratch_shapes=[pltpu.VMEM((2, tq, 1), F32), pltpu.VMEM((2, tq, wl), F32), pltpu.VMEM((2, tq, 1), F32),
                        pltpu.VMEM((2, tq, tk), F32)],
    )
    return pl.pallas_call(
        functools.partial(_fox_prompt_kernel, tq=tq, tk=tk, nk=nk, n_heads=nh),
        grid_spec=grid_spec,
        out_shape=jax.ShapeDtypeStruct((bsz, tp, nh * HEAD_DIM), F32),
        compiler_params=_cparams(("arbitrary", "arbitrary", "arbitrary")),
        name="fox_prompt",
    )(kn_tab, bi_tab, qh, qh, kh, kh, vh, vh)


def _fox_sample_kernel(q_ref, kn_ref, vn_ref, ck_ref, cv_ref, bc_ref, bn_ref, o_ref, *, n_heads):
    tq = q_ref.shape[1]
    r = lax.broadcasted_iota(jnp.int32, (tq, tq), 0)
    c = lax.broadcasted_iota(jnp.int32, (tq, tq), 1)
    outs = []
    dn = (((1,), (1,)), ((), ()))
    for hh in range(n_heads):
        sl = slice(hh * HEAD_DIM, (hh + 1) * HEAD_DIM)
        q = q_ref[hh][:, :HEAD_DIM]
        kn = kn_ref[hh][:, :HEAD_DIM]
        vn = vn_ref[hh][:, :HEAD_DIM]
        kc = ck_ref[:, sl].astype(BF16)
        vc = cv_ref[:, sl].astype(BF16)
        g_c = lax.dot_general(q, kc, dn, preferred_element_type=F32) + bc_ref[hh]
        g_n = lax.dot_general(q, kn, dn, preferred_element_type=F32) + bn_ref[hh]
        g_n = jnp.where(c <= r, g_n, -jnp.inf)
        m = jnp.maximum(jnp.max(g_c, axis=-1, keepdims=True), jnp.max(g_n, axis=-1, keepdims=True))
        p_c = jnp.exp2(g_c - m)
        p_n = jnp.exp2(g_n - m)
        den = jnp.sum(p_c, axis=-1, keepdims=True) + jnp.sum(p_n, axis=-1, keepdims=True)
        num = jnp.dot(p_c.astype(BF16), vc, preferred_element_type=F32)
        num = num + jnp.dot(p_n.astype(BF16), vn, preferred_element_type=F32)
        outs.append(num / den)
    o_ref[...] = jnp.concatenate(outs, axis=-1)


def _fox_sample(qh, kh, vh, cache_k, cache_v, bias_cache, bias_new, *, tq):
    _, nh, _, wl = qh.shape
    bsz, past, _ = cache_k.shape
    w = nh * HEAD_DIM
    b4 = lambda b: (b, 0, 0, 0)
    b3 = lambda b: (b, 0, 0)
    rows = lambda b: (0, 0, b, 0)
    return pl.pallas_call(
        functools.partial(_fox_sample_kernel, n_heads=nh),
        grid=(bsz,),
        in_specs=[pl.BlockSpec((None, nh, tq, wl), rows), pl.BlockSpec((None, nh, tq, wl), rows),
                  pl.BlockSpec((None, nh, tq, wl), rows),
                  pl.BlockSpec((None, past, w), b3), pl.BlockSpec((None, past, w), b3),
                  pl.BlockSpec((None, nh, 1, past), b4), pl.BlockSpec((None, nh, 1, tq), b4)],
        out_specs=pl.BlockSpec((None, tq, w), b3),
        out_shape=jax.ShapeDtypeStruct((bsz, tq, w), F32),
        compiler_params=_cparams(("arbitrary",)),
        name="fox_sample",
    )(qh, kh, vh, cache_k, cache_v, bias_cache, bias_new)


def _cumsum_kernel(x_ref, o_ref, carry_ref, *, tm):
    i = pl.program_id(1)

    @pl.when(i == 0)
    def _():
        carry_ref[...] = jnp.zeros_like(carry_ref)

    fc = _prefix_sum_rows(x_ref[...]) + carry_ref[0:1, :]
    o_ref[...] = fc
    carry_ref[0:1, :] = fc[tm - 1:tm, :]


def _cumsum_time(x, *, tm):
    bsz, t, w = x.shape
    row = lambda b, i: (b, i, 0)
    return pl.pallas_call(
        functools.partial(_cumsum_kernel, tm=tm),
        grid=(bsz, t // tm),
        in_specs=[pl.BlockSpec((None, tm, w), row)],
        out_specs=pl.BlockSpec((None, tm, w), row),
        out_shape=jax.ShapeDtypeStruct((bsz, t, w), F32),
        scratch_shapes=[pltpu.VMEM((8, w), F32)],
        compiler_params=_cparams(("arbitrary", "arbitrary")),
        name="cumsum_time",
    )(x)


def _merge_router_kernel(*refs, t_valid, tm, n_meta):
    n_x = 3 if n_meta else 1
    (ys_ref, ya_ref, gs_ref, ga_ref, wo_ref, gf_ref, wr_ref, br_ref, cnt_in_ref,
     x2_ref, ti_ref, tw_ref, tp_ref, cnt_out_ref, cnt_ref) = refs[n_x:]
    first = (pl.program_id(0) == 0) & (pl.program_id(1) == 0)

    @pl.when(first)
    def _():
        cnt_ref[...] = jnp.broadcast_to(cnt_in_ref[...], cnt_ref.shape)

    mix = jnp.concatenate([_rms(ys_ref[...], gs_ref[...]), _rms(ya_ref[...], ga_ref[...])], axis=-1)
    x = _timeline_rows(refs[:n_x], pl.program_id(1), n_meta)
    x2 = x + jnp.dot(mix.astype(BF16), wo_ref[...], preferred_element_type=F32)
    x2_ref[...] = x2
    h2 = _rms(x2, gf_ref[...])
    h_hi = h2.astype(BF16)
    h_lo = (h2 - h_hi.astype(F32)).astype(BF16)
    logits = (jnp.dot(h_hi, wr_ref[0], preferred_element_type=F32)
              + jnp.dot(h_lo, wr_ref[0], preferred_element_type=F32)
              + jnp.dot(h_hi, wr_ref[1], preferred_element_type=F32)) + br_ref[...]

    lane = lax.broadcasted_iota(jnp.int32, (tm, LANES), 1)
    rows = pl.program_id(1) * tm + lax.broadcasted_iota(jnp.int32, (tm, 1), 0)
    valid = rows < t_valid
    vals = logits
    top_v, top_i, sels = [], [], []
    for _ in range(TOP_K):
        mx = jnp.max(vals, axis=-1, keepdims=True)
        idx = jnp.min(jnp.where(vals == mx, lane, LANES), axis=-1, keepdims=True)
        sel = lane == idx
        vals = jnp.where(sel, -jnp.inf, vals)
        top_v.append(mx)
        top_i.append(idx)
        sels.append(sel)
    ex = [jnp.exp(v - top_v[0]) for v in top_v]
    den = ex[0] + ex[1] + ex[2] + ex[3]

    member = jnp.zeros((tm, LANES), F32)
    for sel in sels:
        member = member + jnp.where(valid, jnp.where(sel, 1.0, 0.0), 0.0)
    r = lax.broadcasted_iota(jnp.int32, (tm, tm), 0)
    c = lax.broadcasted_iota(jnp.int32, (tm, tm), 1)
    tri = (c < r).astype(BF16)
    before = jnp.dot(tri, member.astype(BF16), preferred_element_type=F32) + cnt_ref[0:1, :]
    cnt_new = cnt_ref[0:1, :] + jnp.sum(member, axis=0, keepdims=True)
    cnt_ref[0:1, :] = cnt_new
    cnt_out_ref[...] = cnt_new

    ti = jnp.zeros((tm, LANES), jnp.int32)
    tw = jnp.zeros((tm, LANES), F32)
    tpos = jnp.zeros((tm, LANES), F32)
    for kk in range(TOP_K):
        pos = jnp.sum(jnp.where(sels[kk], before, 0.0), axis=-1, keepdims=True)
        ti = jnp.where(lane == kk, top_i[kk], ti)
        tw = jnp.where(lane == kk, ex[kk] / den, tw)
        tpos = jnp.where(lane == kk, pos, tpos)
    ti_ref[...] = ti
    tw_ref[...] = tw
    tp_ref[...] = tpos.astype(jnp.int32)


def _merge_router(x, ys, ya, g_ssm, g_attn, w_out, g_ffn, w_router, b_router, cnt_in, *, tm, meta=None):
    bsz, _, d = x.shape
    x_ops, x_specs, t = _timeline_specs(x, meta, tm)
    w = ys.shape[-1]
    nt = pl.cdiv(t, tm)
    row = lambda b, i: (b, i, 0)
    c2 = lambda b, i: (0, 0)
    return pl.pallas_call(
        functools.partial(_merge_router_kernel, t_valid=t, tm=tm, n_meta=0 if meta is None else meta.shape[0]),
        grid=(bsz, nt),
        in_specs=x_specs + [
            pl.BlockSpec((None, tm, w), row),
            pl.BlockSpec((None, tm, w), row),
            pl.BlockSpec((1, w), c2), pl.BlockSpec((1, w), c2),
            pl.BlockSpec(w_out.shape, c2),
            pl.BlockSpec((1, d), c2),
            pl.BlockSpec(w_router.shape, lambda b, i: (0, 0, 0)),
            pl.BlockSpec((1, LANES), c2),
            pl.BlockSpec((1, LANES), c2),
        ],
        out_specs=[
            pl.BlockSpec((None, tm, d), row),
            pl.BlockSpec((None, tm, LANES), row),
            pl.BlockSpec((None, tm, LANES), row),
            pl.BlockSpec((None, tm, LANES), row),
            pl.BlockSpec((1, LANES), c2),
        ],
        out_shape=[
            jax.ShapeDtypeStruct((bsz, t, d), F32),
            jax.ShapeDtypeStruct((bsz, t, LANES), jnp.int32),
            jax.ShapeDtypeStruct((bsz, t, LANES), F32),
            jax.ShapeDtypeStruct((bsz, t, LANES), jnp.int32),
            jax.ShapeDtypeStruct((1, LANES), F32),
        ],
        scratch_shapes=[pltpu.VMEM((8, LANES), F32)],
        compiler_params=_cparams(("arbitrary", "arbitrary")),
        name="merge_router",
    )(*x_ops, ys, ya, g_ssm, g_attn, w_out, g_ffn, w_router, b_router, cnt_in)


def _dispatch_kernel(last_tile_ref, nu_ref, slots_ref, xa_ref, xb_ref, g_ref, xs_hbm, hbuf_ref, zero_ref, zsem, sem,
                     *, t_valid, nt_a, n_steps_a, tm, n_experts, n_tiles, tile):
    step = pl.program_id(0)

    def zero_tile(t):
        return pltpu.make_async_copy(zero_ref, xs_hbm.at[pl.ds(pl.multiple_of(t * tile, tile), tile)], zsem)

    @pl.when(step == 0)
    def _():
        zero_ref[...] = jnp.zeros_like(zero_ref)

        def start_tail(t, c):
            zero_tile(t).start()
            return c

        def wait_tail(t, c):
            zero_tile(t).wait()
            return c

        lax.fori_loop(nu_ref[0], n_tiles, start_tail, 0)
        lax.fori_loop(nu_ref[0], n_tiles, wait_tail, 0)
        for e in range(n_experts):
            lt = last_tile_ref[e]

            @pl.when(lt >= 0)
            def _():
                zero_tile(lt).start()
        for e in range(n_experts):
            lt = last_tile_ref[e]

            @pl.when(lt >= 0)
            def _():
                zero_tile(lt).wait()

    in_a = step < n_steps_a
    cur = step % 2

    @pl.when(in_a)
    def _():
        hbuf_ref[cur] = _rms(xa_ref[...], g_ref[...])

    @pl.when(jnp.logical_not(in_a))
    def _():
        hbuf_ref[cur] = _rms(xb_ref[...], g_ref[...])

    def issue(r, c):
        for kk in range(TOP_K):
            pltpu.make_async_copy(hbuf_ref.at[cur, pl.ds(r, 1)], xs_hbm.at[pl.ds(slots_ref[r * TOP_K + kk], 1)],
                                  sem.at[cur]).start()
        return c

    def valid_rows(s):
        return jnp.where(s < n_steps_a, jnp.minimum(tm, t_valid - (s % nt_a) * tm), tm)

    def wait_rows(buf, n_rows):
        @pl.when(n_rows == tm)
        def _():
            for _ in range(TOP_K):
                pltpu.make_async_copy(hbuf_ref.at[buf], xs_hbm.at[pl.ds(0, tm)], sem.at[buf]).wait()

        @pl.when(n_rows < tm)
        def _():
            def one(r, c):
                for _ in range(TOP_K):
                    pltpu.make_async_copy(hbuf_ref.at[buf, pl.ds(0, 1)], xs_hbm.at[pl.ds(0, 1)], sem.at[buf]).wait()
                return c
            lax.fori_loop(0, n_rows, one, 0)

    n_valid = valid_rows(step)

    @pl.when(n_valid == tm)
    def _():
        lax.fori_loop(0, tm, issue, 0, unroll=ISSUE_UNROLL)

    @pl.when(n_valid < tm)
    def _():
        lax.fori_loop(0, n_valid, issue, 0)

    @pl.when(step > 0)
    def _():
        wait_rows(1 - cur, valid_rows(step - 1))

    @pl.when(step == pl.num_programs(0) - 1)
    def _():
        wait_rows(cur, n_valid)


def _dispatch(xa, xb, g_ffn, slots_flat, last_tile, n_used, *, n_slots, tm):
    bsz, t, d = xa.shape
    nt_a = pl.cdiv(t, tm)
    n_steps_a = bsz * nt_a
    n_steps_b = xb.shape[0] // tm
    n_experts = last_tile.shape[0]
    grid_spec = pltpu.PrefetchScalarGridSpec(
        num_scalar_prefetch=2,
        grid=(n_steps_a + n_steps_b,),
        in_specs=[
            pl.BlockSpec((tm * TOP_K,), lambda s, lt, nu: (s,), memory_space=pltpu.SMEM),
            pl.BlockSpec((None, tm, d), lambda s, lt, nu: (jnp.minimum(s, n_steps_a - 1) // nt_a,
                                                          jnp.minimum(s, n_steps_a - 1) % nt_a, 0)),
            pl.BlockSpec((tm, d), lambda s, lt, nu: (jnp.maximum(s - n_steps_a, 0), 0)),
            pl.BlockSpec((1, d), lambda s, lt, nu: (0, 0)),
        ],
        out_specs=pl.BlockSpec(memory_space=pl.ANY),
        scratch_shapes=[pltpu.VMEM((2, tm, d), F32), pltpu.VMEM((EXPERT_TILE, d), F32),
                        pltpu.SemaphoreType.DMA, pltpu.SemaphoreType.DMA((2,))],
    )
    return pl.pallas_call(
        functools.partial(_dispatch_kernel, t_valid=t, nt_a=nt_a, n_steps_a=n_steps_a, tm=tm, n_experts=n_experts,
                          n_tiles=n_slots // EXPERT_TILE, tile=EXPERT_TILE),
        grid_spec=grid_spec,
        out_shape=jax.ShapeDtypeStruct((n_slots, d), F32),
        compiler_params=_cparams(("arbitrary",)),
        name="moe_dispatch",
    )(last_tile, n_used, slots_flat, xa, xb, g_ffn)


def _expert_kernel(te_ref, nu_ref, tpe_ref, x_ref, b1_ref, b2_ref, w1_hbm, w2_hbm, y_ref,
                   w1f_ref, w2f_ref, w1b_ref, w2b_ref, grp_ref, sem1, sem2, *, d_ff):
    i = pl.program_id(0)
    used = nu_ref[0]
    e = te_ref[i]
    prev = te_ref[jnp.maximum(i - 1, 0)]
    changed = (i == 0) | (e != prev)

    def fetch(expert, slot):
        return (pltpu.make_async_copy(w1_hbm.at[expert], w1f_ref.at[slot], sem1.at[slot]),
                pltpu.make_async_copy(w2_hbm.at[expert], w2f_ref.at[slot], sem2.at[slot]))

    @pl.when(i == 0)
    def _():
        grp_ref[0] = 0
        for cp in fetch(e, 0):
            cp.start()

    @pl.when((i < used) & changed)
    def _():
        slot = grp_ref[0] % 2
        for cp in fetch(e, slot):
            cp.wait()
        w1b_ref[...] = w1f_ref[slot].astype(BF16)
        w2b_ref[...] = w2f_ref[slot].astype(BF16)
        nxt = i + tpe_ref[e]

        @pl.when(nxt < used)
        def _():
            for cp in fetch(te_ref[nxt], 1 - slot):
                cp.start()
        grp_ref[0] = grp_ref[0] + 1

    @pl.when(i >= used)
    def _():
        y_ref[...] = jnp.zeros_like(y_ref)

    @pl.when(i < used)
    def _():
        x = x_ref[...].astype(BF16)
        h = jnp.dot(x, w1b_ref[...], preferred_element_type=F32) + b1_ref[...]
        hg = jnp.minimum(h[:, :d_ff], SWIGLU_LIMIT)
        hl = jnp.clip(h[:, d_ff:], -SWIGLU_LIMIT, SWIGLU_LIMIT)
        act = (hl + 1.0) * (hg * jax.nn.sigmoid(SWIGLU_ALPHA * hg))
        y_ref[...] = jnp.dot(act.astype(BF16), w2b_ref[...], preferred_element_type=F32) + b2_ref[...]


def _experts(xs, tile_expert, n_used, tiles_per, w1, b1, w2, b2):
    s, d = xs.shape
    n_tiles = s // EXPERT_TILE
    n_e, _, two_ff = w1.shape
    d_ff = two_ff // 2

    def xmap(i, te, nu, tpe):
        return (i, 0)

    def bmap(i, te, nu, tpe):
        return (te[i], 0, 0)

    grid_spec = pltpu.PrefetchScalarGridSpec(
        num_scalar_prefetch=3,
        grid=(n_tiles,),
        in_specs=[
            pl.BlockSpec((EXPERT_TILE, d), xmap),
            pl.BlockSpec((None, 1, two_ff), bmap),
            pl.BlockSpec((None, 1, d), bmap),
            pl.BlockSpec(memory_space=pl.ANY),
            pl.BlockSpec(memory_space=pl.ANY),
        ],
        out_specs=pl.BlockSpec((EXPERT_TILE, d), xmap),
        scratch_shapes=[pltpu.VMEM((2, d, two_ff), F32), pltpu.VMEM((2, d_ff, d), F32),
                        pltpu.VMEM((d, two_ff), BF16), pltpu.VMEM((d_ff, d), BF16),
                        pltpu.SMEM((1,), jnp.int32),
                        pltpu.SemaphoreType.DMA((2,)), pltpu.SemaphoreType.DMA((2,))],
    )
    return pl.pallas_call(
        functools.partial(_expert_kernel, d_ff=d_ff),
        grid_spec=grid_spec,
        out_shape=jax.ShapeDtypeStruct((s, d), F32),
        compiler_params=_cparams(("arbitrary",)),
        name="moe_experts",
    )(tile_expert, n_used, tiles_per, xs, b1.reshape(n_e, 1, two_ff), b2.reshape(n_e, 1, d), w1, w2)


def _combine_kernel(slots_ref, g_ref, x2_hbm, tw_hbm, ys_hbm, o_ref, xbuf_ref, wbuf_ref, buf_ref, sem, lsem,
                    *, tm, row_offset):
    b = pl.program_id(0)
    start = pl.multiple_of(row_offset + pl.program_id(1) * tm, 8)
    x_copy = pltpu.make_async_copy(x2_hbm.at[b, pl.ds(start, tm)], xbuf_ref, lsem)
    w_copy = pltpu.make_async_copy(tw_hbm.at[b, pl.ds(start, tm)], wbuf_ref, lsem)
    x_copy.start()
    w_copy.start()

    def issue(r, c):
        for kk in range(TOP_K):
            pltpu.make_async_copy(ys_hbm.at[pl.ds(slots_ref[r * TOP_K + kk], 1)],
                                  buf_ref.at[kk, pl.ds(r, 1)], sem).start()
        return c

    lax.fori_loop(0, tm, issue, 0, unroll=ISSUE_UNROLL)
    x_copy.wait()
    w_copy.wait()
    for kk in range(TOP_K):
        pltpu.make_async_copy(ys_hbm.at[pl.ds(0, tm)], buf_ref.at[kk], sem).wait()
    tw = wbuf_ref[...]
    acc = xbuf_ref[...]
    for kk in range(TOP_K):
        acc = acc + tw[:, kk:kk + 1] * buf_ref[kk]
    o_ref[...] = _rms(acc, g_ref[...])


def _combine(x2, tw, slots_flat, ys, g_final, *, tm, row_offset):
    bsz, t, d = x2.shape
    t_out = t - row_offset
    nt = t_out // tm
    grid_spec = pltpu.PrefetchScalarGridSpec(
        num_scalar_prefetch=0,
        grid=(bsz, nt),
        in_specs=[
            pl.BlockSpec((tm * TOP_K,), lambda b, i: (b * nt + i,), memory_space=pltpu.SMEM),
            pl.BlockSpec((1, d), lambda b, i: (0, 0)),
            pl.BlockSpec(memory_space=pl.ANY),
            pl.BlockSpec(memory_space=pl.ANY),
            pl.BlockSpec(memory_space=pl.ANY),
        ],
        out_specs=pl.BlockSpec((None, tm, d), lambda b, i: (b, i, 0)),
        scratch_shapes=[pltpu.VMEM((tm, d), F32), pltpu.VMEM((tm, LANES), F32),
                        pltpu.VMEM((TOP_K, tm, d), F32), pltpu.SemaphoreType.DMA, pltpu.SemaphoreType.DMA],
    )
    return pl.pallas_call(
        functools.partial(_combine_kernel, tm=tm, row_offset=row_offset),
        grid_spec=grid_spec,
        out_shape=jax.ShapeDtypeStruct((bsz, t_out, d), F32),
        compiler_params=_cparams(("arbitrary", "arbitrary")),
        name="moe_combine",
    )(slots_flat, g_final, x2, tw, ys)


def _pad_lanes(a, value=0.0):
    return jnp.pad(a, [(0, 0)] * (a.ndim - 1) + [(0, LANES - a.shape[-1])], constant_values=value)


def kernel(x_prompt, x_sample, cache_k, cache_v, cache_logf, state_ssm_re, state_ssm_im, meta_tokens, norm_mix_g, w_in, b_forget, ssm_a_re, ssm_a_im, ssm_log_dt, ssm_b_re, ssm_b_im, ssm_c_re, ssm_c_im, ssm_d, w_glu, b_glu, g_out_ssm, g_out_attn, w_out, norm_ffn_g, w_router, b_router, w_mlp1, b_mlp1, w_mlp2, b_mlp2, norm_final_g):
    depth = w_in.shape[0]
    assert depth == 1, "the routing tables below are built for a single trunk layer"
    l = 0
    bp, seq, d = x_prompt.shape
    bs, ts, _ = x_sample.shape
    n_heads = b_forget.shape[1]
    n_groups, ssm_n = ssm_a_re.shape[1:]
    n_experts = w_router.shape[2]
    w_ssm = n_groups * SSM_P
    w_attn = n_heads * HEAD_DIM
    gn = n_groups * ssm_n
    past = cache_k.shape[2]
    assert seq % ROW_TILE == 0 and (bs * ts) % ROW_TILE == 0

    meta = meta_tokens.astype(x_prompt.dtype)
    assert meta.shape[0] == N_META
    xs = x_sample
    tp = seq + N_META

    w_main = w_in[l][:, :w_ssm + 3 * w_attn].astype(BF16)
    w_f = _pad_lanes(w_in[l][:, w_ssm + 3 * w_attn:]).astype(BF16)
    b_f = _pad_lanes(b_forget[l][None, :])
    g_mix = norm_mix_g[l][None, :]
    ab_re, ab_im, bb_re, bb_im = _s5_params(ssm_a_re[l], ssm_a_im[l], ssm_log_dt[l], ssm_b_re[l], ssm_b_im[l])
    b_mat, c_mat = _s5_block_mats(bb_re, bb_im, ssm_c_re[l], ssm_c_im[l], n_groups)

    proj = functools.partial(_in_proj, n_heads=n_heads, w_ssm=w_ssm, w_attn=w_attn)
    up, kp, vp, lfp, qhp, khp, vhp, knp, bip = proj(x_prompt, g_mix, w_main, w_f, b_f, tm=ROW_TILE, meta=meta)
    n_s = bs * ts
    us, ks, vs, lfs, qhs, khs, vhs, _, _ = proj(xs.reshape(1, n_s, d), g_mix, w_main, w_f, b_f, tm=n_s)
    us = us.reshape(bs, ts, w_ssm)
    lfs = lfs.reshape(bs, ts, n_heads)

    s5 = functools.partial(_s5_mixer, ab_re=ab_re, ab_im=ab_im, b_mat=b_mat, c_mat=c_mat,
                           d_skip=ssm_d[l].reshape(1, w_ssm), w_glu=w_glu[l].astype(BF16), b_glu=b_glu[l][None, :])
    zeros_state = jnp.zeros((bp, 1, gn), F32)
    ysp, hrp, hip = s5(up, zeros_state, zeros_state, tm=ROW_TILE, nb=bp)
    yss, hrs, his = s5(us, state_ssm_re[l].reshape(bs, 1, gn), state_ssm_im[l].reshape(bs, 1, gn), tm=ts, nb=1)

    yap = _fox_prompt(qhp, khp, vhp, knp, bip, tq=ATTN_Q_TILE)
    t_all = past + ts
    assert bs * n_heads == LANES
    tm_cum = max(m for m in range(8, ROW_TILE + 1, 8) if t_all % m == 0)
    lf_all = jnp.concatenate([cache_logf[l].astype(F32), lfs], axis=1)
    fcs = _cumsum_time(jnp.transpose(lf_all, (1, 0, 2)).reshape(1, t_all, LANES), tm=tm_cum)
    bias_s = -LOG2E * jnp.transpose(fcs.reshape(t_all, bs, n_heads), (1, 2, 0))[:, :, None, :]
    yas = _fox_sample(qhs, khs, vhs, cache_k[l].reshape(bs, past, w_attn), cache_v[l].reshape(bs, past, w_attn),
                      bias_s[..., :past], bias_s[..., past:], tq=ts)

    wr = _pad_lanes(w_router[l])
    wr_hi = wr.astype(BF16)
    wr_parts = jnp.stack([wr_hi, (wr - wr_hi.astype(F32)).astype(BF16)])
    merge = functools.partial(_merge_router, g_ssm=g_out_ssm[l][None, :], g_attn=g_out_attn[l][None, :],
                              w_out=w_out[l].astype(BF16), g_ffn=norm_ffn_g[l][None, :],
                              w_router=wr_parts, b_router=_pad_lanes(b_router[l][None, :], value=-1e30))
    x2p, tip, twp, tpp, cnt_p = merge(x_prompt, ysp, yap, cnt_in=jnp.zeros((1, LANES), F32), tm=ROW_TILE, meta=meta)
    x2s, tis, tws, tps, cnt = merge(xs.reshape(1, n_s, d), yss.reshape(1, n_s, w_ssm), yas.reshape(1, n_s, w_attn),
                                    cnt_in=cnt_p, tm=n_s)

    n_tok = bp * tp + bs * ts
    counts = cnt[0, :n_experts].astype(jnp.int32)
    tiles_per = (counts + EXPERT_TILE - 1) // EXPERT_TILE
    tile_end = jnp.cumsum(tiles_per)
    tile_start = tile_end - tiles_per
    n_tiles = (n_tok * TOP_K) // EXPERT_TILE + n_experts
    n_slots = n_tiles * EXPERT_TILE
    n_used = tile_end[-1:].astype(jnp.int32)
    tile_ids = jnp.minimum(jnp.arange(n_tiles, dtype=jnp.int32), n_used - 1)
    tile_expert = jnp.sum((tile_end[None, :] <= tile_ids[:, None]).astype(jnp.int32), axis=1)
    last_tile = jnp.where(tiles_per > 0, tile_end - 1, -1).astype(jnp.int32)
    slot_base = (tile_start * EXPERT_TILE).astype(F32)

    def slots_of(ids, pos):
        onehot = jax.nn.one_hot(ids[..., :TOP_K], n_experts, dtype=F32)
        base = jnp.einsum('btke,e->btk', onehot, slot_base, precision=lax.Precision.HIGHEST)
        return base.astype(jnp.int32) + pos[..., :TOP_K]

    slots_p = slots_of(tip, tpp)
    slots_s = slots_of(tis, tps)
    tpad = pl.cdiv(tp, ROW_TILE) * ROW_TILE
    slots_disp = jnp.concatenate([jnp.pad(slots_p, ((0, 0), (0, tpad - tp), (0, 0))).reshape(-1),
                                  slots_s.reshape(-1)])
    xs_sorted = _dispatch(x2p, x2s.reshape(bs * ts, d), norm_ffn_g[l][None, :], slots_disp, last_tile, n_used,
                          n_slots=n_slots, tm=ROW_TILE)
    ys_sorted = _experts(xs_sorted, tile_expert, n_used, tiles_per.astype(jnp.int32),
                         w_mlp1[l], b_mlp1[l], w_mlp2[l], b_mlp2[l])

    g_fin = norm_final_g[None, :]
    y_prompt = _combine(x2p, twp, slots_p[:, N_META:].reshape(-1), ys_sorted, g_fin, tm=ROW_TILE, row_offset=N_META)
    y_sample = _combine(x2s.reshape(1, bs * ts, d), tws.reshape(1, bs * ts, LANES), slots_s.reshape(-1), ys_sorted,
                        g_fin, tm=ROW_TILE, row_offset=0).reshape(bs, ts, d)

    hd = HEAD_DIM
    st = lambda a, b: a.reshape(1, b, n_groups, ssm_n)
    return (y_prompt, y_sample,
            kp.reshape(1, bp, tp, n_heads, hd), vp.reshape(1, bp, tp, n_heads, hd), lfp[None],
            st(hrp, bp), st(hip, bp),
            ks.reshape(1, bs, ts, n_heads, hd), vs.reshape(1, bs, ts, n_heads, hd), lfs[None],
            st(hrs, bs), st(his, bs))
```

```python
import functools
import math

import numpy as np
import jax
import jax.numpy as jnp
from jax import lax
from jax.experimental import pallas as pl
from jax.experimental.pallas import tpu as pltpu

F32 = jnp.float32
BF16 = jnp.bfloat16

LANES = 128
VMEM_LIMIT_BYTES = 56 * 1024 * 1024

N_META = 16
HEAD_DIM = 64
SSM_P = 16
SSM_N = 64
TOP_K = 4
SWIGLU_LIMIT = 7.0
SWIGLU_ALPHA = 1.702
RMS_EPS = 1e-6
LAMBDA_RE_MAX = -1e-4
LOG2E = math.log2(math.e)

ROW_TILE = 512
ATTN_Q_TILE = 512
EXPERT_TILE = 512
GROUP_SLAB = 8
BIAS_TERMS = 3
ISSUE_UNROLL = 8
UNDERFLOW_LOG2 = 152.0
NORM_SLACK = 1.001


def _cparams(sem):
    return pltpu.CompilerParams(dimension_semantics=sem, vmem_limit_bytes=VMEM_LIMIT_BYTES)


def _rms(v, g):
    return v * lax.rsqrt(jnp.mean(v * v, axis=-1, keepdims=True) + RMS_EPS) * g


def _timeline_rows(x_refs, i, n_meta):
    if not n_meta:
        return x_refs[0][...]
    meta_ref, prev_ref, cur_ref = x_refs
    head = jnp.where(i == 0, meta_ref[...], prev_ref[...])
    return jnp.concatenate([head, cur_ref[...][:cur_ref.shape[0] - n_meta]], axis=0)


def _timeline_specs(x, meta, tm):
    bsz, t_in, d = x.shape
    if meta is None:
        return [x], [pl.BlockSpec((None, tm, d), lambda b, i: (b, i, 0))], t_in
    n_meta = meta.shape[0]
    assert t_in % tm == 0 and tm % n_meta == 0
    per = tm // n_meta
    last = t_in // tm - 1
    specs = [pl.BlockSpec((n_meta, d), lambda b, i: (0, 0)),
             pl.BlockSpec((None, n_meta, d), lambda b, i: (b, jnp.maximum(i * per - 1, 0), 0)),
             pl.BlockSpec((None, tm, d), lambda b, i: (b, jnp.minimum(i, last), 0))]
    return [meta, x, x], specs, t_in + n_meta


def _split_bf16(a):
    pieces = []
    rest = a
    for _ in range(BIAS_TERMS):
        piece = rest.astype(BF16)
        pieces.append(piece)
        rest = rest - piece.astype(F32)
    return jnp.concatenate(pieces, axis=-1)


def _prefix_sum_rows(x):
    tm = x.shape[0]
    r = lax.broadcasted_iota(jnp.int32, (tm, tm), 0)
    c = lax.broadcasted_iota(jnp.int32, (tm, tm), 1)
    tri = jnp.where(c <= r, 1.0, 0.0).astype(BF16)
    s = jnp.dot(tri, _split_bf16(x), preferred_element_type=F32)
    return sum(s[:, t * LANES:(t + 1) * LANES] for t in range(BIAS_TERMS))


def _in_proj_kernel(*refs, t_valid, tm, n_heads, w_ssm, w_attn, n_meta):
    n_x = 3 if n_meta else 1
    (g_ref, w_ref, wf_ref, bf_ref, sel_ref,
     u_ref, k_ref, v_ref, lf_ref, qh_ref, kh_ref, vh_ref, kn_ref, bi_ref, carry_ref) = refs[n_x:]
    i = pl.program_id(1)

    @pl.when(i == 0)
    def _():
        carry_ref[...] = jnp.zeros_like(carry_ref)

    x = _timeline_rows(refs[:n_x], i, n_meta)
    h = _rms(x, g_ref[...]).astype(BF16)
    rows = i * tm + lax.broadcasted_iota(jnp.int32, (tm, 1), 0)
    valid = rows < t_valid
    z = jnp.where(valid, jnp.dot(h, w_ref[...], preferred_element_type=F32), 0.0)
    u_ref[...] = z[:, :w_ssm]
    k_ref[...] = z[:, w_ssm + w_attn:w_ssm + 2 * w_attn]
    v_ref[...] = z[:, w_ssm + 2 * w_attn:w_ssm + 3 * w_attn]

    zf = jnp.dot(h, wf_ref[...], preferred_element_type=F32) + bf_ref[...]
    lf = jnp.where(valid, jnp.minimum(zf, 0.0) - jnp.log1p(jnp.exp(-jnp.abs(zf))), 0.0)
    lf_ref[...] = lf[:, :n_heads]
    fc = _prefix_sum_rows(lf) + carry_ref[0:1, :]
    carry_ref[0:1, :] = fc[tm - 1:tm, :]

    bias = -LOG2E * fc
    bi_ref[...] = jnp.max(bias, axis=0, keepdims=True)
    tails = jnp.dot(_split_bf16(bias), sel_ref[...], preferred_element_type=F32)

    lane = lax.broadcasted_iota(jnp.int32, (tm, LANES), 1)
    lane1 = lax.broadcasted_iota(jnp.int32, (1, LANES), 1)
    low = lane < HEAD_DIM
    in_tail = (lane >= HEAD_DIM) & (lane < HEAD_DIM + BIAS_TERMS)
    q_tail = jnp.where(in_tail, 1.0, 0.0)
    v_tail = jnp.where(lane == HEAD_DIM, 1.0, 0.0)
    kn = jnp.zeros((1, LANES), F32)
    for pair in range(n_heads // 2):
        sl = slice(pair * LANES, (pair + 1) * LANES)
        qt = z[:, w_ssm:w_ssm + w_attn][:, sl] * (LOG2E * HEAD_DIM ** -0.5)
        kt = z[:, w_ssm + w_attn:w_ssm + 2 * w_attn][:, sl]
        vt = z[:, w_ssm + 2 * w_attn:w_ssm + 3 * w_attn][:, sl]
        for half in range(2):
            hh = 2 * pair + half
            if half:
                qt, kt, vt = (pltpu.roll(a, HEAD_DIM, 1) for a in (qt, kt, vt))
            k_tail = jnp.where(in_tail, pltpu.roll(tails, HEAD_DIM - BIAS_TERMS * hh, 1), 0.0)
            kb = jnp.where(low, kt, k_tail).astype(BF16)
            qh_ref[hh] = jnp.where(low, qt, q_tail).astype(BF16)
            kh_ref[hh] = kb
            vh_ref[hh] = jnp.where(low, vt, v_tail).astype(BF16)
            kf = jnp.where(low, kb.astype(F32), 0.0)
            ksq = jnp.max(jnp.sum(kf * kf, axis=-1, keepdims=True), axis=0, keepdims=True)
            kn = jnp.where(lane1 == hh, ksq, kn)
    kn_ref[...] = kn


def _bias_selector(n_heads):
    sel = np.zeros((BIAS_TERMS * LANES, LANES), np.float32)
    for hh in range(n_heads):
        for c in range(BIAS_TERMS):
            sel[c * LANES + hh, BIAS_TERMS * hh + c] = 1.0
    return jnp.asarray(sel, BF16)


def _in_proj(x, g, w_main, w_f, b_f, *, n_heads, w_ssm, w_attn, tm, meta=None):
    bsz, _, d = x.shape
    x_ops, x_specs, t = _timeline_specs(x, meta, tm)
    nt = pl.cdiv(t, tm)
    tp = nt * tm
    row = lambda b, i: (b, i, 0)
    const = lambda b, i: (0, 0)
    head = lambda b, i: (b, 0, i, 0)
    tile = lambda b, i: (b, i, 0, 0)
    sel = _bias_selector(n_heads)
    kern = functools.partial(_in_proj_kernel, t_valid=t, tm=tm, n_heads=n_heads, w_ssm=w_ssm, w_attn=w_attn,
                             n_meta=0 if meta is None else meta.shape[0])
    return pl.pallas_call(
        kern,
        grid=(bsz, nt),
        in_specs=x_specs + [
            pl.BlockSpec((1, d), const),
            pl.BlockSpec(w_main.shape, const),
            pl.BlockSpec(w_f.shape, const),
            pl.BlockSpec((1, LANES), const),
            pl.BlockSpec(sel.shape, const),
        ],
        out_specs=[
            pl.BlockSpec((None, tm, w_ssm), row),
            pl.BlockSpec((None, tm, w_attn), row),
            pl.BlockSpec((None, tm, w_attn), row),
            pl.BlockSpec((None, tm, n_heads), row),
            pl.BlockSpec((None, n_heads, tm, LANES), head),
            pl.BlockSpec((None, n_heads, tm, LANES), head),
            pl.BlockSpec((None, n_heads, tm, LANES), head),
            pl.BlockSpec((None, None, 1, LANES), tile),
            pl.BlockSpec((None, None, 1, LANES), tile),
        ],
        out_shape=[
            jax.ShapeDtypeStruct((bsz, t, w_ssm), F32),
            jax.ShapeDtypeStruct((bsz, t, w_attn), F32),
            jax.ShapeDtypeStruct((bsz, t, w_attn), F32),
            jax.ShapeDtypeStruct((bsz, t, n_heads), F32),
            jax.ShapeDtypeStruct((bsz, n_heads, tp, LANES), BF16),
            jax.ShapeDtypeStruct((bsz, n_heads, tp, LANES), BF16),
            jax.ShapeDtypeStruct((bsz, n_heads, tp, LANES), BF16),
            jax.ShapeDtypeStruct((bsz, nt, 1, LANES), F32),
            jax.ShapeDtypeStruct((bsz, nt, 1, LANES), F32),
        ],
        scratch_shapes=[pltpu.VMEM((8, LANES), F32)],
        compiler_params=_cparams(("arbitrary", "arbitrary")),
        name="in_proj",
    )(*x_ops, g, w_main, w_f, b_f, sel)


def _s5_param_kernel(are_ref, aim_ref, ldt_ref, bre_ref, bim_ref, abr_ref, abi_ref, bbr_ref, bbi_ref):
    lam_re = jnp.minimum(are_ref[...], LAMBDA_RE_MAX)
    lam_im = aim_ref[...]
    dt = jnp.exp(ldt_ref[...])
    mag = jnp.exp(lam_re * dt)
    ab_re = mag * jnp.cos(lam_im * dt)
    ab_im = mag * jnp.sin(lam_im * dt)
    abr_ref[...] = ab_re
    abi_ref[...] = ab_im
    nr = ab_re - 1.0
    ni = ab_im
    den = lam_re * lam_re + lam_im * lam_im
    cr = (nr * lam_re + ni * lam_im) / den
    ci = (ni * lam_re - nr * lam_im) / den
    b_re = bre_ref[...]
    b_im = bim_ref[...]
    bbr_ref[...] = cr * b_re - ci * b_im
    bbi_ref[...] = cr * b_im + ci * b_re


def _s5_params(a_re, a_im, log_dt, b_re, b_im):
    g, n = a_re.shape
    p = b_re.shape[-1]
    gn = g * n
    flat = lambda a: a.reshape(1, gn)
    ldt = jnp.broadcast_to(log_dt[:, None], (g, n)).reshape(1, gn)
    bt = lambda b: jnp.transpose(b, (2, 0, 1)).reshape(p, gn)
    return pl.pallas_call(
        _s5_param_kernel,
        out_shape=[jax.ShapeDtypeStruct((1, gn), F32), jax.ShapeDtypeStruct((1, gn), F32),
                   jax.ShapeDtypeStruct((p, gn), F32), jax.ShapeDtypeStruct((p, gn), F32)],
        name="s5_params",
    )(flat(a_re), flat(a_im), ldt, bt(b_re), bt(b_im))


def _s5_block_mats(bb_re, bb_im, c_re, c_im, n_groups):
    p = bb_re.shape[0]
    n = bb_re.shape[1] // n_groups
    s = n_groups // GROUP_SLAB
    eye = jnp.eye(GROUP_SLAB, dtype=F32)

    def in_blk(bb):
        b4 = bb.reshape(p, s, GROUP_SLAB, n)
        return jnp.einsum('qsgn,gh->sgqhn', b4, eye)

    b_mat = jnp.concatenate([in_blk(bb_re).reshape(s, GROUP_SLAB * p, GROUP_SLAB * n),
                             in_blk(bb_im).reshape(s, GROUP_SLAB * p, GROUP_SLAB * n)], axis=-1)

    def out_blk(c):
        c4 = c.reshape(s, GROUP_SLAB, p, n)
        return jnp.einsum('sgpn,gh->sgnhp', c4, eye)

    c_mat = jnp.concatenate([out_blk(c_re).reshape(s, GROUP_SLAB * n, GROUP_SLAB * p),
                             out_blk(-c_im).reshape(s, GROUP_SLAB * n, GROUP_SLAB * p)], axis=1)
    return b_mat.astype(BF16), c_mat.astype(BF16)


def _s5_kernel(u_ref, h0r_ref, h0i_ref, abr_ref, abi_ref, bm_ref, cm_ref, d_ref, wg_ref, bg_ref,
               y_ref, hr_out_ref, hi_out_ref, sre_ref, sim_ref, hre_ref, him_ref,
               *, t_valid, tm, nb, n_slabs, slab_in, slab_state):
    i = pl.program_id(1)
    nt = pl.num_programs(1)

    @pl.when(i == 0)
    def _():
        hre_ref[...] = h0r_ref[...]
        him_ref[...] = h0i_ref[...]

    for b in range(nb):
        ub = u_ref[b].astype(BF16)
        for s in range(n_slabs):
            z = jnp.dot(ub[:, s * slab_in:(s + 1) * slab_in], bm_ref[s], preferred_element_type=F32)
            sre_ref[b, :, s * slab_state:(s + 1) * slab_state] = z[:, :slab_state]
            sim_ref[b, :, s * slab_state:(s + 1) * slab_state] = z[:, slab_state:]

    a_re = abr_ref[...]
    a_im = abi_ref[...]
    last_row = (t_valid - 1) % tm

    def step(t, carry):
        out = []
        for b in range(nb):
            h_re, h_im = carry[2 * b], carry[2 * b + 1]
            b_re = sre_ref[b, pl.ds(t, 1), :]
            b_im = sim_ref[b, pl.ds(t, 1), :]
            n_re = a_re * h_re - a_im * h_im + b_re
            n_im = a_re * h_im + a_im * h_re + b_im
            sre_ref[b, pl.ds(t, 1), :] = n_re
            sim_ref[b, pl.ds(t, 1), :] = n_im
            out += [n_re, n_im]
        return tuple(out)

    init = tuple(ref[b] for b in range(nb) for ref in (hre_ref, him_ref))
    fin = lax.fori_loop(0, tm, step, init)
    for b in range(nb):
        hre_ref[b] = fin[2 * b]
        him_ref[b] = fin[2 * b + 1]

    @pl.when(i == nt - 1)
    def _():
        for b in range(nb):
            hr_out_ref[b] = sre_ref[b, last_row:last_row + 1, :]
            hi_out_ref[b] = sim_ref[b, last_row:last_row + 1, :]

    for b in range(nb):
        ys = []
        for s in range(n_slabs):
            sl = slice(s * slab_state, (s + 1) * slab_state)
            cm = cm_ref[s]
            y = jnp.dot(sre_ref[b, :, sl].astype(BF16), cm[:slab_state], preferred_element_type=F32)
            y = y + jnp.dot(sim_ref[b, :, sl].astype(BF16), cm[slab_state:], preferred_element_type=F32)
            ys.append(y)
        y = jnp.concatenate(ys, axis=-1) + d_ref[...] * u_ref[b]
        y = jax.nn.gelu(y)
        gate = jnp.dot(y.astype(BF16), wg_ref[...], preferred_element_type=F32) + bg_ref[...]
        y_ref[b] = y * jax.nn.sigmoid(gate)


def _s5_mixer(u, h0_re, h0_im, ab_re, ab_im, b_mat, c_mat, d_skip, w_glu, b_glu, *, tm, nb):
    bsz, t, w = u.shape
    assert bsz % nb == 0
    gn = ab_re.shape[-1]
    n_slabs = b_mat.shape[0]
    nt = pl.cdiv(t, tm)
    row = lambda b, i: (b, i, 0)
    st = lambda b, i: (b, 0, 0)
    c2 = lambda b, i: (0, 0)
    c3 = lambda b, i: (0, 0, 0)
    kern = functools.partial(_s5_kernel, t_valid=t, tm=tm, nb=nb, n_slabs=n_slabs,
                             slab_in=w // n_slabs, slab_state=gn // n_slabs)
    return pl.pallas_call(
        kern,
        grid=(bsz // nb, nt),
        in_specs=[
            pl.BlockSpec((nb, tm, w), row),
            pl.BlockSpec((nb, 1, gn), st),
            pl.BlockSpec((nb, 1, gn), st),
            pl.BlockSpec((1, gn), c2),
            pl.BlockSpec((1, gn), c2),
            pl.BlockSpec(b_mat.shape, c3),
            pl.BlockSpec(c_mat.shape, c3),
            pl.BlockSpec((1, w), c2),
            pl.BlockSpec((w, w), c2),
            pl.BlockSpec((1, w), c2),
        ],
        out_specs=[
            pl.BlockSpec((nb, tm, w), row),
            pl.BlockSpec((nb, 1, gn), st),
            pl.BlockSpec((nb, 1, gn), st),
        ],
        out_shape=[
            jax.ShapeDtypeStruct((bsz, t, w), F32),
            jax.ShapeDtypeStruct((bsz, 1, gn), F32),
            jax.ShapeDtypeStruct((bsz, 1, gn), F32),
        ],
        scratch_shapes=[pltpu.VMEM((nb, tm, gn), F32), pltpu.VMEM((nb, tm, gn), F32),
                        pltpu.VMEM((nb, 1, gn), F32), pltpu.VMEM((nb, 1, gn), F32)],
        compiler_params=_cparams(("arbitrary", "arbitrary")),
        name="s5_mixer",
    )(u, h0_re, h0_im, ab_re, ab_im, b_mat, c_mat, d_skip, w_glu, b_glu)


def _fox_prompt_kernel(kn_ref, bi_ref, qa_ref, qb_ref, ka_ref, kb_ref, va_ref, vb_ref, o_ref,
                       m_ref, acc_ref, qn_ref, g_ref, *, tq, tk, nk, n_heads):
    i = pl.program_id(2)
    heads = ((qa_ref, ka_ref, va_ref), (qb_ref, kb_ref, vb_ref))
    m_ref[...] = jnp.full_like(m_ref, -jnp.inf)
    acc_ref[...] = jnp.zeros_like(acc_ref)
    lane = lax.broadcasted_iota(jnp.int32, (tq, LANES), 1)
    for hh, (q_ref, _, _) in enumerate(heads):
        qf = jnp.where(lane < HEAD_DIM, q_ref[...].astype(F32), 0.0)
        qn_ref[hh] = jnp.sqrt(jnp.sum(qf * qf, axis=-1, keepdims=True)) * NORM_SLACK
    table = (pl.program_id(0) * n_heads + 2 * pl.program_id(1)) * nk

    def reaches(j):
        hit = False
        for hh in range(2):
            kn = kn_ref[table + hh * nk + j]
            bi = bi_ref[table + hh * nk + j]
            slack = jnp.min(m_ref[hh] - qn_ref[hh] * kn)
            hit = jnp.logical_or(hit, slack <= bi + UNDERFLOW_LOG2)
        return hit

    row0 = i * tq
    diag = row0 // tk

    def scores(j, masked):
        start = pl.multiple_of(j * tk, tk)
        out = []
        for q_ref, k_ref, _ in heads:
            g = lax.dot_general(q_ref[...], k_ref[pl.ds(start, tk), :], (((1,), (1,)), ((), ())),
                                preferred_element_type=F32)
            if masked:
                r = row0 + lax.broadcasted_iota(jnp.int32, (tq, tk), 0)
                c = start + lax.broadcasted_iota(jnp.int32, (tq, tk), 1)
                g = jnp.where(c <= r, g, -jnp.inf)
            out.append(g)
        return out

    def consume(j):
        start = pl.multiple_of(j * tk, tk)
        for hh, (_, _, v_ref) in enumerate(heads):
            g = g_ref[hh]
            m_old = m_ref[hh]
            m_new = jnp.maximum(m_old, jnp.max(g, axis=-1, keepdims=True))
            p = jnp.exp2(g - m_new)
            alpha = jnp.exp2(m_old - m_new)
            acc_ref[hh] = alpha * acc_ref[hh] + jnp.dot(p.astype(BF16), v_ref[pl.ds(start, tk), :],
                                                        preferred_element_type=F32)
            m_ref[hh] = m_new

    for hh, g in enumerate(scores(diag, True)):
        g_ref[hh] = g

    def body(c):
        t, _ = c
        cur = diag - t
        nxt = jnp.maximum(cur - 1, 0)
        consume(cur)
        g_next = scores(nxt, False)
        for hh, g in enumerate(g_next):
            g_ref[hh] = g
        return t + 1, jnp.logical_and(cur >= 1, reaches(nxt))

    lax.while_loop(lambda c: c[1], body, (jnp.int32(0), jnp.bool_(True)))
    outs = []
    for hh in range(2):
        acc = acc_ref[hh]
        outs.append(acc[:, :HEAD_DIM] / acc[:, HEAD_DIM:HEAD_DIM + 1])
    o_ref[...] = jnp.concatenate(outs, axis=-1)


def _fox_prompt(qh, kh, vh, kn_sq, bias_max, *, tq):
    bsz, nh, tp, wl = qh.shape
    nq = tp // tq
    nk = kn_sq.shape[1]
    tk = tp // nk
    assert tk % tq == 0
    stat = lambda a: lax.cummax(jnp.transpose(a[:, :, 0, :nh], (0, 2, 1)), axis=2).reshape(-1)
    kn_tab = stat(jnp.sqrt(kn_sq) * NORM_SLACK)
    bi_tab = stat(bias_max)
    qa = lambda b, p, i, kn, bi: (b, 2 * p, i, 0)
    qb = lambda b, p, i, kn, bi: (b, 2 * p + 1, i, 0)
    fa = lambda b, p, i, kn, bi: (b, 2 * p, 0, 0)
    fb = lambda b, p, i, kn, bi: (b, 2 * p + 1, 0, 0)
    q_spec = lambda im: pl.BlockSpec((None, None, tq, wl), im)
    kv_spec = lambda im: pl.BlockSpec((None, None, tp, wl), im)
    grid_spec = pltpu.PrefetchScalarGridSpec(
        num_scalar_prefetch=2,
        grid=(bsz, nh // 2, nq),
        in_specs=[q_spec(qa), q_spec(qb), kv_spec(fa), kv_spec(fb), kv_spec(fa), kv_spec(fb)],
        out_specs=pl.BlockSpec((None, tq, 2 * HEAD_DIM), lambda b, p, i, kn, bi: (b, i, p)),
        scratch_shapes=[pltpu.VMEM((2, tq, 1), F32), pltpu.VMEM((2, tq, wl), F32), pltpu.VMEM((2, tq, 1), F32),
                        pltpu.VMEM((2, tq, tk), F32)],
    )
    return pl.pallas_call(
        functools.partial(_fox_prompt_kernel, tq=tq, tk=tk, nk=nk, n_heads=nh),
        grid_spec=grid_spec,
        out_shape=jax.ShapeDtypeStruct((bsz, tp, nh * HEAD_DIM), F32),
        compiler_params=_cparams(("arbitrary", "arbitrary", "arbitrary")),
        name="fox_prompt",
    )(kn_tab, bi_tab, qh, qh, kh, kh, vh, vh)


def _fox_sample_kernel(q_ref, kn_ref, vn_ref, ck_ref, cv_ref, bc_ref, bn_ref, o_ref, *, n_heads):
    tq = q_ref.shape[1]
    r = lax.broadcasted_iota(jnp.int32, (tq, tq), 0)
    c = lax.broadcasted_iota(jnp.int32, (tq, tq), 1)
    outs = []
    dn = (((1,), (1,)), ((), ()))
    for hh in range(n_heads):
        sl = slice(hh * HEAD_DIM, (hh + 1) * HEAD_DIM)
        q = q_ref[hh][:, :HEAD_DIM]
        kn = kn_ref[hh][:, :HEAD_DIM]
        vn = vn_ref[hh][:, :HEAD_DIM]
        kc = ck_ref[:, sl].astype(BF16)
        vc = cv_ref[:, sl].astype(BF16)
        g_c = lax.dot_general(q, kc, dn, preferred_element_type=F32) + bc_ref[hh]
        g_n = lax.dot_general(q, kn, dn, preferred_element_type=F32) + bn_ref[hh]
        g_n = jnp.where(c <= r, g_n, -jnp.inf)
        m = jnp.maximum(jnp.max(g_c, axis=-1, keepdims=True), jnp.max(g_n, axis=-1, keepdims=True))
        p_c = jnp.exp2(g_c - m)
        p_n = jnp.exp2(g_n - m)
        den = jnp.sum(p_c, axis=-1, keepdims=True) + jnp.sum(p_n, axis=-1, keepdims=True)
        num = jnp.dot(p_c.astype(BF16), vc, preferred_element_type=F32)
        num = num + jnp.dot(p_n.astype(BF16), vn, preferred_element_type=F32)
        outs.append(num / den)
    o_ref[...] = jnp.concatenate(outs, axis=-1)


def _fox_sample(qh, kh, vh, cache_k, cache_v, bias_cache, bias_new, *, tq):
    _, nh, _, wl = qh.shape
    bsz, past, _ = cache_k.shape
    w = nh * HEAD_DIM
    b4 = lambda b: (b, 0, 0, 0)
    b3 = lambda b: (b, 0, 0)
    rows = lambda b: (0, 0, b, 0)
    return pl.pallas_call(
        functools.partial(_fox_sample_kernel, n_heads=nh),
        grid=(bsz,),
        in_specs=[pl.BlockSpec((None, nh, tq, wl), rows), pl.BlockSpec((None, nh, tq, wl), rows),
                  pl.BlockSpec((None, nh, tq, wl), rows),
                  pl.BlockSpec((None, past, w), b3), pl.BlockSpec((None, past, w), b3),
                  pl.BlockSpec((None, nh, 1, past), b4), pl.BlockSpec((None, nh, 1, tq), b4)],
        out_specs=pl.BlockSpec((None, tq, w), b3),
        out_shape=jax.ShapeDtypeStruct((bsz, tq, w), F32),
        compiler_params=_cparams(("arbitrary",)),
        name="fox_sample",
    )(qh, kh, vh, cache_k, cache_v, bias_cache, bias_new)


def _cumsum_kernel(x_ref, o_ref, carry_ref, *, tm):
    i = pl.program_id(1)

    @pl.when(i == 0)
    def _():
        carry_ref[...] = jnp.zeros_like(carry_ref)

    fc = _prefix_sum_rows(x_ref[...]) + carry_ref[0:1, :]
    o_ref[...] = fc
    carry_ref[0:1, :] = fc[tm - 1:tm, :]


def _cumsum_time(x, *, tm):
    bsz, t, w = x.shape
    row = lambda b, i: (b, i, 0)
    return pl.pallas_call(
        functools.partial(_cumsum_kernel, tm=tm),
        grid=(bsz, t // tm),
        in_specs=[pl.BlockSpec((None, tm, w), row)],
        out_specs=pl.BlockSpec((None, tm, w), row),
        out_shape=jax.ShapeDtypeStruct((bsz, t, w), F32),
        scratch_shapes=[pltpu.VMEM((8, w), F32)],
        compiler_params=_cparams(("arbitrary", "arbitrary")),
        name="cumsum_time",
    )(x)


def _merge_router_kernel(*refs, t_valid, tm, n_meta):
    n_x = 3 if n_meta else 1
    (ys_ref, ya_ref, gs_ref, ga_ref, wo_ref, gf_ref, wr_ref, br_ref, cnt_in_ref,
     x2_ref, ti_ref, tw_ref, tp_ref, cnt_out_ref, cnt_ref) = refs[n_x:]
    first = (pl.program_id(0) == 0) & (pl.program_id(1) == 0)

    @pl.when(first)
    def _():
        cnt_ref[...] = jnp.broadcast_to(cnt_in_ref[...], cnt_ref.shape)

    mix = jnp.concatenate([_rms(ys_ref[...], gs_ref[...]), _rms(ya_ref[...], ga_ref[...])], axis=-1)
    x = _timeline_rows(refs[:n_x], pl.program_id(1), n_meta)
    x2 = x + jnp.dot(mix.astype(BF16), wo_ref[...], preferred_element_type=F32)
    x2_ref[...] = x2
    h2 = _rms(x2, gf_ref[...])
    h_hi = h2.astype(BF16)
    h_lo = (h2 - h_hi.astype(F32)).astype(BF16)
    logits = (jnp.dot(h_hi, wr_ref[0], preferred_element_type=F32)
              + jnp.dot(h_lo, wr_ref[0], preferred_element_type=F32)
              + jnp.dot(h_hi, wr_ref[1], preferred_element_type=F32)) + br_ref[...]

    lane = lax.broadcasted_iota(jnp.int32, (tm, LANES), 1)
    rows = pl.program_id(1) * tm + lax.broadcasted_iota(jnp.int32, (tm, 1), 0)
    valid = rows < t_valid
    vals = logits
    top_v, top_i, sels = [], [], []
    for _ in range(TOP_K):
        mx = jnp.max(vals, axis=-1, keepdims=True)
        idx = jnp.min(jnp.where(vals == mx, lane, LANES), axis=-1, keepdims=True)
        sel = lane == idx
        vals = jnp.where(sel, -jnp.inf, vals)
        top_v.append(mx)
        top_i.append(idx)
        sels.append(sel)
    ex = [jnp.exp(v - top_v[0]) for v in top_v]
    den = ex[0] + ex[1] + ex[2] + ex[3]

    member = jnp.zeros((tm, LANES), F32)
    for sel in sels:
        member = member + jnp.where(valid, jnp.where(sel, 1.0, 0.0), 0.0)
    r = lax.broadcasted_iota(jnp.int32, (tm, tm), 0)
    c = lax.broadcasted_iota(jnp.int32, (tm, tm), 1)
    tri = (c < r).astype(BF16)
    before = jnp.dot(tri, member.astype(BF16), preferred_element_type=F32) + cnt_ref[0:1, :]
    cnt_new = cnt_ref[0:1, :] + jnp.sum(member, axis=0, keepdims=True)
    cnt_ref[0:1, :] = cnt_new
    cnt_out_ref[...] = cnt_new

    ti = jnp.zeros((tm, LANES), jnp.int32)
    tw = jnp.zeros((tm, LANES), F32)
    tpos = jnp.zeros((tm, LANES), F32)
    for kk in range(TOP_K):
        pos = jnp.sum(jnp.where(sels[kk], before, 0.0), axis=-1, keepdims=True)
        ti = jnp.where(lane == kk, top_i[kk], ti)
        tw = jnp.where(lane == kk, ex[kk] / den, tw)
        tpos = jnp.where(lane == kk, pos, tpos)
    ti_ref[...] = ti
    tw_ref[...] = tw
    tp_ref[...] = tpos.astype(jnp.int32)


def _merge_router(x, ys, ya, g_ssm, g_attn, w_out, g_ffn, w_router, b_router, cnt_in, *, tm, meta=None):
    bsz, _, d = x.shape
    x_ops, x_specs, t = _timeline_specs(x, meta, tm)
    w = ys.shape[-1]
    nt = pl.cdiv(t, tm)
    row = lambda b, i: (b, i, 0)
    c2 = lambda b, i: (0, 0)
    return pl.pallas_call(
        functools.partial(_merge_router_kernel, t_valid=t, tm=tm, n_meta=0 if meta is None else meta.shape[0]),
        grid=(bsz, nt),
        in_specs=x_specs + [
            pl.BlockSpec((None, tm, w), row),
            pl.BlockSpec((None, tm, w), row),
            pl.BlockSpec((1, w), c2), pl.BlockSpec((1, w), c2),
            pl.BlockSpec(w_out.shape, c2),
            pl.BlockSpec((1, d), c2),
            pl.BlockSpec(w_router.shape, lambda b, i: (0, 0, 0)),
            pl.BlockSpec((1, LANES), c2),
            pl.BlockSpec((1, LANES), c2),
        ],
        out_specs=[
            pl.BlockSpec((None, tm, d), row),
            pl.BlockSpec((None, tm, LANES), row),
            pl.BlockSpec((None, tm, LANES), row),
            pl.BlockSpec((None, tm, LANES), row),
            pl.BlockSpec((1, LANES), c2),
        ],
        out_shape=[
            jax.ShapeDtypeStruct((bsz, t, d), F32),
            jax.ShapeDtypeStruct((bsz, t, LANES), jnp.int32),
            jax.ShapeDtypeStruct((bsz, t, LANES), F32),
            jax.ShapeDtypeStruct((bsz, t, LANES), jnp.int32),
            jax.ShapeDtypeStruct((1, LANES), F32),
        ],
        scratch_shapes=[pltpu.VMEM((8, LANES), F32)],
        compiler_params=_cparams(("arbitrary", "arbitrary")),
        name="merge_router",
    )(*x_ops, ys, ya, g_ssm, g_attn, w_out, g_ffn, w_router, b_router, cnt_in)


def _dispatch_kernel(last_tile_ref, nu_ref, slots_ref, xa_ref, xb_ref, g_ref, xs_hbm, hbuf_ref, zero_ref, zsem, sem,
                     *, t_valid, nt_a, n_steps_a, tm, n_experts, n_tiles, tile):
    step = pl.program_id(0)

    def zero_tile(t):
        return pltpu.make_async_copy(zero_ref, xs_hbm.at[pl.ds(pl.multiple_of(t * tile, tile), tile)], zsem)

    @pl.when(step == 0)
    def _():
        zero_ref[...] = jnp.zeros_like(zero_ref)

        def start_tail(t, c):
            zero_tile(t).start()
            return c

        def wait_tail(t, c):
            zero_tile(t).wait()
            return c

        lax.fori_loop(nu_ref[0], n_tiles, start_tail, 0)
        lax.fori_loop(nu_ref[0], n_tiles, wait_tail, 0)
        for e in range(n_experts):
            lt = last_tile_ref[e]

            @pl.when(lt >= 0)
            def _():
                zero_tile(lt).start()
        for e in range(n_experts):
            lt = last_tile_ref[e]

            @pl.when(lt >= 0)
            def _():
                zero_tile(lt).wait()

    in_a = step < n_steps_a
    cur = step % 2

    @pl.when(in_a)
    def _():
        hbuf_ref[cur] = _rms(xa_ref[...], g_ref[...])

    @pl.when(jnp.logical_not(in_a))
    def _():
        hbuf_ref[cur] = _rms(xb_ref[...], g_ref[...])

    def issue(r, c):
        for kk in range(TOP_K):
            pltpu.make_async_copy(hbuf_ref.at[cur, pl.ds(r, 1)], xs_hbm.at[pl.ds(slots_ref[r * TOP_K + kk], 1)],
                                  sem.at[cur]).start()
        return c

    def valid_rows(s):
        return jnp.where(s < n_steps_a, jnp.minimum(tm, t_valid - (s % nt_a) * tm), tm)

    def wait_rows(buf, n_rows):
        @pl.when(n_rows == tm)
        def _():
            for _ in range(TOP_K):
                pltpu.make_async_copy(hbuf_ref.at[buf], xs_hbm.at[pl.ds(0, tm)], sem.at[buf]).wait()

        @pl.when(n_rows < tm)
        def _():
            def one(r, c):
                for _ in range(TOP_K):
                    pltpu.make_async_copy(hbuf_ref.at[buf, pl.ds(0, 1)], xs_hbm.at[pl.ds(0, 1)], sem.at[buf]).wait()
                return c
            lax.fori_loop(0, n_rows, one, 0)

    n_valid = valid_rows(step)

    @pl.when(n_valid == tm)
    def _():
        lax.fori_loop(0, tm, issue, 0, unroll=ISSUE_UNROLL)

    @pl.when(n_valid < tm)
    def _():
        lax.fori_loop(0, n_valid, issue, 0)

    @pl.when(step > 0)
    def _():
        wait_rows(1 - cur, valid_rows(step - 1))

    @pl.when(step == pl.num_programs(0) - 1)
    def _():
        wait_rows(cur, n_valid)


def _dispatch(xa, xb, g_ffn, slots_flat, last_tile, n_used, *, n_slots, tm):
    bsz, t, d = xa.shape
    nt_a = pl.cdiv(t, tm)
    n_steps_a = bsz * nt_a
    n_steps_b = xb.shape[0] // tm
    n_experts = last_tile.shape[0]
    grid_spec = pltpu.PrefetchScalarGridSpec(
        num_scalar_prefetch=2,
        grid=(n_steps_a + n_steps_b,),
        in_specs=[
            pl.BlockSpec((tm * TOP_K,), lambda s, lt, nu: (s,), memory_space=pltpu.SMEM),
            pl.BlockSpec((None, tm, d), lambda s, lt, nu: (jnp.minimum(s, n_steps_a - 1) // nt_a,
                                                          jnp.minimum(s, n_steps_a - 1) % nt_a, 0)),
            pl.BlockSpec((tm, d), lambda s, lt, nu: (jnp.maximum(s - n_steps_a, 0), 0)),
            pl.BlockSpec((1, d), lambda s, lt, nu: (0, 0)),
        ],
        out_specs=pl.BlockSpec(memory_space=pl.ANY),
        scratch_shapes=[pltpu.VMEM((2, tm, d), F32), pltpu.VMEM((EXPERT_TILE, d), F32),
                        pltpu.SemaphoreType.DMA, pltpu.SemaphoreType.DMA((2,))],
    )
    return pl.pallas_call(
        functools.partial(_dispatch_kernel, t_valid=t, nt_a=nt_a, n_steps_a=n_steps_a, tm=tm, n_experts=n_experts,
                          n_tiles=n_slots // EXPERT_TILE, tile=EXPERT_TILE),
        grid_spec=grid_spec,
        out_shape=jax.ShapeDtypeStruct((n_slots, d), F32),
        compiler_params=_cparams(("arbitrary",)),
        name="moe_dispatch",
    )(last_tile, n_used, slots_flat, xa, xb, g_ffn)


def _expert_kernel(te_ref, nu_ref, tpe_ref, x_ref, b1_ref, b2_ref, w1_hbm, w2_hbm, y_ref,
                   w1f_ref, w2f_ref, w1b_ref, w2b_ref, grp_ref, sem1, sem2, *, d_ff):
    i = pl.program_id(0)
    used = nu_ref[0]
    e = te_ref[i]
    prev = te_ref[jnp.maximum(i - 1, 0)]
    changed = (i == 0) | (e != prev)

    def fetch(expert, slot):
        return (pltpu.make_async_copy(w1_hbm.at[expert], w1f_ref.at[slot], sem1.at[slot]),
                pltpu.make_async_copy(w2_hbm.at[expert], w2f_ref.at[slot], sem2.at[slot]))

    @pl.when(i == 0)
    def _():
        grp_ref[0] = 0
        for cp in fetch(e, 0):
            cp.start()

    @pl.when((i < used) & changed)
    def _():
        slot = grp_ref[0] % 2
        for cp in fetch(e, slot):
            cp.wait()
        w1b_ref[...] = w1f_ref[slot].astype(BF16)
        w2b_ref[...] = w2f_ref[slot].astype(BF16)
        nxt = i + tpe_ref[e]

        @pl.when(nxt < used)
        def _():
            for cp in fetch(te_ref[nxt], 1 - slot):
                cp.start()
        grp_ref[0] = grp_ref[0] + 1

    @pl.when(i >= used)
    def _():
        y_ref[...] = jnp.zeros_like(y_ref)

    @pl.when(i < used)
    def _():
        x = x_ref[...].astype(BF16)
        h = jnp.dot(x, w1b_ref[...], preferred_element_type=F32) + b1_ref[...]
        hg = jnp.minimum(h[:, :d_ff], SWIGLU_LIMIT)
        hl = jnp.clip(h[:, d_ff:], -SWIGLU_LIMIT, SWIGLU_LIMIT)
        act = (hl + 1.0) * (hg * jax.nn.sigmoid(SWIGLU_ALPHA * hg))
        y_ref[...] = jnp.dot(act.astype(BF16), w2b_ref[...], preferred_element_type=F32) + b2_ref[...]


def _experts(xs, tile_expert, n_used, tiles_per, w1, b1, w2, b2):
    s, d = xs.shape
    n_tiles = s // EXPERT_TILE
    n_e, _, two_ff = w1.shape
    d_ff = two_ff // 2

    def xmap(i, te, nu, tpe):
        return (i, 0)

    def bmap(i, te, nu, tpe):
        return (te[i], 0, 0)

    grid_spec = pltpu.PrefetchScalarGridSpec(
        num_scalar_prefetch=3,
        grid=(n_tiles,),
        in_specs=[
            pl.BlockSpec((EXPERT_TILE, d), xmap),
            pl.BlockSpec((None, 1, two_ff), bmap),
            pl.BlockSpec((None, 1, d), bmap),
            pl.BlockSpec(memory_space=pl.ANY),
            pl.BlockSpec(memory_space=pl.ANY),
        ],
        out_specs=pl.BlockSpec((EXPERT_TILE, d), xmap),
        scratch_shapes=[pltpu.VMEM((2, d, two_ff), F32), pltpu.VMEM((2, d_ff, d), F32),
                        pltpu.VMEM((d, two_ff), BF16), pltpu.VMEM((d_ff, d), BF16),
                        pltpu.SMEM((1,), jnp.int32),
                        pltpu.SemaphoreType.DMA((2,)), pltpu.SemaphoreType.DMA((2,))],
    )
    return pl.pallas_call(
        functools.partial(_expert_kernel, d_ff=d_ff),
        grid_spec=grid_spec,
        out_shape=jax.ShapeDtypeStruct((s, d), F32),
        compiler_params=_cparams(("arbitrary",)),
        name="moe_experts",
    )(tile_expert, n_used, tiles_per, xs, b1.reshape(n_e, 1, two_ff), b2.reshape(n_e, 1, d), w1, w2)


def _combine_kernel(slots_ref, next_slots_ref, g_ref, x2_hbm, tw_hbm, ys_hbm, o_ref, xbuf_ref, wbuf_ref, buf_ref,
                    sem, lsem, *, tm, nt, row_offset):
    step = pl.program_id(0)
    cur = step % 2

    def tile_copies(s, slot):
        start = pl.multiple_of(row_offset + (s % nt) * tm, 8)
        return (pltpu.make_async_copy(x2_hbm.at[s // nt, pl.ds(start, tm)], xbuf_ref.at[slot], lsem.at[slot]),
                pltpu.make_async_copy(tw_hbm.at[s // nt, pl.ds(start, tm)], wbuf_ref.at[slot], lsem.at[slot]))

    def fetch(s, slot, idx_ref):
        for cp in tile_copies(s, slot):
            cp.start()

        def issue(r, c):
            for kk in range(TOP_K):
                pltpu.make_async_copy(ys_hbm.at[pl.ds(idx_ref[r * TOP_K + kk], 1)],
                                      buf_ref.at[slot, kk, pl.ds(r, 1)], sem.at[slot]).start()
            return c

        lax.fori_loop(0, tm, issue, 0, unroll=ISSUE_UNROLL)

    @pl.when(step == 0)
    def _():
        fetch(step, cur, slots_ref)

    @pl.when(step + 1 < pl.num_programs(0))
    def _():
        fetch(step + 1, 1 - cur, next_slots_ref)

    for cp in tile_copies(step, cur):
        cp.wait()
    for kk in range(TOP_K):
        pltpu.make_async_copy(ys_hbm.at[pl.ds(0, tm)], buf_ref.at[cur, kk], sem.at[cur]).wait()
    tw = wbuf_ref[cur]
    acc = xbuf_ref[cur]
    for kk in range(TOP_K):
        acc = acc + tw[:, kk:kk + 1] * buf_ref[cur, kk]
    o_ref[...] = _rms(acc, g_ref[...])


def _combine(x2, tw, slots_flat, ys, g_final, *, tm, row_offset):
    bsz, t, d = x2.shape
    t_out = t - row_offset
    nt = t_out // tm
    n_steps = bsz * nt
    grid_spec = pltpu.PrefetchScalarGridSpec(
        num_scalar_prefetch=0,
        grid=(n_steps,),
        in_specs=[
            pl.BlockSpec((tm * TOP_K,), lambda s: (s,), memory_space=pltpu.SMEM),
            pl.BlockSpec((tm * TOP_K,), lambda s: (jnp.minimum(s + 1, n_steps - 1),), memory_space=pltpu.SMEM),
            pl.BlockSpec((1, d), lambda s: (0, 0)),
            pl.BlockSpec(memory_space=pl.ANY),
            pl.BlockSpec(memory_space=pl.ANY),
            pl.BlockSpec(memory_space=pl.ANY),
        ],
        out_specs=pl.BlockSpec((None, tm, d), lambda s: (s // nt, s % nt, 0)),
        scratch_shapes=[pltpu.VMEM((2, tm, d), F32), pltpu.VMEM((2, tm, LANES), F32),
                        pltpu.VMEM((2, TOP_K, tm, d), F32),
                        pltpu.SemaphoreType.DMA((2,)), pltpu.SemaphoreType.DMA((2,))],
    )
    return pl.pallas_call(
        functools.partial(_combine_kernel, tm=tm, nt=nt, row_offset=row_offset),
        grid_spec=grid_spec,
        out_shape=jax.ShapeDtypeStruct((bsz, t_out, d), F32),
        compiler_params=_cparams(("arbitrary",)),
        name="moe_combine",
    )(slots_flat, slots_flat, g_final, x2, tw, ys)


def _pad_lanes(a, value=0.0):
    return jnp.pad(a, [(0, 0)] * (a.ndim - 1) + [(0, LANES - a.shape[-1])], constant_values=value)


def kernel(x_prompt, x_sample, cache_k, cache_v, cache_logf, state_ssm_re, state_ssm_im, meta_tokens, norm_mix_g, w_in, b_forget, ssm_a_re, ssm_a_im, ssm_log_dt, ssm_b_re, ssm_b_im, ssm_c_re, ssm_c_im, ssm_d, w_glu, b_glu, g_out_ssm, g_out_attn, w_out, norm_ffn_g, w_router, b_router, w_mlp1, b_mlp1, w_mlp2, b_mlp2, norm_final_g):
    depth = w_in.shape[0]
    assert depth == 1, "the routing tables below are built for a single trunk layer"
    l = 0
    bp, seq, d = x_prompt.shape
    bs, ts, _ = x_sample.shape
    n_heads = b_forget.shape[1]
    n_groups, ssm_n = ssm_a_re.shape[1:]
    n_experts = w_router.shape[2]
    w_ssm = n_groups * SSM_P
    w_attn = n_heads * HEAD_DIM
    gn = n_groups * ssm_n
    past = cache_k.shape[2]
    assert seq % ROW_TILE == 0 and (bs * ts) % ROW_TILE == 0

    meta = meta_tokens.astype(x_prompt.dtype)
    assert meta.shape[0] == N_META
    xs = x_sample
    tp = seq + N_META

    w_main = w_in[l][:, :w_ssm + 3 * w_attn].astype(BF16)
    w_f = _pad_lanes(w_in[l][:, w_ssm + 3 * w_attn:]).astype(BF16)
    b_f = _pad_lanes(b_forget[l][None, :])
    g_mix = norm_mix_g[l][None, :]
    ab_re, ab_im, bb_re, bb_im = _s5_params(ssm_a_re[l], ssm_a_im[l], ssm_log_dt[l], ssm_b_re[l], ssm_b_im[l])
    b_mat, c_mat = _s5_block_mats(bb_re, bb_im, ssm_c_re[l], ssm_c_im[l], n_groups)

    proj = functools.partial(_in_proj, n_heads=n_heads, w_ssm=w_ssm, w_attn=w_attn)
    up, kp, vp, lfp, qhp, khp, vhp, knp, bip = proj(x_prompt, g_mix, w_main, w_f, b_f, tm=ROW_TILE, meta=meta)
    n_s = bs * ts
    us, ks, vs, lfs, qhs, khs, vhs, _, _ = proj(xs.reshape(1, n_s, d), g_mix, w_main, w_f, b_f, tm=n_s)
    us = us.reshape(bs, ts, w_ssm)
    lfs = lfs.reshape(bs, ts, n_heads)

    s5 = functools.partial(_s5_mixer, ab_re=ab_re, ab_im=ab_im, b_mat=b_mat, c_mat=c_mat,
                           d_skip=ssm_d[l].reshape(1, w_ssm), w_glu=w_glu[l].astype(BF16), b_glu=b_glu[l][None, :])
    zeros_state = jnp.zeros((bp, 1, gn), F32)
    ysp, hrp, hip = s5(up, zeros_state, zeros_state, tm=ROW_TILE, nb=bp)
    yss, hrs, his = s5(us, state_ssm_re[l].reshape(bs, 1, gn), state_ssm_im[l].reshape(bs, 1, gn), tm=ts, nb=1)

    yap = _fox_prompt(qhp, khp, vhp, knp, bip, tq=ATTN_Q_TILE)
    t_all = past + ts
    assert bs * n_heads == LANES
    tm_cum = max(m for m in range(8, ROW_TILE + 1, 8) if t_all % m == 0)
    lf_all = jnp.concatenate([cache_logf[l].astype(F32), lfs], axis=1)
    fcs = _cumsum_time(jnp.transpose(lf_all, (1, 0, 2)).reshape(1, t_all, LANES), tm=tm_cum)
    bias_s = -LOG2E * jnp.transpose(fcs.reshape(t_all, bs, n_heads), (1, 2, 0))[:, :, None, :]
    yas = _fox_sample(qhs, khs, vhs, cache_k[l].reshape(bs, past, w_attn), cache_v[l].reshape(bs, past, w_attn),
                      bias_s[..., :past], bias_s[..., past:], tq=ts)

    wr = _pad_lanes(w_router[l])
    wr_hi = wr.astype(BF16)
    wr_parts = jnp.stack([wr_hi, (wr - wr_hi.astype(F32)).astype(BF16)])
    merge = functools.partial(_merge_router, g_ssm=g_out_ssm[l][None, :], g_attn=g_out_attn[l][None, :],
                              w_out=w_out[l].astype(BF16), g_ffn=norm_ffn_g[l][None, :],
                              w_router=wr_parts, b_router=_pad_lanes(b_router[l][None, :], value=-1e30))
    x2p, tip, twp, tpp, cnt_p = merge(x_prompt, ysp, yap, cnt_in=jnp.zeros((1, LANES), F32), tm=ROW_TILE, meta=meta)
    x2s, tis, tws, tps, cnt = merge(xs.reshape(1, n_s, d), yss.reshape(1, n_s, w_ssm), yas.reshape(1, n_s, w_attn),
                                    cnt_in=cnt_p, tm=n_s)

    n_tok = bp * tp + bs * ts
    counts = cnt[0, :n_experts].astype(jnp.int32)
    tiles_per = (counts + EXPERT_TILE - 1) // EXPERT_TILE
    tile_end = jnp.cumsum(tiles_per)
    tile_start = tile_end - tiles_per
    n_tiles = (n_tok * TOP_K) // EXPERT_TILE + n_experts
    n_slots = n_tiles * EXPERT_TILE
    n_used = tile_end[-1:].astype(jnp.int32)
    tile_ids = jnp.minimum(jnp.arange(n_tiles, dtype=jnp.int32), n_used - 1)
    tile_expert = jnp.sum((tile_end[None, :] <= tile_ids[:, None]).astype(jnp.int32), axis=1)
    last_tile = jnp.where(tiles_per > 0, tile_end - 1, -1).astype(jnp.int32)
    slot_base = (tile_start * EXPERT_TILE).astype(F32)

    def slots_of(ids, pos):
        onehot = jax.nn.one_hot(ids[..., :TOP_K], n_experts, dtype=F32)
        base = jnp.einsum('btke,e->btk', onehot, slot_base, precision=lax.Precision.HIGHEST)
        return base.astype(jnp.int32) + pos[..., :TOP_K]

    slots_p = slots_of(tip, tpp)
    slots_s = slots_of(tis, tps)
    tpad = pl.cdiv(tp, ROW_TILE) * ROW_TILE
    slots_disp = jnp.concatenate([jnp.pad(slots_p, ((0, 0), (0, tpad - tp), (0, 0))).reshape(-1),
                                  slots_s.reshape(-1)])
    xs_sorted = _dispatch(x2p, x2s.reshape(bs * ts, d), norm_ffn_g[l][None, :], slots_disp, last_tile, n_used,
                          n_slots=n_slots, tm=ROW_TILE)
    ys_sorted = _experts(xs_sorted, tile_expert, n_used, tiles_per.astype(jnp.int32),
                         w_mlp1[l], b_mlp1[l], w_mlp2[l], b_mlp2[l])

    g_fin = norm_final_g[None, :]
    y_prompt = _combine(x2p, twp, slots_p[:, N_META:].reshape(-1), ys_sorted, g_fin, tm=ROW_TILE, row_offset=N_META)
    y_sample = _combine(x2s.reshape(1, bs * ts, d), tws.reshape(1, bs * ts, LANES), slots_s.reshape(-1), ys_sorted,
                        g_fin, tm=ROW_TILE, row_offset=0).reshape(bs, ts, d)

    hd = HEAD_DIM
    st = lambda a, b: a.reshape(1, b, n_groups, ssm_n)
    return (y_prompt, y_sample,
            kp.reshape(1, bp, tp, n_heads, hd), vp.reshape(1, bp, tp, n_heads, hd), lfp[None],
            st(hrp, bp), st(hip, bp),
            ks.reshape(1, bs, ts, n_heads, hd), vs.reshape(1, bs, ts, n_heads, hd), lfs[None],
            st(hrs, bs), st(his, bs))
```

```python
import functools
import math

import numpy as np
import jax
import jax.numpy as jnp
from jax import lax
from jax.experimental import pallas as pl
from jax.experimental.pallas import tpu as pltpu

F32 = jnp.float32
BF16 = jnp.bfloat16

LANES = 128
VMEM_LIMIT_BYTES = 56 * 1024 * 1024

N_META = 16
HEAD_DIM = 64
SSM_P = 16
SSM_N = 64
TOP_K = 4
SWIGLU_LIMIT = 7.0
SWIGLU_ALPHA = 1.702
RMS_EPS = 1e-6
LAMBDA_RE_MAX = -1e-4
LOG2E = math.log2(math.e)

ROW_TILE = 512
ATTN_Q_TILE = 512
ATTN_HEADS_PER_STEP = 4
EXPERT_TILE = 512
GROUP_SLAB = 8
BIAS_TERMS = 3
ISSUE_UNROLL = 8
UNDERFLOW_LOG2 = 152.0
NORM_SLACK = 1.001


def _cparams(sem):
    return pltpu.CompilerParams(dimension_semantics=sem, vmem_limit_bytes=VMEM_LIMIT_BYTES)


def _rms(v, g):
    return v * lax.rsqrt(jnp.mean(v * v, axis=-1, keepdims=True) + RMS_EPS) * g


def _timeline_rows(x_refs, i, n_meta):
    if not n_meta:
        return x_refs[0][...]
    meta_ref, prev_ref, cur_ref = x_refs
    head = jnp.where(i == 0, meta_ref[...], prev_ref[...])
    return jnp.concatenate([head, cur_ref[...][:cur_ref.shape[0] - n_meta]], axis=0)


def _timeline_specs(x, meta, tm):
    bsz, t_in, d = x.shape
    if meta is None:
        return [x], [pl.BlockSpec((None, tm, d), lambda b, i: (b, i, 0))], t_in
    n_meta = meta.shape[0]
    assert t_in % tm == 0 and tm % n_meta == 0
    per = tm // n_meta
    last = t_in // tm - 1
    specs = [pl.BlockSpec((n_meta, d), lambda b, i: (0, 0)),
             pl.BlockSpec((None, n_meta, d), lambda b, i: (b, jnp.maximum(i * per - 1, 0), 0)),
             pl.BlockSpec((None, tm, d), lambda b, i: (b, jnp.minimum(i, last), 0))]
    return [meta, x, x], specs, t_in + n_meta


def _split_bf16(a):
    pieces = []
    rest = a
    for _ in range(BIAS_TERMS):
        piece = rest.astype(BF16)
        pieces.append(piece)
        rest = rest - piece.astype(F32)
    return jnp.concatenate(pieces, axis=-1)


def _prefix_sum_rows(x):
    tm = x.shape[0]
    r = lax.broadcasted_iota(jnp.int32, (tm, tm), 0)
    c = lax.broadcasted_iota(jnp.int32, (tm, tm), 1)
    tri = jnp.where(c <= r, 1.0, 0.0).astype(BF16)
    s = jnp.dot(tri, _split_bf16(x), preferred_element_type=F32)
    return sum(s[:, t * LANES:(t + 1) * LANES] for t in range(BIAS_TERMS))


def _in_proj_kernel(*refs, t_valid, tm, n_heads, w_ssm, w_attn, n_meta):
    n_x = 3 if n_meta else 1
    (g_ref, w_ref, wf_ref, bf_ref, sel_ref,
     u_ref, k_ref, v_ref, lf_ref, qh_ref, kh_ref, vh_ref, kn_ref, bi_ref, carry_ref) = refs[n_x:]
    i = pl.program_id(1)

    @pl.when(i == 0)
    def _():
        carry_ref[...] = jnp.zeros_like(carry_ref)

    x = _timeline_rows(refs[:n_x], i, n_meta)
    h = _rms(x, g_ref[...]).astype(BF16)
    rows = i * tm + lax.broadcasted_iota(jnp.int32, (tm, 1), 0)
    valid = rows < t_valid
    z = jnp.where(valid, jnp.dot(h, w_ref[...], preferred_element_type=F32), 0.0)
    u_ref[...] = z[:, :w_ssm]
    k_ref[...] = z[:, w_ssm + w_attn:w_ssm + 2 * w_attn]
    v_ref[...] = z[:, w_ssm + 2 * w_attn:w_ssm + 3 * w_attn]

    zf = jnp.dot(h, wf_ref[...], preferred_element_type=F32) + bf_ref[...]
    lf = jnp.where(valid, jnp.minimum(zf, 0.0) - jnp.log1p(jnp.exp(-jnp.abs(zf))), 0.0)
    lf_ref[...] = lf[:, :n_heads]
    fc = _prefix_sum_rows(lf) + carry_ref[0:1, :]
    carry_ref[0:1, :] = fc[tm - 1:tm, :]

    bias = -LOG2E * fc
    bi_ref[...] = jnp.max(bias, axis=0, keepdims=True)
    tails = jnp.dot(_split_bf16(bias), sel_ref[...], preferred_element_type=F32)

    lane = lax.broadcasted_iota(jnp.int32, (tm, LANES), 1)
    lane1 = lax.broadcasted_iota(jnp.int32, (1, LANES), 1)
    low = lane < HEAD_DIM
    in_tail = (lane >= HEAD_DIM) & (lane < HEAD_DIM + BIAS_TERMS)
    q_tail = jnp.where(in_tail, 1.0, 0.0)
    v_tail = jnp.where(lane == HEAD_DIM, 1.0, 0.0)
    kn = jnp.zeros((1, LANES), F32)
    for pair in range(n_heads // 2):
        sl = slice(pair * LANES, (pair + 1) * LANES)
        qt = z[:, w_ssm:w_ssm + w_attn][:, sl] * (LOG2E * HEAD_DIM ** -0.5)
        kt = z[:, w_ssm + w_attn:w_ssm + 2 * w_attn][:, sl]
        vt = z[:, w_ssm + 2 * w_attn:w_ssm + 3 * w_attn][:, sl]
        for half in range(2):
            hh = 2 * pair + half
            if half:
                qt, kt, vt = (pltpu.roll(a, HEAD_DIM, 1) for a in (qt, kt, vt))
            k_tail = jnp.where(in_tail, pltpu.roll(tails, HEAD_DIM - BIAS_TERMS * hh, 1), 0.0)
            kb = jnp.where(low, kt, k_tail).astype(BF16)
            qh_ref[hh] = jnp.where(low, qt, q_tail).astype(BF16)
            kh_ref[hh] = kb
            vh_ref[hh] = jnp.where(low, vt, v_tail).astype(BF16)
            kf = jnp.where(low, kb.astype(F32), 0.0)
            ksq = jnp.max(jnp.sum(kf * kf, axis=-1, keepdims=True), axis=0, keepdims=True)
            kn = jnp.where(lane1 == hh, ksq, kn)
    kn_ref[...] = kn


def _bias_selector(n_heads):
    sel = np.zeros((BIAS_TERMS * LANES, LANES), np.float32)
    for hh in range(n_heads):
        for c in range(BIAS_TERMS):
            sel[c * LANES + hh, BIAS_TERMS * hh + c] = 1.0
    return jnp.asarray(sel, BF16)


def _in_proj(x, g, w_main, w_f, b_f, *, n_heads, w_ssm, w_attn, tm, meta=None):
    bsz, _, d = x.shape
    x_ops, x_specs, t = _timeline_specs(x, meta, tm)
    nt = pl.cdiv(t, tm)
    tp = nt * tm
    row = lambda b, i: (b, i, 0)
    const = lambda b, i: (0, 0)
    head = lambda b, i: (b, 0, i, 0)
    tile = lambda b, i: (b, i, 0, 0)
    sel = _bias_selector(n_heads)
    kern = functools.partial(_in_proj_kernel, t_valid=t, tm=tm, n_heads=n_heads, w_ssm=w_ssm, w_attn=w_attn,
                             n_meta=0 if meta is None else meta.shape[0])
    return pl.pallas_call(
        kern,
        grid=(bsz, nt),
        in_specs=x_specs + [
            pl.BlockSpec((1, d), const),
            pl.BlockSpec(w_main.shape, const),
            pl.BlockSpec(w_f.shape, const),
            pl.BlockSpec((1, LANES), const),
            pl.BlockSpec(sel.shape, const),
        ],
        out_specs=[
            pl.BlockSpec((None, tm, w_ssm), row),
            pl.BlockSpec((None, tm, w_attn), row),
            pl.BlockSpec((None, tm, w_attn), row),
            pl.BlockSpec((None, tm, n_heads), row),
            pl.BlockSpec((None, n_heads, tm, LANES), head),
            pl.BlockSpec((None, n_heads, tm, LANES), head),
            pl.BlockSpec((None, n_heads, tm, LANES), head),
            pl.BlockSpec((None, None, 1, LANES), tile),
            pl.BlockSpec((None, None, 1, LANES), tile),
        ],
        out_shape=[
            jax.ShapeDtypeStruct((bsz, t, w_ssm), F32),
            jax.ShapeDtypeStruct((bsz, t, w_attn), F32),
            jax.ShapeDtypeStruct((bsz, t, w_attn), F32),
            jax.ShapeDtypeStruct((bsz, t, n_heads), F32),
            jax.ShapeDtypeStruct((bsz, n_heads, tp, LANES), BF16),
            jax.ShapeDtypeStruct((bsz, n_heads, tp, LANES), BF16),
            jax.ShapeDtypeStruct((bsz, n_heads, tp, LANES), BF16),
            jax.ShapeDtypeStruct((bsz, nt, 1, LANES), F32),
            jax.ShapeDtypeStruct((bsz, nt, 1, LANES), F32),
        ],
        scratch_shapes=[pltpu.VMEM((8, LANES), F32)],
        compiler_params=_cparams(("arbitrary", "arbitrary")),
        name="in_proj",
    )(*x_ops, g, w_main, w_f, b_f, sel)


def _s5_param_kernel(are_ref, aim_ref, ldt_ref, bre_ref, bim_ref, abr_ref, abi_ref, bbr_ref, bbi_ref):
    lam_re = jnp.minimum(are_ref[...], LAMBDA_RE_MAX)
    lam_im = aim_ref[...]
    dt = jnp.exp(ldt_ref[...])
    mag = jnp.exp(lam_re * dt)
    ab_re = mag * jnp.cos(lam_im * dt)
    ab_im = mag * jnp.sin(lam_im * dt)
    abr_ref[...] = ab_re
    abi_ref[...] = ab_im
    nr = ab_re - 1.0
    ni = ab_im
    den = lam_re * lam_re + lam_im * lam_im
    cr = (nr * lam_re + ni * lam_im) / den
    ci = (ni * lam_re - nr * lam_im) / den
    b_re = bre_ref[...]
    b_im = bim_ref[...]
    bbr_ref[...] = cr * b_re - ci * b_im
    bbi_ref[...] = cr * b_im + ci * b_re


def _s5_params(a_re, a_im, log_dt, b_re, b_im):
    g, n = a_re.shape
    p = b_re.shape[-1]
    gn = g * n
    flat = lambda a: a.reshape(1, gn)
    ldt = jnp.broadcast_to(log_dt[:, None], (g, n)).reshape(1, gn)
    bt = lambda b: jnp.transpose(b, (2, 0, 1)).reshape(p, gn)
    return pl.pallas_call(
        _s5_param_kernel,
        out_shape=[jax.ShapeDtypeStruct((1, gn), F32), jax.ShapeDtypeStruct((1, gn), F32),
                   jax.ShapeDtypeStruct((p, gn), F32), jax.ShapeDtypeStruct((p, gn), F32)],
        name="s5_params",
    )(flat(a_re), flat(a_im), ldt, bt(b_re), bt(b_im))


def _s5_block_mats(bb_re, bb_im, c_re, c_im, n_groups):
    p = bb_re.shape[0]
    n = bb_re.shape[1] // n_groups
    s = n_groups // GROUP_SLAB
    eye = jnp.eye(GROUP_SLAB, dtype=F32)

    def in_blk(bb):
        b4 = bb.reshape(p, s, GROUP_SLAB, n)
        return jnp.einsum('qsgn,gh->sgqhn', b4, eye)

    b_mat = jnp.concatenate([in_blk(bb_re).reshape(s, GROUP_SLAB * p, GROUP_SLAB * n),
                             in_blk(bb_im).reshape(s, GROUP_SLAB * p, GROUP_SLAB * n)], axis=-1)

    def out_blk(c):
        c4 = c.reshape(s, GROUP_SLAB, p, n)
        return jnp.einsum('sgpn,gh->sgnhp', c4, eye)

    c_mat = jnp.concatenate([out_blk(c_re).reshape(s, GROUP_SLAB * n, GROUP_SLAB * p),
                             out_blk(-c_im).reshape(s, GROUP_SLAB * n, GROUP_SLAB * p)], axis=1)
    return b_mat.astype(BF16), c_mat.astype(BF16)


def _s5_kernel(u_ref, h0r_ref, h0i_ref, abr_ref, abi_ref, bm_ref, cm_ref, d_ref, wg_ref, bg_ref,
               y_ref, hr_out_ref, hi_out_ref, sre_ref, sim_ref, hre_ref, him_ref,
               *, t_valid, tm, nb, n_slabs, slab_in, slab_state):
    i = pl.program_id(1)
    nt = pl.num_programs(1)

    @pl.when(i == 0)
    def _():
        hre_ref[...] = h0r_ref[...]
        him_ref[...] = h0i_ref[...]

    for b in range(nb):
        ub = u_ref[b].astype(BF16)
        for s in range(n_slabs):
            z = jnp.dot(ub[:, s * slab_in:(s + 1) * slab_in], bm_ref[s], preferred_element_type=F32)
            sre_ref[b, :, s * slab_state:(s + 1) * slab_state] = z[:, :slab_state]
            sim_ref[b, :, s * slab_state:(s + 1) * slab_state] = z[:, slab_state:]

    a_re = abr_ref[...]
    a_im = abi_ref[...]
    last_row = (t_valid - 1) % tm

    def step(t, carry):
        out = []
        for b in range(nb):
            h_re, h_im = carry[2 * b], carry[2 * b + 1]
            b_re = sre_ref[b, pl.ds(t, 1), :]
            b_im = sim_ref[b, pl.ds(t, 1), :]
            n_re = a_re * h_re - a_im * h_im + b_re
            n_im = a_re * h_im + a_im * h_re + b_im
            sre_ref[b, pl.ds(t, 1), :] = n_re
            sim_ref[b, pl.ds(t, 1), :] = n_im
            out += [n_re, n_im]
        return tuple(out)

    init = tuple(ref[b] for b in range(nb) for ref in (hre_ref, him_ref))
    fin = lax.fori_loop(0, tm, step, init)
    for b in range(nb):
        hre_ref[b] = fin[2 * b]
        him_ref[b] = fin[2 * b + 1]

    @pl.when(i == nt - 1)
    def _():
        for b in range(nb):
            hr_out_ref[b] = sre_ref[b, last_row:last_row + 1, :]
            hi_out_ref[b] = sim_ref[b, last_row:last_row + 1, :]

    for b in range(nb):
        ys = []
        for s in range(n_slabs):
            sl = slice(s * slab_state, (s + 1) * slab_state)
            cm = cm_ref[s]
            y = jnp.dot(sre_ref[b, :, sl].astype(BF16), cm[:slab_state], preferred_element_type=F32)
            y = y + jnp.dot(sim_ref[b, :, sl].astype(BF16), cm[slab_state:], preferred_element_type=F32)
            ys.append(y)
        y = jnp.concatenate(ys, axis=-1) + d_ref[...] * u_ref[b]
        y = jax.nn.gelu(y)
        gate = jnp.dot(y.astype(BF16), wg_ref[...], preferred_element_type=F32) + bg_ref[...]
        y_ref[b] = y * jax.nn.sigmoid(gate)


def _s5_mixer(u, h0_re, h0_im, ab_re, ab_im, b_mat, c_mat, d_skip, w_glu, b_glu, *, tm, nb):
    bsz, t, w = u.shape
    assert bsz % nb == 0
    gn = ab_re.shape[-1]
    n_slabs = b_mat.shape[0]
    nt = pl.cdiv(t, tm)
    row = lambda b, i: (b, i, 0)
    st = lambda b, i: (b, 0, 0)
    c2 = lambda b, i: (0, 0)
    c3 = lambda b, i: (0, 0, 0)
    kern = functools.partial(_s5_kernel, t_valid=t, tm=tm, nb=nb, n_slabs=n_slabs,
                             slab_in=w // n_slabs, slab_state=gn // n_slabs)
    return pl.pallas_call(
        kern,
        grid=(bsz // nb, nt),
        in_specs=[
            pl.BlockSpec((nb, tm, w), row),
            pl.BlockSpec((nb, 1, gn), st),
            pl.BlockSpec((nb, 1, gn), st),
            pl.BlockSpec((1, gn), c2),
            pl.BlockSpec((1, gn), c2),
            pl.BlockSpec(b_mat.shape, c3),
            pl.BlockSpec(c_mat.shape, c3),
            pl.BlockSpec((1, w), c2),
            pl.BlockSpec((w, w), c2),
            pl.BlockSpec((1, w), c2),
        ],
        out_specs=[
            pl.BlockSpec((nb, tm, w), row),
            pl.BlockSpec((nb, 1, gn), st),
            pl.BlockSpec((nb, 1, gn), st),
        ],
        out_shape=[
            jax.ShapeDtypeStruct((bsz, t, w), F32),
            jax.ShapeDtypeStruct((bsz, 1, gn), F32),
            jax.ShapeDtypeStruct((bsz, 1, gn), F32),
        ],
        scratch_shapes=[pltpu.VMEM((nb, tm, gn), F32), pltpu.VMEM((nb, tm, gn), F32),
                        pltpu.VMEM((nb, 1, gn), F32), pltpu.VMEM((nb, 1, gn), F32)],
        compiler_params=_cparams(("arbitrary", "arbitrary")),
        name="s5_mixer",
    )(u, h0_re, h0_im, ab_re, ab_im, b_mat, c_mat, d_skip, w_glu, b_glu)


def _fox_prompt_kernel(kn_ref, bi_ref, *refs, tq, tk, nk, n_heads, group):
    i = pl.program_id(2)
    heads = tuple(zip(refs[:group], refs[group:2 * group], refs[2 * group:3 * group]))
    o_ref, m_ref, acc_ref, qn_ref, g_ref = refs[3 * group:]
    m_ref[...] = jnp.full_like(m_ref, -jnp.inf)
    acc_ref[...] = jnp.zeros_like(acc_ref)
    lane = lax.broadcasted_iota(jnp.int32, (tq, LANES), 1)
    for hh, (q_ref, _, _) in enumerate(heads):
        qf = jnp.where(lane < HEAD_DIM, q_ref[...].astype(F32), 0.0)
        qn_ref[hh] = jnp.sqrt(jnp.sum(qf * qf, axis=-1, keepdims=True)) * NORM_SLACK
    table = (pl.program_id(0) * n_heads + group * pl.program_id(1)) * nk

    def reaches(j):
        hit = False
        for hh in range(group):
            kn = kn_ref[table + hh * nk + j]
            bi = bi_ref[table + hh * nk + j]
            slack = jnp.min(m_ref[hh] - qn_ref[hh] * kn)
            hit = jnp.logical_or(hit, slack <= bi + UNDERFLOW_LOG2)
        return hit

    row0 = i * tq
    diag = row0 // tk

    def scores(j, masked):
        start = pl.multiple_of(j * tk, tk)
        out = []
        for q_ref, k_ref, _ in heads:
            g = lax.dot_general(q_ref[...], k_ref[pl.ds(start, tk), :], (((1,), (1,)), ((), ())),
                                preferred_element_type=F32)
            if masked:
                r = row0 + lax.broadcasted_iota(jnp.int32, (tq, tk), 0)
                c = start + lax.broadcasted_iota(jnp.int32, (tq, tk), 1)
                g = jnp.where(c <= r, g, -jnp.inf)
            out.append(g)
        return out

    def consume(j):
        start = pl.multiple_of(j * tk, tk)
        for hh, (_, _, v_ref) in enumerate(heads):
            g = g_ref[hh]
            m_old = m_ref[hh]
            m_new = jnp.maximum(m_old, jnp.max(g, axis=-1, keepdims=True))
            p = jnp.exp2(g - m_new)
            alpha = jnp.exp2(m_old - m_new)
            acc_ref[hh] = alpha * acc_ref[hh] + jnp.dot(p.astype(BF16), v_ref[pl.ds(start, tk), :],
                                                        preferred_element_type=F32)
            m_ref[hh] = m_new

    for hh, g in enumerate(scores(diag, True)):
        g_ref[hh] = g

    def body(c):
        t, _ = c
        cur = diag - t
        nxt = jnp.maximum(cur - 1, 0)
        consume(cur)
        g_next = scores(nxt, False)
        for hh, g in enumerate(g_next):
            g_ref[hh] = g
        return t + 1, jnp.logical_and(cur >= 1, reaches(nxt))

    lax.while_loop(lambda c: c[1], body, (jnp.int32(0), jnp.bool_(True)))
    outs = []
    for hh in range(group):
        acc = acc_ref[hh]
        outs.append(acc[:, :HEAD_DIM] / acc[:, HEAD_DIM:HEAD_DIM + 1])
    o_ref[...] = jnp.concatenate(outs, axis=-1)


def _fox_prompt(qh, kh, vh, kn_sq, bias_max, *, tq):
    bsz, nh, tp, wl = qh.shape
    nq = tp // tq
    nk = kn_sq.shape[1]
    tk = tp // nk
    assert tk % tq == 0
    stat = lambda a: lax.cummax(jnp.transpose(a[:, :, 0, :nh], (0, 2, 1)), axis=2).reshape(-1)
    kn_tab = stat(jnp.sqrt(kn_sq) * NORM_SLACK)
    bi_tab = stat(bias_max)
    group = ATTN_HEADS_PER_STEP
    q_specs = [pl.BlockSpec((None, None, tq, wl), lambda b, p, i, kn, bi, hh=hh: (b, group * p + hh, i, 0))
               for hh in range(group)]
    kv_specs = [pl.BlockSpec((None, None, tp, wl), lambda b, p, i, kn, bi, hh=hh: (b, group * p + hh, 0, 0))
                for hh in range(group)]
    grid_spec = pltpu.PrefetchScalarGridSpec(
        num_scalar_prefetch=2,
        grid=(bsz, nh // group, nq),
        in_specs=q_specs + kv_specs + kv_specs,
        out_specs=pl.BlockSpec((None, tq, group * HEAD_DIM), lambda b, p, i, kn, bi: (b, i, p)),
        scratch_shapes=[pltpu.VMEM((group, tq, 1), F32), pltpu.VMEM((group, tq, wl), F32),
                        pltpu.VMEM((group, tq, 1), F32), pltpu.VMEM((group, tq, tk), F32)],
    )
    return pl.pallas_call(
        functools.partial(_fox_prompt_kernel, tq=tq, tk=tk, nk=nk, n_heads=nh, group=group),
        grid_spec=grid_spec,
        out_shape=jax.ShapeDtypeStruct((bsz, tp, nh * HEAD_DIM), F32),
        compiler_params=_cparams(("arbitrary", "arbitrary", "arbitrary")),
        name="fox_prompt",
    )(kn_tab, bi_tab, *([qh] * group), *([kh] * group), *([vh] * group))


def _fox_sample_kernel(q_ref, kn_ref, vn_ref, ck_ref, cv_ref, bc_ref, bn_ref, o_ref, *, n_heads):
    tq = q_ref.shape[1]
    r = lax.broadcasted_iota(jnp.int32, (tq, tq), 0)
    c = lax.broadcasted_iota(jnp.int32, (tq, tq), 1)
    outs = []
    dn = (((1,), (1,)), ((), ()))
    for hh in range(n_heads):
        sl = slice(hh * HEAD_DIM, (hh + 1) * HEAD_DIM)
        q = q_ref[hh][:, :HEAD_DIM]
        kn = kn_ref[hh][:, :HEAD_DIM]
        vn = vn_ref[hh][:, :HEAD_DIM]
        kc = ck_ref[:, sl].astype(BF16)
        vc = cv_ref[:, sl].astype(BF16)
        g_c = lax.dot_general(q, kc, dn, preferred_element_type=F32) + bc_ref[hh]
        g_n = lax.dot_general(q, kn, dn, preferred_element_type=F32) + bn_ref[hh]
        g_n = jnp.where(c <= r, g_n, -jnp.inf)
        m = jnp.maximum(jnp.max(g_c, axis=-1, keepdims=True), jnp.max(g_n, axis=-1, keepdims=True))
        p_c = jnp.exp2(g_c - m)
        p_n = jnp.exp2(g_n - m)
        den = jnp.sum(p_c, axis=-1, keepdims=True) + jnp.sum(p_n, axis=-1, keepdims=True)
        num = jnp.dot(p_c.astype(BF16), vc, preferred_element_type=F32)
        num = num + jnp.dot(p_n.astype(BF16), vn, preferred_element_type=F32)
        outs.append(num / den)
    o_ref[...] = jnp.concatenate(outs, axis=-1)


def _fox_sample(qh, kh, vh, cache_k, cache_v, bias_cache, bias_new, *, tq):
    _, nh, _, wl = qh.shape
    bsz, past, _ = cache_k.shape
    w = nh * HEAD_DIM
    b4 = lambda b: (b, 0, 0, 0)
    b3 = lambda b: (b, 0, 0)
    rows = lambda b: (0, 0, b, 0)
    return pl.pallas_call(
        functools.partial(_fox_sample_kernel, n_heads=nh),
        grid=(bsz,),
        in_specs=[pl.BlockSpec((None, nh, tq, wl), rows), pl.BlockSpec((None, nh, tq, wl), rows),
                  pl.BlockSpec((None, nh, tq, wl), rows),
                  pl.BlockSpec((None, past, w), b3), pl.BlockSpec((None, past, w), b3),
                  pl.BlockSpec((None, nh, 1, past), b4), pl.BlockSpec((None, nh, 1, tq), b4)],
        out_specs=pl.BlockSpec((None, tq, w), b3),
        out_shape=jax.ShapeDtypeStruct((bsz, tq, w), F32),
        compiler_params=_cparams(("arbitrary",)),
        name="fox_sample",
    )(qh, kh, vh, cache_k, cache_v, bias_cache, bias_new)


def _cumsum_kernel(x_ref, o_ref, carry_ref, *, tm):
    i = pl.program_id(1)

    @pl.when(i == 0)
    def _():
        carry_ref[...] = jnp.zeros_like(carry_ref)

    fc = _prefix_sum_rows(x_ref[...]) + carry_ref[0:1, :]
    o_ref[...] = fc
    carry_ref[0:1, :] = fc[tm - 1:tm, :]


def _cumsum_time(x, *, tm):
    bsz, t, w = x.shape
    row = lambda b, i: (b, i, 0)
    return pl.pallas_call(
        functools.partial(_cumsum_kernel, tm=tm),
        grid=(bsz, t // tm),
        in_specs=[pl.BlockSpec((None, tm, w), row)],
        out_specs=pl.BlockSpec((None, tm, w), row),
        out_shape=jax.ShapeDtypeStruct((bsz, t, w), F32),
        scratch_shapes=[pltpu.VMEM((8, w), F32)],
        compiler_params=_cparams(("arbitrary", "arbitrary")),
        name="cumsum_time",
    )(x)


def _merge_router_kernel(*refs, t_valid, tm, n_meta):
    n_x = 3 if n_meta else 1
    (ys_ref, ya_ref, gs_ref, ga_ref, wo_ref, gf_ref, wr_ref, br_ref, cnt_in_ref,
     x2_ref, ti_ref, tw_ref, tp_ref, cnt_out_ref, cnt_ref) = refs[n_x:]
    first = (pl.program_id(0) == 0) & (pl.program_id(1) == 0)

    @pl.when(first)
    def _():
        cnt_ref[...] = jnp.broadcast_to(cnt_in_ref[...], cnt_ref.shape)

    mix = jnp.concatenate([_rms(ys_ref[...], gs_ref[...]), _rms(ya_ref[...], ga_ref[...])], axis=-1)
    x = _timeline_rows(refs[:n_x], pl.program_id(1), n_meta)
    x2 = x + jnp.dot(mix.astype(BF16), wo_ref[...], preferred_element_type=F32)
    x2_ref[...] = x2
    h2 = _rms(x2, gf_ref[...])
    h_hi = h2.astype(BF16)
    h_lo = (h2 - h_hi.astype(F32)).astype(BF16)
    logits = (jnp.dot(h_hi, wr_ref[0], preferred_element_type=F32)
              + jnp.dot(h_lo, wr_ref[0], preferred_element_type=F32)
              + jnp.dot(h_hi, wr_ref[1], preferred_element_type=F32)) + br_ref[...]

    lane = lax.broadcasted_iota(jnp.int32, (tm, LANES), 1)
    rows = pl.program_id(1) * tm + lax.broadcasted_iota(jnp.int32, (tm, 1), 0)
    valid = rows < t_valid
    vals = logits
    top_v, top_i, sels = [], [], []
    for _ in range(TOP_K):
        mx = jnp.max(vals, axis=-1, keepdims=True)
        idx = jnp.min(jnp.where(vals == mx, lane, LANES), axis=-1, keepdims=True)
        sel = lane == idx
        vals = jnp.where(sel, -jnp.inf, vals)
        top_v.append(mx)
        top_i.append(idx)
        sels.append(sel)
    ex = [jnp.exp(v - top_v[0]) for v in top_v]
    den = ex[0] + ex[1] + ex[2] + ex[3]

    member = jnp.zeros((tm, LANES), F32)
    for sel in sels:
        member = member + jnp.where(valid, jnp.where(sel, 1.0, 0.0), 0.0)
    r = lax.broadcasted_iota(jnp.int32, (tm, tm), 0)
    c = lax.broadcasted_iota(jnp.int32, (tm, tm), 1)
    tri = (c < r).astype(BF16)
    before = jnp.dot(tri, member.astype(BF16), preferred_element_type=F32) + cnt_ref[0:1, :]
    cnt_new = cnt_ref[0:1, :] + jnp.sum(member, axis=0, keepdims=True)
    cnt_ref[0:1, :] = cnt_new
    cnt_out_ref[...] = cnt_new

    ti = jnp.zeros((tm, LANES), jnp.int32)
    tw = jnp.zeros((tm, LANES), F32)
    tpos = jnp.zeros((tm, LANES), F32)
    for kk in range(TOP_K):
        pos = jnp.sum(jnp.where(sels[kk], before, 0.0), axis=-1, keepdims=True)
        ti = jnp.where(lane == kk, top_i[kk], ti)
        tw = jnp.where(lane == kk, ex[kk] / den, tw)
        tpos = jnp.where(lane == kk, pos, tpos)
    ti_ref[...] = ti
    tw_ref[...] = tw
    tp_ref[...] = tpos.astype(jnp.int32)


def _merge_router(x, ys, ya, g_ssm, g_attn, w_out, g_ffn, w_router, b_router, cnt_in, *, tm, meta=None):
    bsz, _, d = x.shape
    x_ops, x_specs, t = _timeline_specs(x, meta, tm)
    w = ys.shape[-1]
    nt = pl.cdiv(t, tm)
    row = lambda b, i: (b, i, 0)
    c2 = lambda b, i: (0, 0)
    return pl.pallas_call(
        functools.partial(_merge_router_kernel, t_valid=t, tm=tm, n_meta=0 if meta is None else meta.shape[0]),
        grid=(bsz, nt),
        in_specs=x_specs + [
            pl.BlockSpec((None, tm, w), row),
            pl.BlockSpec((None, tm, w), row),
            pl.BlockSpec((1, w), c2), pl.BlockSpec((1, w), c2),
            pl.BlockSpec(w_out.shape, c2),
            pl.BlockSpec((1, d), c2),
            pl.BlockSpec(w_router.shape, lambda b, i: (0, 0, 0)),
            pl.BlockSpec((1, LANES), c2),
            pl.BlockSpec((1, LANES), c2),
        ],
        out_specs=[
            pl.BlockSpec((None, tm, d), row),
            pl.BlockSpec((None, tm, LANES), row),
            pl.BlockSpec((None, tm, LANES), row),
            pl.BlockSpec((None, tm, LANES), row),
            pl.BlockSpec((1, LANES), c2),
        ],
        out_shape=[
            jax.ShapeDtypeStruct((bsz, t, d), F32),
            jax.ShapeDtypeStruct((bsz, t, LANES), jnp.int32),
            jax.ShapeDtypeStruct((bsz, t, LANES), F32),
            jax.ShapeDtypeStruct((bsz, t, LANES), jnp.int32),
            jax.ShapeDtypeStruct((1, LANES), F32),
        ],
        scratch_shapes=[pltpu.VMEM((8, LANES), F32)],
        compiler_params=_cparams(("arbitrary", "arbitrary")),
        name="merge_router",
    )(*x_ops, ys, ya, g_ssm, g_attn, w_out, g_ffn, w_router, b_router, cnt_in)


def _dispatch_kernel(last_tile_ref, nu_ref, slots_ref, xa_ref, xb_ref, g_ref, xs_hbm, hbuf_ref, zero_ref, zsem, sem,
                     *, t_valid, nt_a, n_steps_a, tm, n_experts, n_tiles, tile):
    step = pl.program_id(0)

    def zero_tile(t):
        return pltpu.make_async_copy(zero_ref, xs_hbm.at[pl.ds(pl.multiple_of(t * tile, tile), tile)], zsem)

    @pl.when(step == 0)
    def _():
        zero_ref[...] = jnp.zeros_like(zero_ref)

        def start_tail(t, c):
            zero_tile(t).start()
            return c

        def wait_tail(t, c):
            zero_tile(t).wait()
            return c

        lax.fori_loop(nu_ref[0], n_tiles, start_tail, 0)
        lax.fori_loop(nu_ref[0], n_tiles, wait_tail, 0)
        for e in range(n_experts):
            lt = last_tile_ref[e]

            @pl.when(lt >= 0)
            def _():
                zero_tile(lt).start()
        for e in range(n_experts):
            lt = last_tile_ref[e]

            @pl.when(lt >= 0)
            def _():
                zero_tile(lt).wait()

    in_a = step < n_steps_a
    cur = step % 2

    @pl.when(in_a)
    def _():
        hbuf_ref[cur] = _rms(xa_ref[...], g_ref[...])

    @pl.when(jnp.logical_not(in_a))
    def _():
        hbuf_ref[cur] = _rms(xb_ref[...], g_ref[...])

    def issue(r, c):
        for kk in range(TOP_K):
            pltpu.make_async_copy(hbuf_ref.at[cur, pl.ds(r, 1)], xs_hbm.at[pl.ds(slots_ref[r * TOP_K + kk], 1)],
                                  sem.at[cur]).start()
        return c

    def valid_rows(s):
        return jnp.where(s < n_steps_a, jnp.minimum(tm, t_valid - (s % nt_a) * tm), tm)

    def wait_rows(buf, n_rows):
        @pl.when(n_rows == tm)
        def _():
            for _ in range(TOP_K):
                pltpu.make_async_copy(hbuf_ref.at[buf], xs_hbm.at[pl.ds(0, tm)], sem.at[buf]).wait()

        @pl.when(n_rows < tm)
        def _():
            def one(r, c):
                for _ in range(TOP_K):
                    pltpu.make_async_copy(hbuf_ref.at[buf, pl.ds(0, 1)], xs_hbm.at[pl.ds(0, 1)], sem.at[buf]).wait()
                return c
            lax.fori_loop(0, n_rows, one, 0)

    n_valid = valid_rows(step)

    @pl.when(n_valid == tm)
    def _():
        lax.fori_loop(0, tm, issue, 0, unroll=ISSUE_UNROLL)

    @pl.when(n_valid < tm)
    def _():
        lax.fori_loop(0, n_valid, issue, 0)

    @pl.when(step > 0)
    def _():
        wait_rows(1 - cur, valid_rows(step - 1))

    @pl.when(step == pl.num_programs(0) - 1)
    def _():
        wait_rows(cur, n_valid)


def _dispatch(xa, xb, g_ffn, slots_flat, last_tile, n_used, *, n_slots, tm):
    bsz, t, d = xa.shape
    nt_a = pl.cdiv(t, tm)
    n_steps_a = bsz * nt_a
    n_steps_b = xb.shape[0] // tm
    n_experts = last_tile.shape[0]
    grid_spec = pltpu.PrefetchScalarGridSpec(
        num_scalar_prefetch=2,
        grid=(n_steps_a + n_steps_b,),
        in_specs=[
            pl.BlockSpec((tm * TOP_K,), lambda s, lt, nu: (s,), memory_space=pltpu.SMEM),
            pl.BlockSpec((None, tm, d), lambda s, lt, nu: (jnp.minimum(s, n_steps_a - 1) // nt_a,
                                                          jnp.minimum(s, n_steps_a - 1) % nt_a, 0)),
            pl.BlockSpec((tm, d), lambda s, lt, nu: (jnp.maximum(s - n_steps_a, 0), 0)),
            pl.BlockSpec((1, d), lambda s, lt, nu: (0, 0)),
        ],
        out_specs=pl.BlockSpec(memory_space=pl.ANY),
        scratch_shapes=[pltpu.VMEM((2, tm, d), F32), pltpu.VMEM((EXPERT_TILE, d), F32),
                        pltpu.SemaphoreType.DMA, pltpu.SemaphoreType.DMA((2,))],
    )
    return pl.pallas_call(
        functools.partial(_dispatch_kernel, t_valid=t, nt_a=nt_a, n_steps_a=n_steps_a, tm=tm, n_experts=n_experts,
                          n_tiles=n_slots // EXPERT_TILE, tile=EXPERT_TILE),
        grid_spec=grid_spec,
        out_shape=jax.ShapeDtypeStruct((n_slots, d), F32),
        compiler_params=_cparams(("arbitrary",)),
        name="moe_dispatch",
    )(last_tile, n_used, slots_flat, xa, xb, g_ffn)


def _expert_kernel(te_ref, nu_ref, tpe_ref, x_ref, b1_ref, b2_ref, w1_hbm, w2_hbm, y_ref,
                   w1f_ref, w2f_ref, w1b_ref, w2b_ref, grp_ref, sem1, sem2, *, d_ff):
    i = pl.program_id(0)
    used = nu_ref[0]
    e = te_ref[i]
    prev = te_ref[jnp.maximum(i - 1, 0)]
    changed = (i == 0) | (e != prev)

    def fetch(expert, slot):
        return (pltpu.make_async_copy(w1_hbm.at[expert], w1f_ref.at[slot], sem1.at[slot]),
                pltpu.make_async_copy(w2_hbm.at[expert], w2f_ref.at[slot], sem2.at[slot]))

    @pl.when(i == 0)
    def _():
        grp_ref[0] = 0
        for cp in fetch(e, 0):
            cp.start()

    @pl.when((i < used) & changed)
    def _():
        slot = grp_ref[0] % 2
        for cp in fetch(e, slot):
            cp.wait()
        w1b_ref[...] = w1f_ref[slot].astype(BF16)
        w2b_ref[...] = w2f_ref[slot].astype(BF16)
        nxt = i + tpe_ref[e]

        @pl.when(nxt < used)
        def _():
            for cp in fetch(te_ref[nxt], 1 - slot):
                cp.start()
        grp_ref[0] = grp_ref[0] + 1

    @pl.when(i >= used)
    def _():
        y_ref[...] = jnp.zeros_like(y_ref)

    @pl.when(i < used)
    def _():
        x = x_ref[...].astype(BF16)
        h = jnp.dot(x, w1b_ref[...], preferred_element_type=F32) + b1_ref[...]
        hg = jnp.minimum(h[:, :d_ff], SWIGLU_LIMIT)
        hl = jnp.clip(h[:, d_ff:], -SWIGLU_LIMIT, SWIGLU_LIMIT)
        act = (hl + 1.0) * (hg * jax.nn.sigmoid(SWIGLU_ALPHA * hg))
        y_ref[...] = jnp.dot(act.astype(BF16), w2b_ref[...], preferred_element_type=F32) + b2_ref[...]


def _experts(xs, tile_expert, n_used, tiles_per, w1, b1, w2, b2):
    s, d = xs.shape
    n_tiles = s // EXPERT_TILE
    n_e, _, two_ff = w1.shape
    d_ff = two_ff // 2

    def xmap(i, te, nu, tpe):
        return (i, 0)

    def bmap(i, te, nu, tpe):
        return (te[i], 0, 0)

    grid_spec = pltpu.PrefetchScalarGridSpec(
        num_scalar_prefetch=3,
        grid=(n_tiles,),
        in_specs=[
            pl.BlockSpec((EXPERT_TILE, d), xmap),
            pl.BlockSpec((None, 1, two_ff), bmap),
            pl.BlockSpec((None, 1, d), bmap),
            pl.BlockSpec(memory_space=pl.ANY),
            pl.BlockSpec(memory_space=pl.ANY),
        ],
        out_specs=pl.BlockSpec((EXPERT_TILE, d), xmap),
        scratch_shapes=[pltpu.VMEM((2, d, two_ff), F32), pltpu.VMEM((2, d_ff, d), F32),
                        pltpu.VMEM((d, two_ff), BF16), pltpu.VMEM((d_ff, d), BF16),
                        pltpu.SMEM((1,), jnp.int32),
                        pltpu.SemaphoreType.DMA((2,)), pltpu.SemaphoreType.DMA((2,))],
    )
    return pl.pallas_call(
        functools.partial(_expert_kernel, d_ff=d_ff),
        grid_spec=grid_spec,
        out_shape=jax.ShapeDtypeStruct((s, d), F32),
        compiler_params=_cparams(("arbitrary",)),
        name="moe_experts",
    )(tile_expert, n_used, tiles_per, xs, b1.reshape(n_e, 1, two_ff), b2.reshape(n_e, 1, d), w1, w2)


def _combine_kernel(slots_ref, next_slots_ref, g_ref, x2_hbm, tw_hbm, ys_hbm, o_ref, xbuf_ref, wbuf_ref, buf_ref,
                    sem, lsem, *, tm, nt, row_offset):
    step = pl.program_id(0)
    cur = step % 2

    def tile_copies(s, slot):
        start = pl.multiple_of(row_offset + (s % nt) * tm, 8)
        return (pltpu.make_async_copy(x2_hbm.at[s // nt, pl.ds(start, tm)], xbuf_ref.at[slot], lsem.at[slot]),
                pltpu.make_async_copy(tw_hbm.at[s // nt, pl.ds(start, tm)], wbuf_ref.at[slot], lsem.at[slot]))

    def fetch(s, slot, idx_ref):
        for cp in tile_copies(s, slot):
            cp.start()

        def issue(r, c):
            for kk in range(TOP_K):
                pltpu.make_async_copy(ys_hbm.at[pl.ds(idx_ref[r * TOP_K + kk], 1)],
                                      buf_ref.at[slot, kk, pl.ds(r, 1)], sem.at[slot]).start()
            return c

        lax.fori_loop(0, tm, issue, 0, unroll=ISSUE_UNROLL)

    @pl.when(step == 0)
    def _():
        fetch(step, cur, slots_ref)

    @pl.when(step + 1 < pl.num_programs(0))
    def _():
        fetch(step + 1, 1 - cur, next_slots_ref)

    for cp in tile_copies(step, cur):
        cp.wait()
    for kk in range(TOP_K):
        pltpu.make_async_copy(ys_hbm.at[pl.ds(0, tm)], buf_ref.at[cur, kk], sem.at[cur]).wait()
    tw = wbuf_ref[cur]
    acc = xbuf_ref[cur]
    for kk in range(TOP_K):
        acc = acc + tw[:, kk:kk + 1] * buf_ref[cur, kk]
    o_ref[...] = _rms(acc, g_ref[...])


def _combine(x2, tw, slots_flat, ys, g_final, *, tm, row_offset):
    bsz, t, d = x2.shape
    t_out = t - row_offset
    nt = t_out // tm
    n_steps = bsz * nt
    grid_spec = pltpu.PrefetchScalarGridSpec(
        num_scalar_prefetch=0,
        grid=(n_steps,),
        in_specs=[
            pl.BlockSpec((tm * TOP_K,), lambda s: (s,), memory_space=pltpu.SMEM),
            pl.BlockSpec((tm * TOP_K,), lambda s: (jnp.minimum(s + 1, n_steps - 1),), memory_space=pltpu.SMEM),
            pl.BlockSpec((1, d), lambda s: (0, 0)),
            pl.BlockSpec(memory_space=pl.ANY),
            pl.BlockSpec(memory_space=pl.ANY),
            pl.BlockSpec(memory_space=pl.ANY),
        ],
        out_specs=pl.BlockSpec((None, tm, d), lambda s: (s // nt, s % nt, 0)),
        scratch_shapes=[pltpu.VMEM((2, tm, d), F32), pltpu.VMEM((2, tm, LANES), F32),
                        pltpu.VMEM((2, TOP_K, tm, d), F32),
                        pltpu.SemaphoreType.DMA((2,)), pltpu.SemaphoreType.DMA((2,))],
    )
    return pl.pallas_call(
        functools.partial(_combine_kernel, tm=tm, nt=nt, row_offset=row_offset),
        grid_spec=grid_spec,
        out_shape=jax.ShapeDtypeStruct((bsz, t_out, d), F32),
        compiler_params=_cparams(("arbitrary",)),
        name="moe_combine",
    )(slots_flat, slots_flat, g_final, x2, tw, ys)


def _pad_lanes(a, value=0.0):
    return jnp.pad(a, [(0, 0)] * (a.ndim - 1) + [(0, LANES - a.shape[-1])], constant_values=value)


def kernel(x_prompt, x_sample, cache_k, cache_v, cache_logf, state_ssm_re, state_ssm_im, meta_tokens, norm_mix_g, w_in, b_forget, ssm_a_re, ssm_a_im, ssm_log_dt, ssm_b_re, ssm_b_im, ssm_c_re, ssm_c_im, ssm_d, w_glu, b_glu, g_out_ssm, g_out_attn, w_out, norm_ffn_g, w_router, b_router, w_mlp1, b_mlp1, w_mlp2, b_mlp2, norm_final_g):
    depth = w_in.shape[0]
    assert depth == 1, "the routing tables below are built for a single trunk layer"
    l = 0
    bp, seq, d = x_prompt.shape
    bs, ts, _ = x_sample.shape
    n_heads = b_forget.shape[1]
    n_groups, ssm_n = ssm_a_re.shape[1:]
    n_experts = w_router.shape[2]
    w_ssm = n_groups * SSM_P
    w_attn = n_heads * HEAD_DIM
    gn = n_groups * ssm_n
    past = cache_k.shape[2]
    assert seq % ROW_TILE == 0 and (bs * ts) % ROW_TILE == 0

    meta = meta_tokens.astype(x_prompt.dtype)
    assert meta.shape[0] == N_META
    xs = x_sample
    tp = seq + N_META

    w_main = w_in[l][:, :w_ssm + 3 * w_attn].astype(BF16)
    w_f = _pad_lanes(w_in[l][:, w_ssm + 3 * w_attn:]).astype(BF16)
    b_f = _pad_lanes(b_forget[l][None, :])
    g_mix = norm_mix_g[l][None, :]
    ab_re, ab_im, bb_re, bb_im = _s5_params(ssm_a_re[l], ssm_a_im[l], ssm_log_dt[l], ssm_b_re[l], ssm_b_im[l])
    b_mat, c_mat = _s5_block_mats(bb_re, bb_im, ssm_c_re[l], ssm_c_im[l], n_groups)

    proj = functools.partial(_in_proj, n_heads=n_heads, w_ssm=w_ssm, w_attn=w_attn)
    up, kp, vp, lfp, qhp, khp, vhp, knp, bip = proj(x_prompt, g_mix, w_main, w_f, b_f, tm=ROW_TILE, meta=meta)
    n_s = bs * ts
    us, ks, vs, lfs, qhs, khs, vhs, _, _ = proj(xs.reshape(1, n_s, d), g_mix, w_main, w_f, b_f, tm=n_s)
    us = us.reshape(bs, ts, w_ssm)
    lfs = lfs.reshape(bs, ts, n_heads)

    s5 = functools.partial(_s5_mixer, ab_re=ab_re, ab_im=ab_im, b_mat=b_mat, c_mat=c_mat,
                           d_skip=ssm_d[l].reshape(1, w_ssm), w_glu=w_glu[l].astype(BF16), b_glu=b_glu[l][None, :])
    zeros_state = jnp.zeros((bp, 1, gn), F32)
    ysp, hrp, hip = s5(up, zeros_state, zeros_state, tm=ROW_TILE, nb=bp)
    yss, hrs, his = s5(us, state_ssm_re[l].reshape(bs, 1, gn), state_ssm_im[l].reshape(bs, 1, gn), tm=ts, nb=1)

    yap = _fox_prompt(qhp, khp, vhp, knp, bip, tq=ATTN_Q_TILE)
    t_all = past + ts
    assert bs * n_heads == LANES
    tm_cum = max(m for m in range(8, ROW_TILE + 1, 8) if t_all % m == 0)
    lf_all = jnp.concatenate([cache_logf[l].astype(F32), lfs], axis=1)
    fcs = _cumsum_time(jnp.transpose(lf_all, (1, 0, 2)).reshape(1, t_all, LANES), tm=tm_cum)
    bias_s = -LOG2E * jnp.transpose(fcs.reshape(t_all, bs, n_heads), (1, 2, 0))[:, :, None, :]
    yas = _fox_sample(qhs, khs, vhs, cache_k[l].reshape(bs, past, w_attn), cache_v[l].reshape(bs, past, w_attn),
                      bias_s[..., :past], bias_s[..., past:], tq=ts)

    wr = _pad_lanes(w_router[l])
    wr_hi = wr.astype(BF16)
    wr_parts = jnp.stack([wr_hi, (wr - wr_hi.astype(F32)).astype(BF16)])
    merge = functools.partial(_merge_router, g_ssm=g_out_ssm[l][None, :], g_attn=g_out_attn[l][None, :],
                              w_out=w_out[l].astype(BF16), g_ffn=norm_ffn_g[l][None, :],
                              w_router=wr_parts, b_router=_pad_lanes(b_router[l][None, :], value=-1e30))
    x2p, tip, twp, tpp, cnt_p = merge(x_prompt, ysp, yap, cnt_in=jnp.zeros((1, LANES), F32), tm=ROW_TILE, meta=meta)
    x2s, tis, tws, tps, cnt = merge(xs.reshape(1, n_s, d), yss.reshape(1, n_s, w_ssm), yas.reshape(1, n_s, w_attn),
                                    cnt_in=cnt_p, tm=n_s)

    n_tok = bp * tp + bs * ts
    counts = cnt[0, :n_experts].astype(jnp.int32)
    tiles_per = (counts + EXPERT_TILE - 1) // EXPERT_TILE
    tile_end = jnp.cumsum(tiles_per)
    tile_start = tile_end - tiles_per
    n_tiles = (n_tok * TOP_K) // EXPERT_TILE + n_experts
    n_slots = n_tiles * EXPERT_TILE
    n_used = tile_end[-1:].astype(jnp.int32)
    tile_ids = jnp.minimum(jnp.arange(n_tiles, dtype=jnp.int32), n_used - 1)
    tile_expert = jnp.sum((tile_end[None, :] <= tile_ids[:, None]).astype(jnp.int32), axis=1)
    last_tile = jnp.where(tiles_per > 0, tile_end - 1, -1).astype(jnp.int32)
    slot_base = (tile_start * EXPERT_TILE).astype(F32)

    def slots_of(ids, pos):
        onehot = jax.nn.one_hot(ids[..., :TOP_K], n_experts, dtype=F32)
        base = jnp.einsum('btke,e->btk', onehot, slot_base, precision=lax.Precision.HIGHEST)
        return base.astype(jnp.int32) + pos[..., :TOP_K]

    slots_p = slots_of(tip, tpp)
    slots_s = slots_of(tis, tps)
    tpad = pl.cdiv(tp, ROW_TILE) * ROW_TILE
    slots_disp = jnp.concatenate([jnp.pad(slots_p, ((0, 0), (0, tpad - tp), (0, 0))).reshape(-1),
                                  slots_s.reshape(-1)])
    xs_sorted = _dispatch(x2p, x2s.reshape(bs * ts, d), norm_ffn_g[l][None, :], slots_disp, last_tile, n_used,
                          n_slots=n_slots, tm=ROW_TILE)
    ys_sorted = _experts(xs_sorted, tile_expert, n_used, tiles_per.astype(jnp.int32),
                         w_mlp1[l], b_mlp1[l], w_mlp2[l], b_mlp2[l])

    g_fin = norm_final_g[None, :]
    y_prompt = _combine(x2p, twp, slots_p[:, N_META:].reshape(-1), ys_sorted, g_fin, tm=ROW_TILE, row_offset=N_META)
    y_sample = _combine(x2s.reshape(1, bs * ts, d), tws.reshape(1, bs * ts, LANES), slots_s.reshape(-1), ys_sorted,
                        g_fin, tm=ROW_TILE, row_offset=0).reshape(bs, ts, d)

    hd = HEAD_DIM
    st = lambda a, b: a.reshape(1, b, n_groups, ssm_n)
    return (y_prompt, y_sample,
            kp.reshape(1, bp, tp, n_heads, hd), vp.reshape(1, bp, tp, n_heads, hd), lfp[None],
            st(hrp, bp), st(hip, bp),
            ks.reshape(1, bs, ts, n_heads, hd), vs.reshape(1, bs, ts, n_heads, hd), lfs[None],
            st(hrs, bs), st(his, bs))
```
